```python
import math
import jax, jax.numpy as jnp
from jax import lax
import numpy as np

D_MODEL = 1024
BATCH = 2
SEQ = 8192
DEPTH = 1
DEC_BATCH = 8
DEC_SEQ = 32
PAST_LEN = 1024

CHUNK = 64
N_PREV_CHUNKS = 8
BAND_CHUNKS = N_PREV_CHUNKS + 1
MIX_WIDTH = D_MODEL
A_WIDTH = MIX_WIDTH // 2
A_HEAD_DIM = 64
A_HEADS = A_WIDTH // A_HEAD_DIM
MAX_REL = 128
N_REL = 2 * MAX_REL + 1
B_WIDTH = MIX_WIDTH - A_WIDTH
B_HEAD_DIM = 64
B_HEADS = B_WIDTH // (2 * B_HEAD_DIM)
B_V_DIM = 2 * B_HEAD_DIM
ROPE_THETA = 500000.0
ROPE_DIM = B_HEAD_DIM // 4
EPS = 1e-6
Q_BLOCK = 128
IN_COLS = 4 * A_WIDTH + 4 * B_WIDTH
IN_SPLITS = tuple(int(v) for v in np.cumsum([A_WIDTH] * 4 + [B_WIDTH] * 4)[:-1])

kernel_name = "hybrid_chunkband_diffattn_stream_step"


def rmsnorm(x, g):
    xf = x.astype(jnp.float32)
    y = xf * lax.rsqrt(jnp.mean(xf * xf, axis=-1, keepdims=True) + EPS)
    return (y * g.astype(jnp.float32)).astype(x.dtype)


def partial_rope(x, pos):
    half = ROPE_DIM // 2
    inv = ROPE_THETA ** (-jnp.arange(0, ROPE_DIM, 2, dtype=jnp.float32) / ROPE_DIM)
    ang = pos.astype(jnp.float32)[:, None] * inv[None, :]
    cos = jnp.cos(ang)[:, None, None, :]
    sin = jnp.sin(ang)[:, None, None, :]
    xr = x[..., :ROPE_DIM].astype(jnp.float32)
    x1, x2 = xr[..., :half], xr[..., half:]
    rot = jnp.concatenate([x1 * cos - x2 * sin, x2 * cos + x1 * sin], axis=-1)
    return jnp.concatenate([rot.astype(x.dtype), x[..., ROPE_DIM:]], axis=-1)


def split_proj(h, w_in):
    z = jnp.einsum('bsd,dc->bsc', h, w_in)
    b, s, _ = z.shape
    qa, ka, va, ga, qb, kb, vb, gb = jnp.split(z, IN_SPLITS, axis=-1)
    qa = qa.reshape(b, s, A_HEADS, A_HEAD_DIM)
    ka = ka.reshape(b, s, A_HEADS, A_HEAD_DIM)
    va = va.reshape(b, s, A_HEADS, A_HEAD_DIM)
    qb = qb.reshape(b, s, B_HEADS, 2, B_HEAD_DIM)
    kb = kb.reshape(b, s, B_HEADS, 2, B_HEAD_DIM)
    vb = vb.reshape(b, s, B_HEADS, B_V_DIM)
    return qa, ka, va, ga, qb, kb, vb, gb


def rel_index(rel):
    return jnp.clip(rel, -MAX_REL, MAX_REL) + MAX_REL


def band_attention_prompt(q, k, v, rel_bias):
    b, s, h, d = q.shape
    nc = s // CHUNK
    pad = N_PREV_CHUNKS * CHUNK
    kp = jnp.pad(k, ((0, 0), (pad, 0), (0, 0), (0, 0))).reshape(b, nc + N_PREV_CHUNKS, CHUNK, h, d)
    vp = jnp.pad(v, ((0, 0), (pad, 0), (0, 0), (0, 0))).reshape(b, nc + N_PREV_CHUNKS, CHUNK, h, d)
    kband = jnp.concatenate([kp[:, j:j + nc] for j in range(BAND_CHUNKS)], axis=2)
    vband = jnp.concatenate([vp[:, j:j + nc] for j in range(BAND_CHUNKS)], axis=2)
    qc = q.reshape(b, nc, CHUNK, h, d)
    scores = jnp.einsum('bcqhd,bckhd->bchqk', qc, kband,
                        preferred_element_type=jnp.float32) * (d ** -0.5)
    band_len = BAND_CHUNKS * CHUNK
    i = jnp.arange(CHUNK)
    j = jnp.arange(band_len)
    rel = i[:, None] - j[None, :] + pad
    bias = rel_bias.astype(jnp.float32)[:, rel_index(rel)]
    k_pos = jnp.arange(nc)[:, None] * CHUNK - pad + j[None, :]
    valid = (k_pos >= 0)[None, :, None, None, :]
    scores = jnp.where(valid, scores + bias[None, None], -jnp.inf)
    p = jax.nn.softmax(scores, axis=-1)
    out = jnp.einsum('bchqk,bckhd->bcqhd', p.astype(v.dtype), vband)
    return out.reshape(b, s, h * d)


def band_attention_sample(q, k_new, v_new, k_cache, v_cache, rel_bias):
    b, t, h, d = q.shape
    L = k_cache.shape[1]
    keys = jnp.concatenate([k_cache.astype(k_new.dtype), k_new], axis=1)
    vals = jnp.concatenate([v_cache.astype(v_new.dtype), v_new], axis=1)
    scores = jnp.einsum('bqhd,bkhd->bhqk', q, keys,
                        preferred_element_type=jnp.float32) * (d ** -0.5)
    rel = (L + jnp.arange(t))[:, None] - jnp.arange(L + t)[None, :]
    bias = rel_bias.astype(jnp.float32)[:, rel_index(rel)]
    p = jax.nn.softmax(scores + bias[None], axis=-1)
    out = jnp.einsum('bhqk,bkhd->bqhd', p.astype(vals.dtype), vals).reshape(b, t, h * d)
    return out, keys[:, -L:], vals[:, -L:]


def diff_lambda(lq1, lk1, lq2, lk2, lambda_init):
    f32 = lambda a: a.astype(jnp.float32)
    return (jnp.exp(jnp.sum(f32(lq1) * f32(lk1))) - jnp.exp(jnp.sum(f32(lq2) * f32(lk2)))
            + lambda_init)


def diff_core(q, k, v, mask, lam):
    d = q.shape[-1]
    scores = jnp.einsum('bqhmd,bkhmd->bhmqk', q, k,
                        preferred_element_type=jnp.float32) * (d ** -0.5)
    scores = jnp.where(mask, scores, -jnp.inf)
    p = jax.nn.softmax(scores, axis=-1)
    attn = p[:, :, 0] - lam * p[:, :, 1]
    return jnp.einsum('bhqk,bkhe->bqhe', attn.astype(v.dtype), v)


def diff_attention_prompt(q, k, v, lam):
    b, s, h, _, d = q.shape
    nblk = s // Q_BLOCK
    qblocks = q.reshape(b, nblk, Q_BLOCK, h, 2, d).transpose(1, 0, 2, 3, 4, 5)
    k_chunk = jnp.arange(s) // CHUNK

    def one_block(args):
        qblk, idx = args
        q_chunk = (idx * Q_BLOCK + jnp.arange(Q_BLOCK)) // CHUNK
        mask = k_chunk[None, :] <= q_chunk[:, None]
        return diff_core(qblk, k, v, mask, lam)

    out = lax.map(one_block, (qblocks, jnp.arange(nblk)))
    return out.transpose(1, 0, 2, 3, 4).reshape(b, s, h, v.shape[-1])


def diff_attention_sample(q, k_new, v_new, k_cache, v_cache, lam):
    t = q.shape[1]
    keys = jnp.concatenate([k_cache.astype(k_new.dtype), k_new], axis=1)
    vals = jnp.concatenate([v_cache.astype(v_new.dtype), v_new], axis=1)
    mask = jnp.ones((t, keys.shape[1]), dtype=bool)
    return diff_core(q, keys, vals, mask, lam)


def diff_subnorm(o, g, lambda_init):
    b, s = o.shape[0], o.shape[1]
    return (rmsnorm(o, g) * (1.0 - lambda_init)).reshape(b, s, B_WIDTH)


def merge_out(oa, ga, ob, gb, w_out):
    o = jnp.concatenate([oa * jax.nn.silu(ga), ob * jax.nn.silu(gb)], axis=-1)
    return jnp.einsum('bsc,cd->bsd', o, w_out)


def setup_inputs(seed: int = 0) -> dict:
    key = jax.random.key(seed)
    ks = jax.random.split(key, 17)
    a_keep = min(N_PREV_CHUNKS * CHUNK, PAST_LEN)
    nrm = lambda k, shape, sc: jax.random.normal(k, shape, jnp.float32) * sc
    return {
        'x_prompt': nrm(ks[0], (BATCH, SEQ, D_MODEL), 1.0),
        'x_sample': nrm(ks[1], (DEC_BATCH, DEC_SEQ, D_MODEL), 1.0),
        'cache_a_k': nrm(ks[2], (DEPTH, DEC_BATCH, a_keep, A_HEADS, A_HEAD_DIM), 1.0),
        'cache_a_v': nrm(ks[3], (DEPTH, DEC_BATCH, a_keep, A_HEADS, A_HEAD_DIM), 1.0),
        'cache_b_k': nrm(ks[4], (DEPTH, DEC_BATCH, PAST_LEN, B_HEADS, 2, B_HEAD_DIM), 1.0),
        'cache_b_v': nrm(ks[5], (DEPTH, DEC_BATCH, PAST_LEN, B_HEADS, B_V_DIM), 1.0),
        'norm_gain': 1.0 + nrm(ks[6], (DEPTH, D_MODEL), 0.05),
        'w_in': nrm(ks[7], (DEPTH, D_MODEL, IN_COLS), D_MODEL ** -0.5),
        'w_out': nrm(ks[8], (DEPTH, MIX_WIDTH, D_MODEL), MIX_WIDTH ** -0.5),
        'rel_bias': nrm(ks[9], (DEPTH, A_HEADS, N_REL), 0.1),
        'lambda_q1': nrm(ks[10], (DEPTH, B_HEAD_DIM), 0.1),
        'lambda_k1': nrm(ks[11], (DEPTH, B_HEAD_DIM), 0.1),
        'lambda_q2': nrm(ks[12], (DEPTH, B_HEAD_DIM), 0.1),
        'lambda_k2': nrm(ks[13], (DEPTH, B_HEAD_DIM), 0.1),
        'subln_gain': 1.0 + nrm(ks[14], (DEPTH, B_V_DIM), 0.05),
        'final_gain': 1.0 + nrm(ks[15], (D_MODEL,), 0.05),
    }


def reference(x_prompt, x_sample, cache_a_k, cache_a_v, cache_b_k, cache_b_v,
              norm_gain, w_in, w_out, rel_bias, lambda_q1, lambda_k1, lambda_q2, lambda_k2,
              subln_gain, final_gain):
    s_prompt = x_prompt.shape[1]
    t_sample = x_sample.shape[1]
    past_len = cache_b_k.shape[2]
    a_keep_prompt = min(N_PREV_CHUNKS * CHUNK, s_prompt)
    pos_prompt = jnp.arange(s_prompt)
    pos_sample = past_len + jnp.arange(t_sample)
    yp, ys = x_prompt, x_sample
    akp, avp, bkp, bvp, aks, avs, bks, bvs = [], [], [], [], [], [], [], []
    for l in range(DEPTH):
        lambda_init = 0.8 - 0.6 * math.exp(-0.3 * l)
        lam = diff_lambda(lambda_q1[l], lambda_k1[l], lambda_q2[l], lambda_k2[l], lambda_init)
        h = rmsnorm(yp, norm_gain[l])
        qa, ka, va, ga, qb, kb, vb, gb = split_proj(h, w_in[l])
        qb = partial_rope(qb, pos_prompt)
        kb = partial_rope(kb, pos_prompt)
        oa = band_attention_prompt(qa, ka, va, rel_bias[l])
        ob = diff_subnorm(diff_attention_prompt(qb, kb, vb, lam), subln_gain[l], lambda_init)
        yp = yp + merge_out(oa, ga, ob, gb, w_out[l])
        akp.append(ka[:, s_prompt - a_keep_prompt:])
        avp.append(va[:, s_prompt - a_keep_prompt:])
        bkp.append(kb)
        bvp.append(vb)
        h = rmsnorm(ys, norm_gain[l])
        qa, ka, va, ga, qb, kb, vb, gb = split_proj(h, w_in[l])
        qb = partial_rope(qb, pos_sample)
        kb = partial_rope(kb, pos_sample)
        oa, ka_buf, va_buf = band_attention_sample(qa, ka, va, cache_a_k[l], cache_a_v[l], rel_bias[l])
        ob = diff_subnorm(diff_attention_sample(qb, kb, vb, cache_b_k[l], cache_b_v[l], lam),
                          subln_gain[l], lambda_init)
        ys = ys + merge_out(oa, ga, ob, gb, w_out[l])
        aks.append(ka_buf)
        avs.append(va_buf)
        bks.append(kb)
        bvs.append(vb)
    y_prompt = rmsnorm(yp, final_gain)
    y_sample = rmsnorm(ys, final_gain)
    return (y_prompt, y_sample,
            jnp.stack(akp), jnp.stack(avp), jnp.stack(bkp), jnp.stack(bvp),
            jnp.stack(aks), jnp.stack(avs), jnp.stack(bks), jnp.stack(bvs))
```

```python
import functools
import math

import numpy as np
import jax
import jax.numpy as jnp
from jax import lax
from jax.experimental import pallas as pl
from jax.experimental.pallas import tpu as pltpu

F32 = jnp.float32
BF16 = jnp.bfloat16
NEG_INF = float("-inf")

CHUNK = 64
N_PREV_CHUNKS = 8
A_HEADS = 8
A_HEAD_DIM = 64
A_WIDTH = A_HEADS * A_HEAD_DIM
B_HEADS = 4
B_HEAD_DIM = 64
B_V_DIM = 2 * B_HEAD_DIM
B_WIDTH = B_HEADS * B_V_DIM
SEG = 512
MAX_REL = 128
ROPE_THETA = 500000.0
ROPE_DIM = 16
EPS = 1e-6
LANES = 128
VMEM_LIMIT = 56 * 1024 * 1024

ROW_TILE = 512
BAND_ROWS = N_PREV_CHUNKS * CHUNK
A_SUB = 128
A_WIN = BAND_ROWS + A_SUB
B_TQ = 512
B_TK = 512

NT_DIMS = (((1,), (1,)), ((), ()))


def _nt_dot(a, b):
    return lax.dot_general(a, b, NT_DIMS, preferred_element_type=F32)


def _silu(g):
    return g / (1.0 + jnp.exp(-g))


def _rope_lane_freq():
    d = np.arange(LANES) % B_HEAD_DIM
    inv = ROPE_THETA ** (-np.arange(0, ROPE_DIM, 2, dtype=np.float64) / ROPE_DIM)
    return np.where(d < ROPE_DIM, inv[d % (ROPE_DIM // 2)], 0.0)


def _rope_tables(row_pos, base_pos):
    f = _rope_lane_freq()[None, :]
    ar = np.asarray(row_pos, np.float64)[:, None] * f
    ab = np.asarray(base_pos, np.float64)[:, None] * f
    as32 = lambda a: jnp.asarray(a.astype(np.float32))
    return (as32(np.cos(ar)), as32(np.sin(ar)),
            as32(np.cos(ab))[:, None, :], as32(np.sin(ab))[:, None, :])


def _proj_kernel(x_ref, g_ref, w_ref, cr_ref, sr_ref, cb_ref, sb_ref,
                 qa_ref, ka_ref, va_ref, ga_ref, qb_ref, kb_ref, vb_ref, gb_ref,
                 kaf_ref, vaf_ref, kbf_ref, vbf_ref, *, tail_every):
    x = x_ref[...]
    inv = lax.rsqrt(jnp.mean(x * x, axis=-1, keepdims=True) + EPS)
    h = (x * inv * g_ref[...]).astype(BF16)

    def seg(k):
        return jnp.dot(h, w_ref[:, k * SEG:(k + 1) * SEG], preferred_element_type=F32)

    cb, sb = cb_ref[0], sb_ref[0]
    cr, sr = cr_ref[...], sr_ref[...]
    cos = cb * cr - sb * sr
    sin = sb * cr + cb * sr
    d = lax.broadcasted_iota(jnp.int32, cos.shape, 1) % B_HEAD_DIM
    sin_lo = jnp.where(d < ROPE_DIM // 2, -sin, 0.0)
    sin_hi = jnp.where(d >= ROPE_DIM // 2, sin, 0.0)

    def rope(z):
        cols = []
        for c in range(SEG // LANES):
            zc = z[:, c * LANES:(c + 1) * LANES]
            up = pltpu.roll(zc, LANES - ROPE_DIM // 2, 1)
            dn = pltpu.roll(zc, ROPE_DIM // 2, 1)
            cols.append(zc * cos + up * sin_lo + dn * sin_hi)
        return jnp.concatenate(cols, axis=1)

    is_tail = (pl.program_id(0) % tail_every) == tail_every - 1
    scale = A_HEAD_DIM ** -0.5

    qa_ref[...] = (seg(0) * scale).astype(BF16)
    ka = seg(1)
    ka_ref[...] = ka.astype(BF16)
    va = seg(2)
    va_ref[...] = va.astype(BF16)

    @pl.when(is_tail)
    def _():
        kaf_ref[...] = ka
        vaf_ref[...] = va

    ga_ref[...] = seg(3).astype(BF16)
    qb_ref[...] = (rope(seg(4)) * scale).astype(BF16)
    kb = rope(seg(5))
    kbf_ref[...] = kb
    kb_ref[...] = kb.astype(BF16)
    vb = seg(6)
    vbf_ref[...] = vb
    vb_ref[...] = vb.astype(BF16)
    gb_ref[...] = seg(7).astype(BF16)


def _proj(x, gain, w_bf16, tables, *, tm, tiles_per_seq, tail_every):
    rows, d_model = x.shape
    n_tiles = rows // tm
    n_tail = n_tiles // tail_every
    cr, sr, cb, sb = tables
    row_blk = lambda i: (i, 0)
    const2 = lambda i: (0, 0)
    base_blk = lambda i: (i % tiles_per_seq, 0, 0)
    tail_blk = lambda i: (i // tail_every, 0)
    bf = jax.ShapeDtypeStruct((rows, SEG), BF16)
    f32_full = jax.ShapeDtypeStruct((rows, SEG), F32)
    f32_tail = jax.ShapeDtypeStruct((n_tail * tm, SEG), F32)
    full_spec = pl.BlockSpec((tm, SEG), row_blk)
    tail_spec = pl.BlockSpec((tm, SEG), tail_blk)
    return pl.pallas_call(
        functools.partial(_proj_kernel, tail_every=tail_every),
        grid=(n_tiles,),
        in_specs=[
            pl.BlockSpec((tm, d_model), row_blk),
            pl.BlockSpec((1, d_model), const2),
            pl.BlockSpec(w_bf16.shape, const2),
            pl.BlockSpec((tm, LANES), const2),
            pl.BlockSpec((tm, LANES), const2),
            pl.BlockSpec((1, 1, LANES), base_blk),
            pl.BlockSpec((1, 1, LANES), base_blk),
        ],
        out_specs=[full_spec] * 8 + [tail_spec, tail_spec, full_spec, full_spec],
        out_shape=[bf] * 8 + [f32_tail, f32_tail, f32_full, f32_full],
        compiler_params=pltpu.CompilerParams(
            dimension_semantics=("arbitrary",), vmem_limit_bytes=VMEM_LIMIT),
        name="proj",
    )(x, gain, w_bf16, cr, sr, cb, sb)


def _band_bias_prompt(rel_bias):
    i = np.arange(A_SUB)[:, None]
    j = np.arange(A_WIN)[None, :]
    rel = i - j + BAND_ROWS
    idx = np.clip(rel, -MAX_REL, MAX_REL) + MAX_REL
    chunk_gap = (i // CHUNK + N_PREV_CHUNKS) - j // CHUNK
    band = (chunk_gap >= 0) & (chunk_gap <= N_PREV_CHUNKS)
    return jnp.where(jnp.asarray(band)[None], rel_bias.astype(F32)[:, idx], NEG_INF)


def _attn_a_prompt_kernel(q_ref, kp_ref, kc_ref, vp_ref, vc_ref, g_ref, bias_ref, o_ref,
                          kcat_ref, vcat_ref):
    n = pl.program_id(1)
    kcat_ref[0:ROW_TILE, :] = kp_ref[...]
    kcat_ref[ROW_TILE:2 * ROW_TILE, :] = kc_ref[...]
    vcat_ref[0:ROW_TILE, :] = vp_ref[...]
    vcat_ref[ROW_TILE:2 * ROW_TILE, :] = vc_ref[...]
    col = lax.broadcasted_iota(jnp.int32, (A_SUB, A_WIN), 1)
    lane = lax.broadcasted_iota(jnp.int32, (A_SUB, LANES), 1)
    low_half = lane < A_HEAD_DIM

    def sub_block(u, carry):
        r0 = pl.multiple_of(u * A_SUB, A_SUB)
        first_valid = jnp.where(n == 0, BAND_ROWS - r0, 0)
        valid = col >= first_valid
        for pair in range(A_HEADS // 2):
            cs = slice(pair * LANES, (pair + 1) * LANES)
            qp = q_ref[pl.ds(r0, A_SUB), cs]
            kp = kcat_ref[pl.ds(r0, A_WIN), cs]
            vp = vcat_ref[pl.ds(r0, A_WIN), cs]
            outs = []
            for e in range(2):
                qm = jnp.where(low_half if e == 0 else ~low_half, qp, jnp.zeros_like(qp))
                s = _nt_dot(qm, kp) + bias_ref[2 * pair + e]
                s = jnp.where(valid, s, NEG_INF)
                m = jnp.max(s, axis=-1, keepdims=True)
                p = jnp.exp(s - m)
                l = jnp.sum(p, axis=-1, keepdims=True)
                o = jnp.dot(p.astype(BF16), vp, preferred_element_type=F32)
                outs.append(o / l)
            o_pair = jnp.where(low_half, outs[0], outs[1])
            gate = g_ref[pl.ds(r0, A_SUB), cs].astype(F32)
            o_ref[pl.ds(r0, A_SUB), cs] = (o_pair * _silu(gate)).astype(BF16)
        return carry

    lax.fori_loop(0, ROW_TILE // A_SUB, sub_block, 0)


def _attn_a_prompt(qa, ka, va, ga, bias, *, batch, seq):
    tiles = seq // ROW_TILE
    cur = lambda b, n: (b * tiles + n, 0)
    prev = lambda b, n: (b * tiles + jnp.maximum(n - 1, 0), 0)
    blk = (ROW_TILE, A_WIDTH)
    return pl.pallas_call(
        _attn_a_prompt_kernel,
        grid=(batch, tiles),
        in_specs=[
            pl.BlockSpec(blk, cur),
            pl.BlockSpec(blk, prev), pl.BlockSpec(blk, cur),
            pl.BlockSpec(blk, prev), pl.BlockSpec(blk, cur),
            pl.BlockSpec(blk, cur),
            pl.BlockSpec(bias.shape, lambda b, n: (0, 0, 0)),
        ],
        out_specs=pl.BlockSpec(blk, cur),
        out_shape=jax.ShapeDtypeStruct(qa.shape, BF16),
        scratch_shapes=[pltpu.VMEM((2 * ROW_TILE, A_WIDTH), BF16),
                        pltpu.VMEM((2 * ROW_TILE, A_WIDTH), BF16)],
        compiler_params=pltpu.CompilerParams(
            dimension_semantics=("arbitrary", "arbitrary"), vmem_limit_bytes=VMEM_LIMIT),
        name="attn_a_prompt",
    )(qa, ka, ka, va, va, ga, bias)


def _diff_lambda(lq1_ref, lk1_ref, lq2_ref, lk2_ref, lambda_init):
    e1 = jnp.exp(jnp.sum(lq1_ref[...] * lk1_ref[...], axis=-1, keepdims=True))
    e2 = jnp.exp(jnp.sum(lq2_ref[...] * lk2_ref[...], axis=-1, keepdims=True))
    return e1 - e2 + lambda_init


def _stack_maps(qh):
    lane = lax.broadcasted_iota(jnp.int32, qh.shape, 1)
    zero = jnp.zeros_like(qh)
    return jnp.concatenate([jnp.where(lane < B_HEAD_DIM, qh, zero),
                            jnp.where(lane >= B_HEAD_DIM, qh, zero)], axis=0)


def _subnorm_gate(o, subg_ref, gate, lambda_init):
    o = o * lax.rsqrt(jnp.mean(o * o, axis=-1, keepdims=True) + EPS) * subg_ref[...]
    return (o * (1.0 - lambda_init)) * _silu(gate)


def _attn_b_prompt_kernel(q_ref, k_ref, v_ref, g_ref, lq1_ref, lk1_ref, lq2_ref, lk2_ref,
                          subg_ref, o_ref, *, lambda_init):
    i = pl.program_id(2)
    qs = _stack_maps(q_ref[...])
    rows = 2 * B_TQ

    def kv_step(j, carry, masked):
        m, l, acc = carry
        k0 = pl.multiple_of(j * B_TK, B_TK)
        s = _nt_dot(qs, k_ref[pl.ds(k0, B_TK), :])
        if masked:
            qc = (lax.broadcasted_iota(jnp.int32, s.shape, 0) % B_TQ) // CHUNK
            kc = lax.broadcasted_iota(jnp.int32, s.shape, 1) // CHUNK
            s = jnp.where(kc <= qc, s, NEG_INF)
        m_new = jnp.maximum(m, jnp.max(s, axis=-1, keepdims=True))
        alpha = jnp.exp(m - m_new)
        p = jnp.exp(s - m_new)
        l = alpha * l + jnp.sum(p, axis=-1, keepdims=True)
        acc = alpha * acc + jnp.dot(p.astype(BF16), v_ref[pl.ds(k0, B_TK), :],
                                    preferred_element_type=F32)
        return m_new, l, acc

    init = (jnp.full((rows, 1), NEG_INF, F32), jnp.zeros((rows, 1), F32),
            jnp.zeros((rows, B_V_DIM), F32))
    carry = lax.fori_loop(0, i, functools.partial(kv_step, masked=False), init)
    m, l, acc = kv_step(i, carry, masked=True)
    o = acc / l
    lam = _diff_lambda(lq1_ref, lk1_ref, lq2_ref, lk2_ref, lambda_init)
    o = o[:B_TQ] - lam * o[B_TQ:]
    o_ref[...] = _subnorm_gate(o, subg_ref, g_ref[...].astype(F32), lambda_init).astype(BF16)


def _attn_b_prompt(qb, kb, vb, gb, lams, subg, *, batch, seq, lambda_init):
    tiles = seq // B_TQ
    qblk = lambda b, h, i: (b * tiles + i, h)
    kvblk = lambda b, h, i: (b, h)
    vec = pl.BlockSpec((1, B_HEAD_DIM), lambda b, h, i: (0, 0))
    return pl.pallas_call(
        functools.partial(_attn_b_prompt_kernel, lambda_init=lambda_init),
        grid=(batch, B_HEADS, tiles),
        in_specs=[
            pl.BlockSpec((B_TQ, B_V_DIM), qblk),
            pl.BlockSpec((seq, B_V_DIM), kvblk),
            pl.BlockSpec((seq, B_V_DIM), kvblk),
            pl.BlockSpec((B_TQ, B_V_DIM), qblk),
            vec, vec, vec, vec,
            pl.BlockSpec((1, B_V_DIM), lambda b, h, i: (0, 0)),
        ],
        out_specs=pl.BlockSpec((B_TQ, B_V_DIM), qblk),
        out_shape=jax.ShapeDtypeStruct(qb.shape, BF16),
        compiler_params=pltpu.CompilerParams(
            dimension_semantics=("arbitrary", "arbitrary", "arbitrary"),
            vmem_limit_bytes=VMEM_LIMIT),
        name="attn_b_prompt",
    )(qb, kb, vb, gb, *lams, subg)


def _band_bias_sample(rel_bias, keep, t):
    rel = (keep + np.arange(t))[:, None] - np.arange(keep + t)[None, :]
    idx = np.clip(rel, -MAX_REL, MAX_REL) + MAX_REL
    bias = rel_bias.astype(F32)[:, idx]
    return bias[:, :, :keep], bias[:, :, keep:]


def _attn_a_sample_kernel(q_ref, kn_ref, vn_ref, knf_ref, vnf_ref, kc_ref, vc_ref, g_ref,
                          bc_ref, bn_ref, o_ref, ko_ref, vo_ref):
    t = q_ref.shape[0]
    keep = kc_ref.shape[0]
    lane = lax.broadcasted_iota(jnp.int32, (t, LANES), 1)
    low_half = lane < A_HEAD_DIM
    for pair in range(A_HEADS // 2):
        cs = slice(pair * LANES, (pair + 1) * LANES)
        qp = q_ref[:, cs]
        kc = kc_ref[:, cs].astype(BF16)
        vc = vc_ref[:, cs].astype(BF16)
        kn = kn_ref[:, cs]
        vn = vn_ref[:, cs]
        outs = []
        for e in range(2):
            qm = jnp.where(low_half if e == 0 else ~low_half, qp, jnp.zeros_like(qp))
            s_c = _nt_dot(qm, kc) + bc_ref[2 * pair + e]
            s_n = _nt_dot(qm, kn) + bn_ref[2 * pair + e]
            m = jnp.maximum(jnp.max(s_c, axis=-1, keepdims=True),
                            jnp.max(s_n, axis=-1, keepdims=True))
            p_c = jnp.exp(s_c - m)
            p_n = jnp.exp(s_n - m)
            l = jnp.sum(p_c, axis=-1, keepdims=True) + jnp.sum(p_n, axis=-1, keepdims=True)
            o = (jnp.dot(p_c.astype(BF16), vc, preferred_element_type=F32)
                 + jnp.dot(p_n.astype(BF16), vn, preferred_element_type=F32))
            outs.append(o / l)
        o_pair = jnp.where(low_half, outs[0], outs[1])
        o_ref[:, cs] = (o_pair * _silu(g_ref[:, cs].astype(F32))).astype(BF16)
    ko_ref[0:keep - t, :] = kc_ref[t:keep, :]
    ko_ref[keep - t:keep, :] = knf_ref[...]
    vo_ref[0:keep - t, :] = vc_ref[t:keep, :]
    vo_ref[keep - t:keep, :] = vnf_ref[...]


def _attn_a_sample(qa, ka, va, kaf, vaf, cache_k, cache_v, ga, bias_c, bias_n, *, batch, t):
    keep = cache_k.shape[1]
    new = pl.BlockSpec((t, A_WIDTH), lambda b: (b, 0))
    cache = pl.BlockSpec((None, keep, A_WIDTH), lambda b: (b, 0, 0))
    const3 = lambda b: (0, 0, 0)
    return pl.pallas_call(
        _attn_a_sample_kernel,
        grid=(batch,),
        in_specs=[new, new, new, new, new, cache, cache, new,
                  pl.BlockSpec(bias_c.shape, const3), pl.BlockSpec(bias_n.shape, const3)],
        out_specs=[new, cache, cache],
        out_shape=[jax.ShapeDtypeStruct(qa.shape, BF16),
                   jax.ShapeDtypeStruct(cache_k.shape, F32),
                   jax.ShapeDtypeStruct(cache_v.shape, F32)],
        compiler_params=pltpu.CompilerParams(
            dimension_semantics=("arbitrary",), vmem_limit_bytes=VMEM_LIMIT),
        name="attn_a_sample",
    )(qa, ka, va, kaf, vaf, cache_k, cache_v, ga, bias_c, bias_n)


def _attn_b_sample_kernel(q_ref, kn_ref, vn_ref, kc_ref, vc_ref, g_ref,
                          lq1_ref, lk1_ref, lq2_ref, lk2_ref, subg_ref, o_ref, *, lambda_init):
    t = q_ref.shape[0]
    lam = _diff_lambda(lq1_ref, lk1_ref, lq2_ref, lk2_ref, lambda_init)
    for h in range(B_HEADS):
        cs = slice(h * B_V_DIM, (h + 1) * B_V_DIM)
        qs = _stack_maps(q_ref[:, cs])
        s_c = _nt_dot(qs, kc_ref[:, cs].astype(BF16))
        s_n = _nt_dot(qs, kn_ref[:, cs])
        m = jnp.maximum(jnp.max(s_c, axis=-1, keepdims=True),
                        jnp.max(s_n, axis=-1, keepdims=True))
        p_c = jnp.exp(s_c - m)
        p_n = jnp.exp(s_n - m)
        l = jnp.sum(p_c, axis=-1, keepdims=True) + jnp.sum(p_n, axis=-1, keepdims=True)
        p_c = p_c / l
        p_n = p_n / l
        a_c = (p_c[:t] - lam * p_c[t:]).astype(BF16)
        a_n = (p_n[:t] - lam * p_n[t:]).astype(BF16)
        o = (jnp.dot(a_c, vc_ref[:, cs].astype(BF16), preferred_element_type=F32)
             + jnp.dot(a_n, vn_ref[:, cs], preferred_element_type=F32))
        o_ref[:, cs] = _subnorm_gate(o, subg_ref, g_ref[:, cs].astype(F32),
                                     lambda_init).astype(BF16)


def _attn_b_sample(qb, kb, vb, cache_k, cache_v, gb, lams, subg, *, batch, t, lambda_init):
    past = cache_k.shape[1]
    new = pl.BlockSpec((t, B_WIDTH), lambda b: (b, 0))
    cache = pl.BlockSpec((None, past, B_WIDTH), lambda b: (b, 0, 0))
    vec = pl.BlockSpec((1, B_HEAD_DIM), lambda b: (0, 0))
    return pl.pallas_call(
        functools.partial(_attn_b_sample_kernel, lambda_init=lambda_init),
        grid=(batch,),
        in_specs=[new, new, new, cache, cache, new, vec, vec, vec, vec,
                  pl.BlockSpec((1, B_V_DIM), lambda b: (0, 0))],
        out_specs=new,
        out_shape=jax.ShapeDtypeStruct(qb.shape, BF16),
        compiler_params=pltpu.CompilerParams(
            dimension_semantics=("arbitrary",), vmem_limit_bytes=VMEM_LIMIT),
        name="attn_b_sample",
    )(qb, kb, vb, cache_k, cache_v, gb, *lams, subg)


def _out_kernel(oa_ref, ob_ref, wa_ref, wb_ref, x_ref, fg_ref, y_ref, *, final_norm):
    y = (x_ref[...]
         + jnp.dot(oa_ref[...], wa_ref[...], preferred_element_type=F32)
         + jnp.dot(ob_ref[...], wb_ref[...], preferred_element_type=F32))
    if final_norm:
        y = y * lax.rsqrt(jnp.mean(y * y, axis=-1, keepdims=True) + EPS) * fg_ref[...]
    y_ref[...] = y


def _out_proj(oa, ob, w_a, w_b, x, final_gain, *, tm, final_norm):
    rows, d_model = x.shape
    row_blk = lambda i: (i, 0)
    const2 = lambda i: (0, 0)
    return pl.pallas_call(
        functools.partial(_out_kernel, final_norm=final_norm),
        grid=(rows // tm,),
        in_specs=[
            pl.BlockSpec((tm, A_WIDTH), row_blk),
            pl.BlockSpec((tm, B_WIDTH), row_blk),
            pl.BlockSpec(w_a.shape, const2),
            pl.BlockSpec(w_b.shape, const2),
            pl.BlockSpec((tm, d_model), row_blk),
            pl.BlockSpec((1, d_model), const2),
        ],
        out_specs=pl.BlockSpec((tm, d_model), row_blk),
        out_shape=jax.ShapeDtypeStruct(x.shape, F32),
        compiler_params=pltpu.CompilerParams(
            dimension_semantics=("arbitrary",), vmem_limit_bytes=VMEM_LIMIT),
        name="out_proj",
    )(oa, ob, w_a, w_b, x, final_gain)


def kernel(x_prompt, x_sample, cache_a_k, cache_a_v, cache_b_k, cache_b_v,
           norm_gain, w_in, w_out, rel_bias, lambda_q1, lambda_k1, lambda_q2, lambda_k2,
           subln_gain, final_gain):
    batch, seq, d_model = x_prompt.shape
    dec_batch, t_sample, _ = x_sample.shape
    depth = w_in.shape[0]
    past_len = cache_b_k.shape[2]
    a_keep = cache_a_k.shape[2]
    keep_prompt = min(BAND_ROWS, seq)
    assert seq % ROW_TILE == 0 and keep_prompt == ROW_TILE and seq % B_TQ == 0
    assert a_keep >= t_sample and d_model == 2 * SEG

    rows_p = batch * seq
    rows_s = dec_batch * t_sample
    tiles_per_seq = seq // ROW_TILE
    tables_p = _rope_tables(np.arange(ROW_TILE), np.arange(tiles_per_seq) * ROW_TILE)
    tables_s = _rope_tables(past_len + np.arange(rows_s) % t_sample, np.zeros(1))

    yp = x_prompt.reshape(rows_p, d_model)
    ys = x_sample.reshape(rows_s, d_model)
    fg = final_gain.reshape(1, d_model)
    outs = [[] for _ in range(8)]
    for l in range(depth):
        lambda_init = 0.8 - 0.6 * math.exp(-0.3 * l)
        last = l == depth - 1
        w_l = w_in[l].astype(BF16)
        w_a = w_out[l, :A_WIDTH].astype(BF16)
        w_b = w_out[l, A_WIDTH:].astype(BF16)
        gain = norm_gain[l].reshape(1, d_model)
        lams = [a[l].reshape(1, B_HEAD_DIM).astype(F32)
                for a in (lambda_q1, lambda_k1, lambda_q2, lambda_k2)]
        subg = subln_gain[l].reshape(1, B_V_DIM).astype(F32)

        qa, ka, va, ga, qb, kb, vb, gb, kaf, vaf, kbf, vbf = _proj(
            yp, gain, w_l, tables_p, tm=ROW_TILE, tiles_per_seq=tiles_per_seq,
            tail_every=tiles_per_seq)
        oa = _attn_a_prompt(qa, ka, va, ga, _band_bias_prompt(rel_bias[l]), batch=batch, seq=seq)
        ob = _attn_b_prompt(qb, kb, vb, gb, lams, subg, batch=batch, seq=seq,
                            lambda_init=lambda_init)
        yp = _out_proj(oa, ob, w_a, w_b, yp, fg, tm=ROW_TILE, final_norm=last)
        outs[0].append(kaf.reshape(batch, keep_prompt, A_HEADS, A_HEAD_DIM))
        outs[1].append(vaf.reshape(batch, keep_prompt, A_HEADS, A_HEAD_DIM))
        outs[2].append(kbf.reshape(batch, seq, B_HEADS, 2, B_HEAD_DIM))
        outs[3].append(vbf.reshape(batch, seq, B_HEADS, B_V_DIM))

        qa, ka, va, ga, qb, kb, vb, gb, kaf, vaf, kbf, vbf = _proj(
            ys, gain, w_l, tables_s, tm=rows_s, tiles_per_seq=1, tail_every=1)
        bias_c, bias_n = _band_bias_sample(rel_bias[l], a_keep, t_sample)
        oa, ak_new, av_new = _attn_a_sample(
            qa, ka, va, kaf, vaf,
            cache_a_k[l].reshape(dec_batch, a_keep, A_WIDTH),
            cache_a_v[l].reshape(dec_batch, a_keep, A_WIDTH),
            ga, bias_c, bias_n, batch=dec_batch, t=t_sample)
        ob = _attn_b_sample(
            qb, kb, vb,
            cache_b_k[l].reshape(dec_batch, past_len, B_WIDTH),
            cache_b_v[l].reshape(dec_batch, past_len, B_WIDTH),
            gb, lams, subg, batch=dec_batch, t=t_sample, lambda_init=lambda_init)
        ys = _out_proj(oa, ob, w_a, w_b, ys, fg, tm=rows_s, final_norm=last)
        outs[4].append(ak_new.reshape(dec_batch, a_keep, A_HEADS, A_HEAD_DIM))
        outs[5].append(av_new.reshape(dec_batch, a_keep, A_HEADS, A_HEAD_DIM))
        outs[6].append(kbf.reshape(dec_batch, t_sample, B_HEADS, 2, B_HEAD_DIM))
        outs[7].append(vbf.reshape(dec_batch, t_sample, B_HEADS, B_V_DIM))

    return (yp.reshape(batch, seq, d_model), ys.reshape(dec_batch, t_sample, d_model),
            *[jnp.stack(o) for o in outs])
```

```python
import functools
import math

import numpy as np
import jax
import jax.numpy as jnp
from jax import lax
from jax.experimental import pallas as pl
from jax.experimental.pallas import tpu as pltpu

F32 = jnp.float32
BF16 = jnp.bfloat16
NEG_INF = float("-inf")

CHUNK = 64
N_PREV_CHUNKS = 8
A_HEADS = 8
A_HEAD_DIM = 64
A_WIDTH = A_HEADS * A_HEAD_DIM
B_HEADS = 4
B_HEAD_DIM = 64
B_V_DIM = 2 * B_HEAD_DIM
B_WIDTH = B_HEADS * B_V_DIM
SEG = 512
MAX_REL = 128
ROPE_THETA = 500000.0
ROPE_DIM = 16
EPS = 1e-6
LANES = 128
VMEM_LIMIT = 56 * 1024 * 1024

ROW_TILE = 512
BAND_ROWS = N_PREV_CHUNKS * CHUNK
A_SUB = 128
A_WIN = BAND_ROWS + A_SUB
B_TQ = 512
B_TK = 512

NT_DIMS = (((1,), (1,)), ((), ()))


def _nt_dot(a, b):
    return lax.dot_general(a, b, NT_DIMS, preferred_element_type=F32)


def _dot(a, b):
    return jnp.dot(a, b, preferred_element_type=F32)


def _silu(g):
    return g / (1.0 + jnp.exp(-g))


def _round_up(n, m):
    return -(-n // m) * m


def _rope_lane_freq():
    d = np.arange(LANES) % B_HEAD_DIM
    inv = ROPE_THETA ** (-np.arange(0, ROPE_DIM, 2, dtype=np.float64) / ROPE_DIM)
    return np.where(d < ROPE_DIM, inv[d % (ROPE_DIM // 2)], 0.0)


def _rope_tables(row_pos, base_pos):
    f = _rope_lane_freq()[None, :]
    ar = np.asarray(row_pos, np.float64)[:, None] * f
    ab = np.asarray(base_pos, np.float64)[:, None] * f
    as32 = lambda a: jnp.asarray(a.astype(np.float32))
    return (as32(np.cos(ar)), as32(np.sin(ar)),
            as32(np.cos(ab))[:, None, :], as32(np.sin(ab))[:, None, :])


def _bias_rows(rel_bias, q0, n_q, n_k):
    width = _round_up(n_k + n_q - 1, LANES)
    t = np.arange(width)
    t = np.where(t < n_k, t, t - width)
    idx = np.clip(q0 - t, -MAX_REL, MAX_REL) + MAX_REL
    return rel_bias.astype(F32)[:, idx]


def _toeplitz(row, n_q):
    return pltpu.roll(jnp.broadcast_to(row, (n_q, row.shape[-1])), 0, 1, stride=1, stride_axis=0)


def _proj_kernel(x_ref, g_ref, w_ref, cr_ref, sr_ref, cb_ref, sb_ref, *out_refs,
                 tail_every, channel_major):
    (qa_ref, ka_ref, va_ref, ga_ref, qb_ref, kb_ref, vb_ref, gb_ref,
     kaf_ref, vaf_ref, kbf_ref, vbf_ref) = out_refs
    x = x_ref[...]
    inv = lax.rsqrt(jnp.mean(x * x, axis=-1, keepdims=True) + EPS)
    h = (x * inv * g_ref[...]).astype(BF16)

    def seg(k):
        return _dot(h, w_ref[:, k * SEG:(k + 1) * SEG])

    cb, sb = cb_ref[0], sb_ref[0]
    cr, sr = cr_ref[...], sr_ref[...]
    cos = cb * cr - sb * sr
    sin = sb * cr + cb * sr
    d = lax.broadcasted_iota(jnp.int32, cos.shape, 1) % B_HEAD_DIM
    sin_lo = jnp.where(d < ROPE_DIM // 2, -sin, 0.0)
    sin_hi = jnp.where(d >= ROPE_DIM // 2, sin, 0.0)

    def rope(z):
        cols = []
        for c in range(SEG // LANES):
            zc = z[:, c * LANES:(c + 1) * LANES]
            up = pltpu.roll(zc, LANES - ROPE_DIM // 2, 1)
            dn = pltpu.roll(zc, ROPE_DIM // 2, 1)
            cols.append(zc * cos + up * sin_lo + dn * sin_hi)
        return jnp.concatenate(cols, axis=1)

    is_tail = (pl.program_id(0) % tail_every) == tail_every - 1
    scale = A_HEAD_DIM ** -0.5

    qa_ref[...] = (seg(0) * scale).astype(BF16)
    ka = seg(1)
    ka_ref[...] = ka.astype(BF16)
    va = seg(2)
    va_ref[...] = va.astype(BF16)

    @pl.when(is_tail)
    def _():
        kaf_ref[...] = ka.T if channel_major else ka
        vaf_ref[...] = va.T if channel_major else va

    ga_ref[...] = seg(3).astype(BF16)
    qb_ref[...] = (rope(seg(4)) * scale).astype(BF16)
    kb = rope(seg(5))
    kbf_ref[...] = kb.T if channel_major else kb
    kb_ref[...] = kb.astype(BF16)
    vb = seg(6)
    vbf_ref[...] = vb
    vb_ref[...] = vb.astype(BF16)
    gb_ref[...] = seg(7).astype(BF16)


def _proj(x, gain, w_bf16, tables, *, tm, tiles_per_seq, tail_every, channel_major):
    rows, d_model = x.shape
    n_tiles = rows // tm
    n_tail = n_tiles // tail_every
    n_seq = n_tiles // tiles_per_seq
    cr, sr, cb, sb = tables
    row_blk = lambda i: (i, 0)
    const2 = lambda i: (0, 0)
    base_blk = lambda i: (i % tiles_per_seq, 0, 0)
    bf = jax.ShapeDtypeStruct((rows, SEG), BF16)
    f32_full = jax.ShapeDtypeStruct((rows, SEG), F32)
    full_spec = pl.BlockSpec((tm, SEG), row_blk)
    if channel_major:
        assert tm == SEG
        tail = jax.ShapeDtypeStruct((n_tail, SEG, tm), F32)
        tail_spec = pl.BlockSpec((None, SEG, tm), lambda i: (i // tail_every, 0, 0))
        kbf = jax.ShapeDtypeStruct((n_seq, SEG, tiles_per_seq * tm), F32)
        kbf_spec = pl.BlockSpec((None, SEG, tm), lambda i: (i // tiles_per_seq, 0, i % tiles_per_seq))
    else:
        tail = jax.ShapeDtypeStruct((n_tail * tm, SEG), F32)
        tail_spec = pl.BlockSpec((tm, SEG), lambda i: (i // tail_every, 0))
        kbf, kbf_spec = f32_full, full_spec
    return pl.pallas_call(
        functools.partial(_proj_kernel, tail_every=tail_every, channel_major=channel_major),
        grid=(n_tiles,),
        in_specs=[
            pl.BlockSpec((tm, d_model), row_blk),
            pl.BlockSpec((1, d_model), const2),
            pl.BlockSpec(w_bf16.shape, const2),
            pl.BlockSpec((tm, LANES), const2),
            pl.BlockSpec((tm, LANES), const2),
            pl.BlockSpec((1, 1, LANES), base_blk),
            pl.BlockSpec((1, 1, LANES), base_blk),
        ],
        out_specs=[full_spec] * 8 + [tail_spec, tail_spec, kbf_spec, full_spec],
        out_shape=[bf] * 8 + [tail, tail, kbf, f32_full],
        compiler_params=pltpu.CompilerParams(
            dimension_semantics=("arbitrary",), vmem_limit_bytes=VMEM_LIMIT),
        name="proj",
    )(x, gain, w_bf16, cr, sr, cb, sb)


def _attn_a_prompt_kernel(q_ref, kp_ref, kc_ref, vp_ref, vc_ref, g_ref, brow_ref, o_ref,
                          kcat_ref, vcat_ref, bias_ref):
    n = pl.program_id(1)

    @pl.when((pl.program_id(0) == 0) & (n == 0))
    def _():
        i = lax.broadcasted_iota(jnp.int32, (A_SUB, A_WIN), 0)
        j = lax.broadcasted_iota(jnp.int32, (A_SUB, A_WIN), 1)
        gap = i // CHUNK + N_PREV_CHUNKS - j // CHUNK
        band = (gap >= 0) & (gap <= N_PREV_CHUNKS)
        for hd in range(A_HEADS):
            t = _toeplitz(brow_ref[hd:hd + 1, :], A_SUB)[:, :A_WIN]
            bias_ref[hd] = jnp.where(band, t, NEG_INF)

    kcat_ref[0:ROW_TILE, :] = kp_ref[...]
    kcat_ref[ROW_TILE:2 * ROW_TILE, :] = kc_ref[...]
    vcat_ref[0:ROW_TILE, :] = vp_ref[...]
    vcat_ref[ROW_TILE:2 * ROW_TILE, :] = vc_ref[...]
    col = lax.broadcasted_iota(jnp.int32, (A_SUB, A_WIN), 1)
    lane = lax.broadcasted_iota(jnp.int32, (A_SUB, LANES), 1)
    low_half = lane < A_HEAD_DIM

    def sub_block(u, carry):
        r0 = pl.multiple_of(u * A_SUB, A_SUB)
        first_valid = jnp.where(n == 0, BAND_ROWS - r0, 0)
        valid = col >= first_valid
        for pair in range(A_HEADS // 2):
            cs = slice(pair * LANES, (pair + 1) * LANES)
            qp = q_ref[pl.ds(r0, A_SUB), cs]
            kp = kcat_ref[pl.ds(r0, A_WIN), cs]
            vp = vcat_ref[pl.ds(r0, A_WIN), cs]
            outs = []
            for e in range(2):
                qm = jnp.where(low_half if e == 0 else ~low_half, qp, jnp.zeros_like(qp))
                s = _nt_dot(qm, kp) + bias_ref[2 * pair + e]
                s = jnp.where(valid, s, NEG_INF)
                m = jnp.max(s, axis=-1, keepdims=True)
                p = jnp.exp(s - m)
                l = jnp.sum(p, axis=-1, keepdims=True)
                outs.append(_dot(p.astype(BF16), vp) / l)
            o_pair = jnp.where(low_half, outs[0], outs[1])
            gate = g_ref[pl.ds(r0, A_SUB), cs].astype(F32)
            o_ref[pl.ds(r0, A_SUB), cs] = (o_pair * _silu(gate)).astype(BF16)
        return carry

    lax.fori_loop(0, ROW_TILE // A_SUB, sub_block, 0)


def _attn_a_prompt(qa, ka, va, ga, bias_rows, *, batch, seq):
    tiles = seq // ROW_TILE
    cur = lambda b, n: (b * tiles + n, 0)
    prev = lambda b, n: (b * tiles + jnp.maximum(n - 1, 0), 0)
    blk = (ROW_TILE, A_WIDTH)
    return pl.pallas_call(
        _attn_a_prompt_kernel,
        grid=(batch, tiles),
        in_specs=[
            pl.BlockSpec(blk, cur),
            pl.BlockSpec(blk, prev), pl.BlockSpec(blk, cur),
            pl.BlockSpec(blk, prev), pl.BlockSpec(blk, cur),
            pl.BlockSpec(blk, cur),
            pl.BlockSpec(bias_rows.shape, lambda b, n: (0, 0)),
        ],
        out_specs=pl.BlockSpec(blk, cur),
        out_shape=jax.ShapeDtypeStruct(qa.shape, BF16),
        scratch_shapes=[pltpu.VMEM((2 * ROW_TILE, A_WIDTH), BF16),
                        pltpu.VMEM((2 * ROW_TILE, A_WIDTH), BF16),
                        pltpu.VMEM((A_HEADS, A_SUB, A_WIN), F32)],
        compiler_params=pltpu.CompilerParams(
            dimension_semantics=("arbitrary", "arbitrary"), vmem_limit_bytes=VMEM_LIMIT),
        name="attn_a_prompt",
    )(qa, ka, ka, va, va, ga, bias_rows)


def _diff_lambda(lq1_ref, lk1_ref, lq2_ref, lk2_ref, lambda_init):
    e1 = jnp.exp(jnp.sum(lq1_ref[...] * lk1_ref[...], axis=-1, keepdims=True))
    e2 = jnp.exp(jnp.sum(lq2_ref[...] * lk2_ref[...], axis=-1, keepdims=True))
    return e1 - e2 + lambda_init


def _stack_maps(qh):
    lane = lax.broadcasted_iota(jnp.int32, qh.shape, 1)
    zero = jnp.zeros_like(qh)
    return jnp.concatenate([jnp.where(lane < B_HEAD_DIM, qh, zero),
                            jnp.where(lane >= B_HEAD_DIM, qh, zero)], axis=0)


def _subnorm_gate(o, subg_ref, gate, lambda_init):
    o = o * lax.rsqrt(jnp.mean(o * o, axis=-1, keepdims=True) + EPS) * subg_ref[...]
    return (o * (1.0 - lambda_init)) * _silu(gate)


def _attn_b_prompt_kernel(q_ref, k_ref, v_ref, g_ref, lq1_ref, lk1_ref, lq2_ref, lk2_ref,
                          subg_ref, o_ref, *, lambda_init):
    i = pl.program_id(2)
    qs = _stack_maps(q_ref[...])
    rows = 2 * B_TQ

    def kv_step(j, carry, masked):
        m, l, acc = carry
        k0 = pl.multiple_of(j * B_TK, B_TK)
        s = _nt_dot(qs, k_ref[pl.ds(k0, B_TK), :])
        if masked:
            qc = (lax.broadcasted_iota(jnp.int32, s.shape, 0) % B_TQ) // CHUNK
            kc = lax.broadcasted_iota(jnp.int32, s.shape, 1) // CHUNK
            s = jnp.where(kc <= qc, s, NEG_INF)
        m_new = jnp.maximum(m, jnp.max(s, axis=-1, keepdims=True))
        alpha = jnp.exp(m - m_new)
        p = jnp.exp(s - m_new)
        l = alpha * l + jnp.sum(p, axis=-1, keepdims=True)
        acc = alpha * acc + _dot(p.astype(BF16), v_ref[pl.ds(k0, B_TK), :])
        return m_new, l, acc

    init = (jnp.full((rows, 1), NEG_INF, F32), jnp.zeros((rows, 1), F32),
            jnp.zeros((rows, B_V_DIM), F32))
    carry = lax.fori_loop(0, i, functools.partial(kv_step, masked=False), init)
    m, l, acc = kv_step(i, carry, masked=True)
    o = acc / l
    lam = _diff_lambda(lq1_ref, lk1_ref, lq2_ref, lk2_ref, lambda_init)
    o = o[:B_TQ] - lam * o[B_TQ:]
    o_ref[...] = _subnorm_gate(o, subg_ref, g_ref[...].astype(F32), lambda_init).astype(BF16)


def _attn_b_prompt(qb, kb, vb, gb, lams, subg, *, batch, seq, lambda_init):
    tiles = seq // B_TQ
    qblk = lambda b, h, i: (b * tiles + i, h)
    kvblk = lambda b, h, i: (b, h)
    vec = pl.BlockSpec((1, B_HEAD_DIM), lambda b, h, i: (0, 0))
    return pl.pallas_call(
        functools.partial(_attn_b_prompt_kernel, lambda_init=lambda_init),
        grid=(batch, B_HEADS, tiles),
        in_specs=[
            pl.BlockSpec((B_TQ, B_V_DIM), qblk),
            pl.BlockSpec((seq, B_V_DIM), kvblk),
            pl.BlockSpec((seq, B_V_DIM), kvblk),
            pl.BlockSpec((B_TQ, B_V_DIM), qblk),
            vec, vec, vec, vec,
            pl.BlockSpec((1, B_V_DIM), lambda b, h, i: (0, 0)),
        ],
        out_specs=pl.BlockSpec((B_TQ, B_V_DIM), qblk),
        out_shape=jax.ShapeDtypeStruct(qb.shape, BF16),
        compiler_params=pltpu.CompilerParams(
            dimension_semantics=("arbitrary", "arbitrary", "arbitrary"),
            vmem_limit_bytes=VMEM_LIMIT),
        name="attn_b_prompt",
    )(qb, kb, vb, gb, *lams, subg)


def _roll_in(cache_t, new_rows):
    t = new_rows.shape[0]
    keep = cache_t.shape[1]
    shifted = pltpu.roll(cache_t, keep - t, 1)
    pad = jnp.concatenate([jnp.zeros((LANES - t, new_rows.shape[1]), F32), new_rows], axis=0)
    new_t = pad.T
    lane = lax.broadcasted_iota(jnp.int32, new_t.shape, 1)
    last = jnp.where(lane >= LANES - t, new_t, shifted[:, keep - LANES:])
    return jnp.concatenate([shifted[:, :keep - LANES], last], axis=1)


def _attn_a_sample_kernel(q_ref, kn_ref, vn_ref, knf_ref, vnf_ref, kc_ref, vc_ref, g_ref,
                          brow_ref, o_ref, ko_ref, vo_ref):
    t = q_ref.shape[0]
    keep = kc_ref.shape[1]
    lane = lax.broadcasted_iota(jnp.int32, (t, LANES), 1)
    low_half = lane < A_HEAD_DIM
    for pair in range(A_HEADS // 2):
        cs = slice(pair * LANES, (pair + 1) * LANES)
        qp = q_ref[:, cs]
        kc = kc_ref[cs, :].astype(BF16)
        vc = vc_ref[cs, :].astype(BF16)
        kn = kn_ref[:, cs]
        vn = vn_ref[:, cs]
        outs = []
        for e in range(2):
            hd = 2 * pair + e
            bias = _toeplitz(brow_ref[hd:hd + 1, :], t)
            qm = jnp.where(low_half if e == 0 else ~low_half, qp, jnp.zeros_like(qp))
            s_c = _dot(qm, kc) + bias[:, :keep]
            s_n = _nt_dot(qm, kn) + bias[:, keep:keep + t]
            m = jnp.maximum(jnp.max(s_c, axis=-1, keepdims=True),
                            jnp.max(s_n, axis=-1, keepdims=True))
            p_c = jnp.exp(s_c - m)
            p_n = jnp.exp(s_n - m)
            l = jnp.sum(p_c, axis=-1, keepdims=True) + jnp.sum(p_n, axis=-1, keepdims=True)
            o = _nt_dot(p_c.astype(BF16), vc) + _dot(p_n.astype(BF16), vn)
            outs.append(o / l)
        o_pair = jnp.where(low_half, outs[0], outs[1])
        o_ref[:, cs] = (o_pair * _silu(g_ref[:, cs].astype(F32))).astype(BF16)
    ko_ref[...] = _roll_in(kc_ref[...], knf_ref[...])
    vo_ref[...] = _roll_in(vc_ref[...], vnf_ref[...])


def _attn_a_sample(qa, ka, va, kaf, vaf, cache_kt, cache_vt, ga, bias_rows, *, batch, t):
    keep = cache_kt.shape[2]
    new = pl.BlockSpec((t, A_WIDTH), lambda b: (b, 0))
    cache = pl.BlockSpec((None, A_WIDTH, keep), lambda b: (b, 0, 0))
    return pl.pallas_call(
        _attn_a_sample_kernel,
        grid=(batch,),
        in_specs=[new, new, new, new, new, cache, cache, new,
                  pl.BlockSpec(bias_rows.shape, lambda b: (0, 0))],
        out_specs=[new, cache, cache],
        out_shape=[jax.ShapeDtypeStruct(qa.shape, BF16),
                   jax.ShapeDtypeStruct(cache_kt.shape, F32),
                   jax.ShapeDtypeStruct(cache_vt.shape, F32)],
        compiler_params=pltpu.CompilerParams(
            dimension_semantics=("arbitrary",), vmem_limit_bytes=VMEM_LIMIT),
        name="attn_a_sample",
    )(qa, ka, va, kaf, vaf, cache_kt, cache_vt, ga, bias_rows)


def _attn_b_sample_kernel(q_ref, kn_ref, vn_ref, kc_ref, vc_ref, g_ref,
                          lq1_ref, lk1_ref, lq2_ref, lk2_ref, subg_ref, o_ref, *, lambda_init):
    t = q_ref.shape[0]
    past = kc_ref.shape[1]
    lam = _diff_lambda(lq1_ref, lk1_ref, lq2_ref, lk2_ref, lambda_init)
    for h in range(B_HEADS):
        cs = slice(h * B_V_DIM, (h + 1) * B_V_DIM)
        qs = _stack_maps(q_ref[:, cs])
        s_c = _dot(qs, kc_ref[cs, :].astype(BF16))
        s_n = _nt_dot(qs, kn_ref[:, cs])
        m = jnp.maximum(jnp.max(s_c, axis=-1, keepdims=True),
                        jnp.max(s_n, axis=-1, keepdims=True))
        p_c = jnp.exp(s_c - m)
        p_n = jnp.exp(s_n - m)
        l = jnp.sum(p_c, axis=-1, keepdims=True) + jnp.sum(p_n, axis=-1, keepdims=True)
        p_c = p_c / l
        p_n = p_n / l
        a_c = (p_c[:t] - lam * p_c[t:]).astype(BF16)
        a_n = (p_n[:t] - lam * p_n[t:]).astype(BF16)
        vc = vc_ref[pl.ds(h, past, stride=B_HEADS), :].astype(BF16)
        o = _dot(a_c, vc) + _dot(a_n, vn_ref[:, cs])
        o_ref[:, cs] = _subnorm_gate(o, subg_ref, g_ref[:, cs].astype(F32),
                                     lambda_init).astype(BF16)


def _attn_b_sample(qb, kb, vb, cache_kt, cache_v, gb, lams, subg, *, batch, t, lambda_init):
    past = cache_kt.shape[2]
    new = pl.BlockSpec((t, B_WIDTH), lambda b: (b, 0))
    vec = pl.BlockSpec((1, B_HEAD_DIM), lambda b: (0, 0))
    return pl.pallas_call(
        functools.partial(_attn_b_sample_kernel, lambda_init=lambda_init),
        grid=(batch,),
        in_specs=[new, new, new,
                  pl.BlockSpec((None, B_WIDTH, past), lambda b: (b, 0, 0)),
                  pl.BlockSpec((None, past * B_HEADS, B_V_DIM), lambda b: (b, 0, 0)),
                  new, vec, vec, vec, vec,
                  pl.BlockSpec((1, B_V_DIM), lambda b: (0, 0))],
        out_specs=new,
        out_shape=jax.ShapeDtypeStruct(qb.shape, BF16),
        compiler_params=pltpu.CompilerParams(
            dimension_semantics=("arbitrary",), vmem_limit_bytes=VMEM_LIMIT),
        name="attn_b_sample",
    )(qb, kb, vb, cache_kt, cache_v, gb, *lams, subg)


def _out_kernel(oa_ref, ob_ref, wa_ref, wb_ref, x_ref, fg_ref, y_ref, *, final_norm):
    y = x_ref[...] + _dot(oa_ref[...], wa_ref[...]) + _dot(ob_ref[...], wb_ref[...])
    if final_norm:
        y = y * lax.rsqrt(jnp.mean(y * y, axis=-1, keepdims=True) + EPS) * fg_ref[...]
    y_ref[...] = y


def _out_proj(oa, ob, w_a, w_b, x, final_gain, *, tm, final_norm):
    rows, d_model = x.shape
    row_blk = lambda i: (i, 0)
    const2 = lambda i: (0, 0)
    return pl.pallas_call(
        functools.partial(_out_kernel, final_norm=final_norm),
        grid=(rows // tm,),
        in_specs=[
            pl.BlockSpec((tm, A_WIDTH), row_blk),
            pl.BlockSpec((tm, B_WIDTH), row_blk),
            pl.BlockSpec(w_a.shape, const2),
            pl.BlockSpec(w_b.shape, const2),
            pl.BlockSpec((tm, d_model), row_blk),
            pl.BlockSpec((1, d_model), const2),
        ],
        out_specs=pl.BlockSpec((tm, d_model), row_blk),
        out_shape=jax.ShapeDtypeStruct(x.shape, F32),
        compiler_params=pltpu.CompilerParams(
            dimension_semantics=("arbitrary",), vmem_limit_bytes=VMEM_LIMIT),
        name="out_proj",
    )(oa, ob, w_a, w_b, x, final_gain)


def _channel_major(a):
    n, pos = a.shape[:2]
    return jnp.moveaxis(a.reshape(n, pos, -1), 1, 2)


def _position_major(a, channel_dims):
    n, _, pos = a.shape
    return jnp.moveaxis(a, 2, 1).reshape(n, pos, *channel_dims)


def kernel(x_prompt, x_sample, cache_a_k, cache_a_v, cache_b_k, cache_b_v,
           norm_gain, w_in, w_out, rel_bias, lambda_q1, lambda_k1, lambda_q2, lambda_k2,
           subln_gain, final_gain):
    batch, seq, d_model = x_prompt.shape
    dec_batch, t_sample, _ = x_sample.shape
    depth = w_in.shape[0]
    past_len = cache_b_k.shape[2]
    a_keep = cache_a_k.shape[2]
    keep_prompt = min(BAND_ROWS, seq)
    assert seq % ROW_TILE == 0 and keep_prompt == ROW_TILE and seq % B_TQ == 0
    assert t_sample <= a_keep and t_sample <= LANES and d_model == 2 * SEG

    rows_p = batch * seq
    rows_s = dec_batch * t_sample
    tiles_per_seq = seq // ROW_TILE
    tables_p = _rope_tables(np.arange(ROW_TILE), np.arange(tiles_per_seq) * ROW_TILE)
    tables_s = _rope_tables(past_len + np.arange(rows_s) % t_sample, np.zeros(1))
    a_dims = (A_HEADS, A_HEAD_DIM)
    bk_dims = (B_HEADS, 2, B_HEAD_DIM)
    bv_dims = (B_HEADS, B_V_DIM)

    yp = x_prompt.reshape(rows_p, d_model)
    ys = x_sample.reshape(rows_s, d_model)
    fg = final_gain.reshape(1, d_model)
    outs = [[] for _ in range(8)]
    for l in range(depth):
        lambda_init = 0.8 - 0.6 * math.exp(-0.3 * l)
        last = l == depth - 1
        w_l = w_in[l].astype(BF16)
        w_a = w_out[l, :A_WIDTH].astype(BF16)
        w_b = w_out[l, A_WIDTH:].astype(BF16)
        gain = norm_gain[l].reshape(1, d_model)
        lams = [a[l].reshape(1, B_HEAD_DIM).astype(F32)
                for a in (lambda_q1, lambda_k1, lambda_q2, lambda_k2)]
        subg = subln_gain[l].reshape(1, B_V_DIM).astype(F32)

        qa, ka, va, ga, qb, kb, vb, gb, kaf, vaf, kbf, vbf = _proj(
            yp, gain, w_l, tables_p, tm=ROW_TILE, tiles_per_seq=tiles_per_seq,
            tail_every=tiles_per_seq, channel_major=True)
        oa = _attn_a_prompt(qa, ka, va, ga, _bias_rows(rel_bias[l], BAND_ROWS, A_SUB, A_WIN),
                            batch=batch, seq=seq)
        ob = _attn_b_prompt(qb, kb, vb, gb, lams, subg, batch=batch, seq=seq,
                            lambda_init=lambda_init)
        yp = _out_proj(oa, ob, w_a, w_b, yp, fg, tm=ROW_TILE, final_norm=last)
        outs[0].append(_position_major(kaf, a_dims))
        outs[1].append(_position_major(vaf, a_dims))
        outs[2].append(_position_major(kbf, bk_dims))
        outs[3].append(vbf.reshape(batch, seq, *bv_dims))

        qa, ka, va, ga, qb, kb, vb, gb, kaf, vaf, kbf, vbf = _proj(
            ys, gain, w_l, tables_s, tm=rows_s, tiles_per_seq=1, tail_every=1,
            channel_major=False)
        oa, ak_new, av_new = _attn_a_sample(
            qa, ka, va, kaf, vaf, _channel_major(cache_a_k[l]), _channel_major(cache_a_v[l]),
            ga, _bias_rows(rel_bias[l], a_keep, t_sample, a_keep + t_sample),
            batch=dec_batch, t=t_sample)
        ob = _attn_b_sample(
            qb, kb, vb, _channel_major(cache_b_k[l]),
            cache_b_v[l].reshape(dec_batch, past_len * B_HEADS, B_V_DIM),
            gb, lams, subg, batch=dec_batch, t=t_sample, lambda_init=lambda_init)
        ys = _out_proj(oa, ob, w_a, w_b, ys, fg, tm=rows_s, final_norm=last)
        outs[4].append(_position_major(ak_new, a_dims))
        outs[5].append(_position_major(av_new, a_dims))
        outs[6].append(kbf.reshape(dec_batch, t_sample, *bk_dims))
        outs[7].append(vbf.reshape(dec_batch, t_sample, *bv_dims))

    return (yp.reshape(batch, seq, d_model), ys.reshape(dec_batch, t_sample, d_model),
            *[jnp.stack(o) for o in outs])
```

```python
import functools
import math

import numpy as np
import jax
import jax.numpy as jnp
from jax import lax
from jax.experimental import pallas as pl
from jax.experimental.pallas import tpu as pltpu

F32 = jnp.float32
BF16 = jnp.bfloat16
NEG_INF = float("-inf")

CHUNK = 64
N_PREV_CHUNKS = 8
A_HEADS = 8
A_HEAD_DIM = 64
A_WIDTH = A_HEADS * A_HEAD_DIM
B_HEADS = 4
B_HEAD_DIM = 64
B_V_DIM = 2 * B_HEAD_DIM
B_WIDTH = B_HEADS * B_V_DIM
SEG = 512
MAX_REL = 128
ROPE_THETA = 500000.0
ROPE_DIM = 16
EPS = 1e-6
LOG2E = math.log2(math.e)
LANES = 128
VMEM_LIMIT = 56 * 1024 * 1024

ROW_TILE = 512
BAND_ROWS = N_PREV_CHUNKS * CHUNK
A_SUB = 128
A_WIN = BAND_ROWS + A_SUB
B_TQ = 512
B_TK = 512
B_HEADS_PER_STEP = 2
B_STRIP = 256
B_ROWS = 64
B_SUM_ROWS = 16

NT_DIMS = (((1,), (1,)), ((), ()))


def _nt_dot(a, b):
    return lax.dot_general(a, b, NT_DIMS, preferred_element_type=F32)


def _dot(a, b):
    return jnp.dot(a, b, preferred_element_type=F32)


def _silu(g):
    return g / (1.0 + jnp.exp(-g))


def _round_up(n, m):
    return -(-n // m) * m


def _rope_lane_freq():
    d = np.arange(LANES) % B_HEAD_DIM
    inv = ROPE_THETA ** (-np.arange(0, ROPE_DIM, 2, dtype=np.float64) / ROPE_DIM)
    return np.where(d < ROPE_DIM, inv[d % (ROPE_DIM // 2)], 0.0)


def _rope_tables(row_pos, base_pos):
    f = _rope_lane_freq()[None, :]
    ar = np.asarray(row_pos, np.float64)[:, None] * f
    ab = np.asarray(base_pos, np.float64)[:, None] * f
    as32 = lambda a: jnp.asarray(a.astype(np.float32))
    return (as32(np.cos(ar)), as32(np.sin(ar)),
            as32(np.cos(ab))[:, None, :], as32(np.sin(ab))[:, None, :])


def _bias_rows(rel_bias, q0, n_q, n_k):
    width = _round_up(n_k + n_q - 1, LANES)
    t = np.arange(width)
    t = np.where(t < n_k, t, t - width)
    idx = np.clip(q0 - t, -MAX_REL, MAX_REL) + MAX_REL
    return rel_bias.astype(F32)[:, idx]


def _toeplitz(row, n_q):
    return pltpu.roll(jnp.broadcast_to(row, (n_q, row.shape[-1])), 0, 1, stride=1, stride_axis=0)


def _proj_kernel(x_ref, g_ref, w_ref, cr_ref, sr_ref, cb_ref, sb_ref, *out_refs,
                 tail_every, channel_major, qb_scale):
    (qa_ref, ka_ref, va_ref, ga_ref, qb_ref, kb_ref, vb_ref, gb_ref,
     kaf_ref, vaf_ref, kbf_ref, vbf_ref) = out_refs
    x = x_ref[...]
    inv = lax.rsqrt(jnp.mean(x * x, axis=-1, keepdims=True) + EPS)
    h = (x * inv * g_ref[...]).astype(BF16)

    def seg(k):
        return _dot(h, w_ref[:, k * SEG:(k + 1) * SEG])

    cb, sb = cb_ref[0], sb_ref[0]
    cr, sr = cr_ref[...], sr_ref[...]
    cos = cb * cr - sb * sr
    sin = sb * cr + cb * sr
    d = lax.broadcasted_iota(jnp.int32, cos.shape, 1) % B_HEAD_DIM
    sin_lo = jnp.where(d < ROPE_DIM // 2, -sin, 0.0)
    sin_hi = jnp.where(d >= ROPE_DIM // 2, sin, 0.0)

    def rope(z):
        cols = []
        for c in range(SEG // LANES):
            zc = z[:, c * LANES:(c + 1) * LANES]
            up = pltpu.roll(zc, LANES - ROPE_DIM // 2, 1)
            dn = pltpu.roll(zc, ROPE_DIM // 2, 1)
            cols.append(zc * cos + up * sin_lo + dn * sin_hi)
        return jnp.concatenate(cols, axis=1)

    is_tail = (pl.program_id(0) % tail_every) == tail_every - 1
    scale = A_HEAD_DIM ** -0.5

    qa_ref[...] = (seg(0) * scale).astype(BF16)
    ka = seg(1)
    ka_ref[...] = ka.astype(BF16)
    va = seg(2)
    va_ref[...] = va.astype(BF16)

    @pl.when(is_tail)
    def _():
        kaf_ref[...] = ka.T if channel_major else ka
        vaf_ref[...] = va.T if channel_major else va

    ga_ref[...] = seg(3).astype(BF16)
    qb = rope(seg(4)) * qb_scale
    qb_ref[...] = (qb.T if channel_major else qb).astype(BF16)
    kb = rope(seg(5))
    kbf_ref[...] = kb.T if channel_major else kb
    kb_ref[...] = kb.astype(BF16)
    vb = seg(6)
    vbf_ref[...] = vb
    vb_ref[...] = (vb.T if channel_major else vb).astype(BF16)
    gb_ref[...] = seg(7).astype(BF16)


def _proj(x, gain, w_bf16, tables, *, tm, tiles_per_seq, tail_every, channel_major, qb_scale):
    rows, d_model = x.shape
    n_tiles = rows // tm
    n_tail = n_tiles // tail_every
    n_seq = n_tiles // tiles_per_seq
    cr, sr, cb, sb = tables
    row_blk = lambda i: (i, 0)
    const2 = lambda i: (0, 0)
    base_blk = lambda i: (i % tiles_per_seq, 0, 0)
    bf = jax.ShapeDtypeStruct((rows, SEG), BF16)
    f32_full = jax.ShapeDtypeStruct((rows, SEG), F32)
    full_spec = pl.BlockSpec((tm, SEG), row_blk)
    if channel_major:
        assert tm == SEG
        tail = jax.ShapeDtypeStruct((n_tail, SEG, tm), F32)
        tail_spec = pl.BlockSpec((None, SEG, tm), lambda i: (i // tail_every, 0, 0))
        kbf = jax.ShapeDtypeStruct((n_seq, SEG, tiles_per_seq * tm), F32)
        kbf_spec = pl.BlockSpec((None, SEG, tm), lambda i: (i // tiles_per_seq, 0, i % tiles_per_seq))
        bf_t = jax.ShapeDtypeStruct((n_tiles, SEG, tm), BF16)
        bf_t_spec = pl.BlockSpec((None, SEG, tm), lambda i: (i, 0, 0))
    else:
        tail = jax.ShapeDtypeStruct((n_tail * tm, SEG), F32)
        tail_spec = pl.BlockSpec((tm, SEG), lambda i: (i // tail_every, 0))
        kbf, kbf_spec = f32_full, full_spec
        bf_t, bf_t_spec = bf, full_spec
    return pl.pallas_call(
        functools.partial(_proj_kernel, tail_every=tail_every, channel_major=channel_major,
                          qb_scale=qb_scale),
        grid=(n_tiles,),
        in_specs=[
            pl.BlockSpec((tm, d_model), row_blk),
            pl.BlockSpec((1, d_model), const2),
            pl.BlockSpec(w_bf16.shape, const2),
            pl.BlockSpec((tm, LANES), const2),
            pl.BlockSpec((tm, LANES), const2),
            pl.BlockSpec((1, 1, LANES), base_blk),
            pl.BlockSpec((1, 1, LANES), base_blk),
        ],
        out_specs=([full_spec] * 4 + [bf_t_spec, full_spec, bf_t_spec, full_spec]
                   + [tail_spec, tail_spec, kbf_spec, full_spec]),
        out_shape=[bf] * 4 + [bf_t, bf, bf_t, bf] + [tail, tail, kbf, f32_full],
        compiler_params=pltpu.CompilerParams(
            dimension_semantics=("arbitrary",), vmem_limit_bytes=VMEM_LIMIT),
        name="proj",
    )(x, gain, w_bf16, cr, sr, cb, sb)


def _attn_a_prompt_kernel(q_ref, kp_ref, kc_ref, vp_ref, vc_ref, g_ref, brow_ref, o_ref,
                          kcat_ref, vcat_ref, bias_ref):
    n = pl.program_id(1)

    @pl.when((pl.program_id(0) == 0) & (n == 0))
    def _():
        i = lax.broadcasted_iota(jnp.int32, (A_SUB, A_WIN), 0)
        j = lax.broadcasted_iota(jnp.int32, (A_SUB, A_WIN), 1)
        gap = i // CHUNK + N_PREV_CHUNKS - j // CHUNK
        band = (gap >= 0) & (gap <= N_PREV_CHUNKS)
        for hd in range(A_HEADS):
            t = _toeplitz(brow_ref[hd:hd + 1, :], A_SUB)[:, :A_WIN]
            bias_ref[hd] = jnp.where(band, t, NEG_INF)

    kcat_ref[0:ROW_TILE, :] = kp_ref[...]
    kcat_ref[ROW_TILE:2 * ROW_TILE, :] = kc_ref[...]
    vcat_ref[0:ROW_TILE, :] = vp_ref[...]
    vcat_ref[ROW_TILE:2 * ROW_TILE, :] = vc_ref[...]
    col = lax.broadcasted_iota(jnp.int32, (A_SUB, A_WIN), 1)
    lane = lax.broadcasted_iota(jnp.int32, (A_SUB, LANES), 1)
    low_half = lane < A_HEAD_DIM

    def sub_block(u, carry):
        r0 = pl.multiple_of(u * A_SUB, A_SUB)
        first_valid = jnp.where(n == 0, BAND_ROWS - r0, 0)
        valid = col >= first_valid
        for pair in range(A_HEADS // 2):
            cs = slice(pair * LANES, (pair + 1) * LANES)
            qp = q_ref[pl.ds(r0, A_SUB), cs]
            kp = kcat_ref[pl.ds(r0, A_WIN), cs]
            vp = vcat_ref[pl.ds(r0, A_WIN), cs]
            outs = []
            for e in range(2):
                qm = jnp.where(low_half if e == 0 else ~low_half, qp, jnp.zeros_like(qp))
                s = _nt_dot(qm, kp) + bias_ref[2 * pair + e]
                s = jnp.where(valid, s, NEG_INF)
                m = jnp.max(s, axis=-1, keepdims=True)
                p = jnp.exp(s - m)
                l = jnp.sum(p, axis=-1, keepdims=True)
                outs.append(_dot(p.astype(BF16), vp) / l)
            o_pair = jnp.where(low_half, outs[0], outs[1])
            gate = g_ref[pl.ds(r0, A_SUB), cs].astype(F32)
            o_ref[pl.ds(r0, A_SUB), cs] = (o_pair * _silu(gate)).astype(BF16)
        return carry

    lax.fori_loop(0, ROW_TILE // A_SUB, sub_block, 0)


def _attn_a_prompt(qa, ka, va, ga, bias_rows, *, batch, seq):
    tiles = seq // ROW_TILE
    cur = lambda b, n: (b * tiles + n, 0)
    prev = lambda b, n: (b * tiles + jnp.maximum(n - 1, 0), 0)
    blk = (ROW_TILE, A_WIDTH)
    return pl.pallas_call(
        _attn_a_prompt_kernel,
        grid=(batch, tiles),
        in_specs=[
            pl.BlockSpec(blk, cur),
            pl.BlockSpec(blk, prev), pl.BlockSpec(blk, cur),
            pl.BlockSpec(blk, prev), pl.BlockSpec(blk, cur),
            pl.BlockSpec(blk, cur),
            pl.BlockSpec(bias_rows.shape, lambda b, n: (0, 0)),
        ],
        out_specs=pl.BlockSpec(blk, cur),
        out_shape=jax.ShapeDtypeStruct(qa.shape, BF16),
        scratch_shapes=[pltpu.VMEM((2 * ROW_TILE, A_WIDTH), BF16),
                        pltpu.VMEM((2 * ROW_TILE, A_WIDTH), BF16),
                        pltpu.VMEM((A_HEADS, A_SUB, A_WIN), F32)],
        compiler_params=pltpu.CompilerParams(
            dimension_semantics=("arbitrary", "arbitrary"), vmem_limit_bytes=VMEM_LIMIT),
        name="attn_a_prompt",
    )(qa, ka, ka, va, va, ga, bias_rows)


def _diff_lambda(lq1_ref, lk1_ref, lq2_ref, lk2_ref, lambda_init):
    e1 = jnp.exp(jnp.sum(lq1_ref[...] * lk1_ref[...], axis=-1, keepdims=True))
    e2 = jnp.exp(jnp.sum(lq2_ref[...] * lk2_ref[...], axis=-1, keepdims=True))
    return e1 - e2 + lambda_init


def _stack_maps(qh):
    lane = lax.broadcasted_iota(jnp.int32, qh.shape, 1)
    zero = jnp.zeros_like(qh)
    return jnp.concatenate([jnp.where(lane < B_HEAD_DIM, qh, zero),
                            jnp.where(lane >= B_HEAD_DIM, qh, zero)], axis=0)


def _subnorm_gate(o, subg_ref, gate, lambda_init):
    o = o * lax.rsqrt(jnp.mean(o * o, axis=-1, keepdims=True) + EPS) * subg_ref[...]
    return (o * (1.0 - lambda_init)) * _silu(gate)


def _attn_b_prompt_kernel(qt_ref, k_ref, vt_ref, g_ref, lq1_ref, lk1_ref, lq2_ref, lk2_ref,
                          subg_ref, o_ref, rhs_ref, acc_ref, m_ref, sa_ref, sb_ref, *, lambda_init):
    i = pl.program_id(2)
    heads = rhs_ref.shape[0]
    for h in range(heads):
        qt = qt_ref[h * B_V_DIM:(h + 1) * B_V_DIM, :]
        row = lax.broadcasted_iota(jnp.int32, qt.shape, 0)
        zero = jnp.zeros_like(qt)
        rhs_ref[h, :, :B_TQ] = jnp.where(row < B_HEAD_DIM, qt, zero)
        rhs_ref[h, :, B_TQ:] = jnp.where(row >= B_HEAD_DIM, qt, zero)
    m_ref[...] = jnp.full(m_ref.shape, NEG_INF, F32)
    acc_ref[...] = jnp.zeros(acc_ref.shape, F32)

    items = [(h, c) for h in range(heads) for c in range(2 * B_TQ // B_STRIP)]

    def score_item(j, n, dst):
        h, c = items[n]
        k0 = pl.multiple_of(j * B_TK, B_TK)
        dst[n] = _dot(k_ref[pl.ds(k0, B_TK), h * B_V_DIM:(h + 1) * B_V_DIM],
                      rhs_ref[h, :, c * B_STRIP:(c + 1) * B_STRIP])

    def update_item(j, n, src, masked):
        h, c = items[n]
        hs = slice(h * B_V_DIM, (h + 1) * B_V_DIM)
        cols = slice(c * B_STRIP, (c + 1) * B_STRIP)

        def chunk(r):
            s = src[n, r:r + B_ROWS, :]
            if masked:
                q0 = (c * B_STRIP) % B_TQ
                kc = (r + lax.broadcasted_iota(jnp.int32, s.shape, 0)) // CHUNK
                qc = (q0 + lax.broadcasted_iota(jnp.int32, s.shape, 1)) // CHUNK
                s = jnp.where(kc <= qc, s, NEG_INF)
            return s

        rows = range(0, B_TK, B_ROWS)
        m_blk = functools.reduce(jnp.maximum, [chunk(r) for r in rows])
        m_old = m_ref[h, :, cols]
        m_new = jnp.maximum(m_old, jnp.max(m_blk, axis=0, keepdims=True))
        alpha = jnp.exp2(m_old - m_new)
        m_ref[h, :, cols] = m_new
        p = jnp.concatenate([jnp.exp2((chunk(r) - m_new).astype(BF16)) for r in rows], axis=0)
        vt1 = jnp.concatenate([vt_ref[j, hs, :], jnp.ones((B_SUM_ROWS, B_TK), BF16)], axis=0)
        acc_ref[h, :, cols] = alpha * acc_ref[h, :, cols] + _dot(vt1, p)

    def stage(j_score, dst, j_update, src, masked=False):
        for n in range(len(items)):
            if dst is not None:
                score_item(j_score, n, dst)
            if src is not None:
                update_item(j_update, n, src, masked)

    def pair(jj, carry):
        stage(2 * jj + 1, sb_ref, 2 * jj, sa_ref)
        stage(2 * jj + 2, sa_ref, 2 * jj + 1, sb_ref)
        return carry

    stage(0, sa_ref, None, None)
    lax.fori_loop(0, i // 2, pair, 0)

    @pl.when(i % 2 == 0)
    def _():
        stage(None, None, i, sa_ref, masked=True)

    @pl.when(i % 2 == 1)
    def _():
        stage(i, sb_ref, i - 1, sa_ref)
        stage(None, None, i, sb_ref, masked=True)

    lam = _diff_lambda(lq1_ref, lk1_ref, lq2_ref, lk2_ref, lambda_init)
    for h in range(heads):
        hs = slice(h * B_V_DIM, (h + 1) * B_V_DIM)
        o = acc_ref[h, :B_V_DIM, :] / acc_ref[h, B_V_DIM:B_V_DIM + 1, :]
        o = (o[:, :B_TQ] - lam * o[:, B_TQ:]).T
        o_ref[:, hs] = _subnorm_gate(o, subg_ref, g_ref[:, hs].astype(F32),
                                     lambda_init).astype(BF16)


def _attn_b_prompt(qbt, kb, vbt, gb, lams, subg, *, batch, seq, lambda_init):
    assert B_TQ == B_TK == ROW_TILE and B_HEADS % B_HEADS_PER_STEP == 0
    tiles = seq // B_TQ
    rows = batch * seq
    width = B_HEADS_PER_STEP * B_V_DIM
    n_items = B_HEADS_PER_STEP * 2 * B_TQ // B_STRIP
    qblk = lambda b, h, i: (b * tiles + i, h)
    vec = pl.BlockSpec((1, B_HEAD_DIM), lambda b, h, i: (0, 0))
    vbt = vbt.reshape(batch, tiles, B_WIDTH, B_TK)
    return pl.pallas_call(
        functools.partial(_attn_b_prompt_kernel, lambda_init=lambda_init),
        grid=(batch, B_HEADS // B_HEADS_PER_STEP, tiles),
        in_specs=[
            pl.BlockSpec((None, width, B_TQ), lambda b, h, i: (b * tiles + i, h, 0)),
            pl.BlockSpec((seq, width), lambda b, h, i: (b, h)),
            pl.BlockSpec((None, tiles, width, B_TK), lambda b, h, i: (b, 0, h, 0)),
            pl.BlockSpec((B_TQ, width), qblk),
            vec, vec, vec, vec,
            pl.BlockSpec((1, B_V_DIM), lambda b, h, i: (0, 0)),
        ],
        out_specs=pl.BlockSpec((B_TQ, width), qblk),
        out_shape=jax.ShapeDtypeStruct((rows, B_WIDTH), BF16),
        scratch_shapes=[pltpu.VMEM((B_HEADS_PER_STEP, B_V_DIM, 2 * B_TQ), BF16),
                        pltpu.VMEM((B_HEADS_PER_STEP, B_V_DIM + B_SUM_ROWS, 2 * B_TQ), F32),
                        pltpu.VMEM((B_HEADS_PER_STEP, 1, 2 * B_TQ), F32),
                        pltpu.VMEM((n_items, B_TK, B_STRIP), F32),
                        pltpu.VMEM((n_items, B_TK, B_STRIP), F32)],
        compiler_params=pltpu.CompilerParams(
            dimension_semantics=("arbitrary", "arbitrary", "arbitrary"),
            vmem_limit_bytes=VMEM_LIMIT),
        name="attn_b_prompt",
    )(qbt, kb, vbt, gb, *lams, subg)


def _roll_in(cache_t, new_rows):
    t = new_rows.shape[0]
    keep = cache_t.shape[1]
    shifted = pltpu.roll(cache_t, keep - t, 1)
    pad = jnp.concatenate([jnp.zeros((LANES - t, new_rows.shape[1]), F32), new_rows], axis=0)
    new_t = pad.T
    lane = lax.broadcasted_iota(jnp.int32, new_t.shape, 1)
    last = jnp.where(lane >= LANES - t, new_t, shifted[:, keep - LANES:])
    return jnp.concatenate([shifted[:, :keep - LANES], last], axis=1)


def _attn_a_sample_kernel(q_ref, kn_ref, vn_ref, knf_ref, vnf_ref, kc_ref, vc_ref, g_ref,
                          brow_ref, o_ref, ko_ref, vo_ref):
    t = q_ref.shape[0]
    keep = kc_ref.shape[1]
    lane = lax.broadcasted_iota(jnp.int32, (t, LANES), 1)
    low_half = lane < A_HEAD_DIM
    for pair in range(A_HEADS // 2):
        cs = slice(pair * LANES, (pair + 1) * LANES)
        qp = q_ref[:, cs]
        kc = kc_ref[cs, :].astype(BF16)
        vc = vc_ref[cs, :].astype(BF16)
        kn = kn_ref[:, cs]
        vn = vn_ref[:, cs]
        outs = []
        for e in range(2):
            hd = 2 * pair + e
            bias = _toeplitz(brow_ref[hd:hd + 1, :], t)
            qm = jnp.where(low_half if e == 0 else ~low_half, qp, jnp.zeros_like(qp))
            s_c = _dot(qm, kc) + bias[:, :keep]
            s_n = _nt_dot(qm, kn) + bias[:, keep:keep + t]
            m = jnp.maximum(jnp.max(s_c, axis=-1, keepdims=True),
                            jnp.max(s_n, axis=-1, keepdims=True))
            p_c = jnp.exp(s_c - m)
            p_n = jnp.exp(s_n - m)
            l = jnp.sum(p_c, axis=-1, keepdims=True) + jnp.sum(p_n, axis=-1, keepdims=True)
            o = _nt_dot(p_c.astype(BF16), vc) + _dot(p_n.astype(BF16), vn)
            outs.append(o / l)
        o_pair = jnp.where(low_half, outs[0], outs[1])
        o_ref[:, cs] = (o_pair * _silu(g_ref[:, cs].astype(F32))).astype(BF16)
    ko_ref[...] = _roll_in(kc_ref[...], knf_ref[...])
    vo_ref[...] = _roll_in(vc_ref[...], vnf_ref[...])


def _attn_a_sample(qa, ka, va, kaf, vaf, cache_kt, cache_vt, ga, bias_rows, *, batch, t):
    keep = cache_kt.shape[2]
    new = pl.BlockSpec((t, A_WIDTH), lambda b: (b, 0))
    cache = pl.BlockSpec((None, A_WIDTH, keep), lambda b: (b, 0, 0))
    return pl.pallas_call(
        _attn_a_sample_kernel,
        grid=(batch,),
        in_specs=[new, new, new, new, new, cache, cache, new,
                  pl.BlockSpec(bias_rows.shape, lambda b: (0, 0))],
        out_specs=[new, cache, cache],
        out_shape=[jax.ShapeDtypeStruct(qa.shape, BF16),
                   jax.ShapeDtypeStruct(cache_kt.shape, F32),
                   jax.ShapeDtypeStruct(cache_vt.shape, F32)],
        compiler_params=pltpu.CompilerParams(
            dimension_semantics=("arbitrary",), vmem_limit_bytes=VMEM_LIMIT),
        name="attn_a_sample",
    )(qa, ka, va, kaf, vaf, cache_kt, cache_vt, ga, bias_rows)


def _attn_b_sample_kernel(q_ref, kn_ref, vn_ref, kc_ref, vc_ref, g_ref,
                          lq1_ref, lk1_ref, lq2_ref, lk2_ref, subg_ref, o_ref, *, lambda_init):
    t = q_ref.shape[0]
    past = kc_ref.shape[1]
    lam = _diff_lambda(lq1_ref, lk1_ref, lq2_ref, lk2_ref, lambda_init)
    for h in range(B_HEADS):
        cs = slice(h * B_V_DIM, (h + 1) * B_V_DIM)
        qs = _stack_maps(q_ref[:, cs])
        s_c = _dot(qs, kc_ref[cs, :].astype(BF16))
        s_n = _nt_dot(qs, kn_ref[:, cs])
        m = jnp.maximum(jnp.max(s_c, axis=-1, keepdims=True),
                        jnp.max(s_n, axis=-1, keepdims=True))
        p_c = jnp.exp(s_c - m)
        p_n = jnp.exp(s_n - m)
        l = jnp.sum(p_c, axis=-1, keepdims=True) + jnp.sum(p_n, axis=-1, keepdims=True)
        p_c = p_c / l
        p_n = p_n / l
        a_c = (p_c[:t] - lam * p_c[t:]).astype(BF16)
        a_n = (p_n[:t] - lam * p_n[t:]).astype(BF16)
        vc = vc_ref[pl.ds(h, past, stride=B_HEADS), :].astype(BF16)
        o = _dot(a_c, vc) + _dot(a_n, vn_ref[:, cs])
        o_ref[:, cs] = _subnorm_gate(o, subg_ref, g_ref[:, cs].astype(F32),
                                     lambda_init).astype(BF16)


def _attn_b_sample(qb, kb, vb, cache_kt, cache_v, gb, lams, subg, *, batch, t, lambda_init):
    past = cache_kt.shape[2]
    new = pl.BlockSpec((t, B_WIDTH), lambda b: (b, 0))
    vec = pl.BlockSpec((1, B_HEAD_DIM), lambda b: (0, 0))
    return pl.pallas_call(
        functools.partial(_attn_b_sample_kernel, lambda_init=lambda_init),
        grid=(batch,),
        in_specs=[new, new, new,
                  pl.BlockSpec((None, B_WIDTH, past), lambda b: (b, 0, 0)),
                  pl.BlockSpec((None, past * B_HEADS, B_V_DIM), lambda b: (b, 0, 0)),
                  new, vec, vec, vec, vec,
                  pl.BlockSpec((1, B_V_DIM), lambda b: (0, 0))],
        out_specs=new,
        out_shape=jax.ShapeDtypeStruct(qb.shape, BF16),
        compiler_params=pltpu.CompilerParams(
            dimension_semantics=("arbitrary",), vmem_limit_bytes=VMEM_LIMIT),
        name="attn_b_sample",
    )(qb, kb, vb, cache_kt, cache_v, gb, *lams, subg)


def _out_kernel(oa_ref, ob_ref, wa_ref, wb_ref, x_ref, fg_ref, y_ref, *, final_norm):
    y = x_ref[...] + _dot(oa_ref[...], wa_ref[...]) + _dot(ob_ref[...], wb_ref[...])
    if final_norm:
        y = y * lax.rsqrt(jnp.mean(y * y, axis=-1, keepdims=True) + EPS) * fg_ref[...]
    y_ref[...] = y


def _out_proj(oa, ob, w_a, w_b, x, final_gain, *, tm, final_norm):
    rows, d_model = x.shape
    row_blk = lambda i: (i, 0)
    const2 = lambda i: (0, 0)
    return pl.pallas_call(
        functools.partial(_out_kernel, final_norm=final_norm),
        grid=(rows // tm,),
        in_specs=[
            pl.BlockSpec((tm, A_WIDTH), row_blk),
            pl.BlockSpec((tm, B_WIDTH), row_blk),
            pl.BlockSpec(w_a.shape, const2),
            pl.BlockSpec(w_b.shape, const2),
            pl.BlockSpec((tm, d_model), row_blk),
            pl.BlockSpec((1, d_model), const2),
        ],
        out_specs=pl.BlockSpec((tm, d_model), row_blk),
        out_shape=jax.ShapeDtypeStruct(x.shape, F32),
        compiler_params=pltpu.CompilerParams(
            dimension_semantics=("arbitrary",), vmem_limit_bytes=VMEM_LIMIT),
        name="out_proj",
    )(oa, ob, w_a, w_b, x, final_gain)


def _channel_major(a):
    n, pos = a.shape[:2]
    return jnp.moveaxis(a.reshape(n, pos, -1), 1, 2)


def _position_major(a, channel_dims):
    n, _, pos = a.shape
    return jnp.moveaxis(a, 2, 1).reshape(n, pos, *channel_dims)


def kernel(x_prompt, x_sample, cache_a_k, cache_a_v, cache_b_k, cache_b_v,
           norm_gain, w_in, w_out, rel_bias, lambda_q1, lambda_k1, lambda_q2, lambda_k2,
           subln_gain, final_gain):
    batch, seq, d_model = x_prompt.shape
    dec_batch, t_sample, _ = x_sample.shape
    depth = w_in.shape[0]
    past_len = cache_b_k.shape[2]
    a_keep = cache_a_k.shape[2]
    keep_prompt = min(BAND_ROWS, seq)
    assert seq % ROW_TILE == 0 and keep_prompt == ROW_TILE and seq % B_TQ == 0
    assert t_sample <= a_keep and t_sample <= LANES and d_model == 2 * SEG

    rows_p = batch * seq
    rows_s = dec_batch * t_sample
    tiles_per_seq = seq // ROW_TILE
    tables_p = _rope_tables(np.arange(ROW_TILE), np.arange(tiles_per_seq) * ROW_TILE)
    tables_s = _rope_tables(past_len + np.arange(rows_s) % t_sample, np.zeros(1))
    a_dims = (A_HEADS, A_HEAD_DIM)
    bk_dims = (B_HEADS, 2, B_HEAD_DIM)
    bv_dims = (B_HEADS, B_V_DIM)

    yp = x_prompt.reshape(rows_p, d_model)
    ys = x_sample.reshape(rows_s, d_model)
    fg = final_gain.reshape(1, d_model)
    outs = [[] for _ in range(8)]
    for l in range(depth):
        lambda_init = 0.8 - 0.6 * math.exp(-0.3 * l)
        last = l == depth - 1
        w_l = w_in[l].astype(BF16)
        w_a = w_out[l, :A_WIDTH].astype(BF16)
        w_b = w_out[l, A_WIDTH:].astype(BF16)
        gain = norm_gain[l].reshape(1, d_model)
        lams = [a[l].reshape(1, B_HEAD_DIM).astype(F32)
                for a in (lambda_q1, lambda_k1, lambda_q2, lambda_k2)]
        subg = subln_gain[l].reshape(1, B_V_DIM).astype(F32)

        qa, ka, va, ga, qb, kb, vb, gb, kaf, vaf, kbf, vbf = _proj(
            yp, gain, w_l, tables_p, tm=ROW_TILE, tiles_per_seq=tiles_per_seq,
            tail_every=tiles_per_seq, channel_major=True, qb_scale=LOG2E * B_HEAD_DIM ** -0.5)
        oa = _attn_a_prompt(qa, ka, va, ga, _bias_rows(rel_bias[l], BAND_ROWS, A_SUB, A_WIN),
                            batch=batch, seq=seq)
        ob = _attn_b_prompt(qb, kb, vb, gb, lams, subg, batch=batch, seq=seq,
                            lambda_init=lambda_init)
        yp = _out_proj(oa, ob, w_a, w_b, yp, fg, tm=ROW_TILE, final_norm=last)
        outs[0].append(_position_major(kaf, a_dims))
        outs[1].append(_position_major(vaf, a_dims))
        outs[2].append(_position_major(kbf, bk_dims))
        outs[3].append(vbf.reshape(batch, seq, *bv_dims))

        qa, ka, va, ga, qb, kb, vb, gb, kaf, vaf, kbf, vbf = _proj(
            ys, gain, w_l, tables_s, tm=rows_s, tiles_per_seq=1, tail_every=1,
            channel_major=False, qb_scale=B_HEAD_DIM ** -0.5)
        oa, ak_new, av_new = _attn_a_sample(
            qa, ka, va, kaf, vaf, _channel_major(cache_a_k[l]), _channel_major(cache_a_v[l]),
            ga, _bias_rows(rel_bias[l], a_keep, t_sample, a_keep + t_sample),
            batch=dec_batch, t=t_sample)
        ob = _attn_b_sample(
            qb, kb, vb, _channel_major(cache_b_k[l]),
            cache_b_v[l].reshape(dec_batch, past_len * B_HEADS, B_V_DIM),
            gb, lams, subg, batch=dec_batch, t=t_sample, lambda_init=lambda_init)
        ys = _out_proj(oa, ob, w_a, w_b, ys, fg, tm=rows_s, final_norm=last)
        outs[4].append(_position_major(ak_new, a_dims))
        outs[5].append(_position_major(av_new, a_dims))
        outs[6].append(kbf.reshape(dec_batch, t_sample, *bk_dims))
        outs[7].append(vbf.reshape(dec_batch, t_sample, *bv_dims))

    return (yp.reshape(batch, seq, d_model), ys.reshape(dec_batch, t_sample, d_model),
            *[jnp.stack(o) for o in outs])
```

```python
import functools
import math

import numpy as np
import jax
import jax.numpy as jnp
from jax import lax
from jax.experimental import pallas as pl
from jax.experimental.pallas import tpu as pltpu

F32 = jnp.float32
BF16 = jnp.bfloat16
NEG_INF = float("-inf")

CHUNK = 64
N_PREV_CHUNKS = 8
A_HEADS = 8
A_HEAD_DIM = 64
A_WIDTH = A_HEADS * A_HEAD_DIM
B_HEADS = 4
B_HEAD_DIM = 64
B_V_DIM = 2 * B_HEAD_DIM
B_WIDTH = B_HEADS * B_V_DIM
SEG = 512
MAX_REL = 128
ROPE_THETA = 500000.0
ROPE_DIM = 16
EPS = 1e-6
LOG2E = math.log2(math.e)
LANES = 128
VMEM_LIMIT = 56 * 1024 * 1024

ROW_TILE = 512
BAND_ROWS = N_PREV_CHUNKS * CHUNK
A_QB = 256
A_WIN = BAND_ROWS + A_QB
A_ROWS = 64
A_QK_SPLIT = 3
A_SUM_ROWS = 16
B_TQ = 512
B_TK = 512
B_HEADS_PER_STEP = 2
B_STRIP = 256
B_ROWS = 64
B_SUM_ROWS = 16

NT_DIMS = (((1,), (1,)), ((), ()))


def _nt_dot(a, b):
    return lax.dot_general(a, b, NT_DIMS, preferred_element_type=F32)


def _dot(a, b):
    return jnp.dot(a, b, preferred_element_type=F32)


def _silu(g):
    return g / (1.0 + jnp.exp(-g))


def _round_up(n, m):
    return -(-n // m) * m


def _rope_lane_freq():
    d = np.arange(LANES) % B_HEAD_DIM
    inv = ROPE_THETA ** (-np.arange(0, ROPE_DIM, 2, dtype=np.float64) / ROPE_DIM)
    return np.where(d < ROPE_DIM, inv[d % (ROPE_DIM // 2)], 0.0)


def _rope_tables(row_pos, base_pos):
    f = _rope_lane_freq()[None, :]
    ar = np.asarray(row_pos, np.float64)[:, None] * f
    ab = np.asarray(base_pos, np.float64)[:, None] * f
    as32 = lambda a: jnp.asarray(a.astype(np.float32))
    return (as32(np.cos(ar)), as32(np.sin(ar)),
            as32(np.cos(ab))[:, None, :], as32(np.sin(ab))[:, None, :])


def _bias_rows(rel_bias, t_min, t_max, rel_of_t, scale=1.0):
    width = _round_up(t_max - t_min + 1, LANES)
    t = np.arange(width)
    t = np.where(t <= t_max, t, t - width)
    idx = np.clip(rel_of_t(t), -MAX_REL, MAX_REL) + MAX_REL
    return rel_bias.astype(F32)[:, idx] * scale


def _toeplitz(row, n_rows):
    return pltpu.roll(jnp.broadcast_to(row, (n_rows, row.shape[-1])), 0, 1, stride=1, stride_axis=0)


def _proj_kernel(x_ref, g_ref, w_ref, cr_ref, sr_ref, cb_ref, sb_ref, *out_refs,
                 tail_every, channel_major, q_scale):
    (qa_ref, ka_ref, va_ref, ga_ref, qb_ref, kb_ref, vb_ref, gb_ref,
     kaf_ref, vaf_ref, kbf_ref, vbf_ref) = out_refs
    x = x_ref[...]
    inv = lax.rsqrt(jnp.mean(x * x, axis=-1, keepdims=True) + EPS)
    h = (x * inv * g_ref[...]).astype(BF16)

    def seg(k):
        return _dot(h, w_ref[:, k * SEG:(k + 1) * SEG])

    cb, sb = cb_ref[0], sb_ref[0]
    cr, sr = cr_ref[...], sr_ref[...]
    cos = cb * cr - sb * sr
    sin = sb * cr + cb * sr
    d = lax.broadcasted_iota(jnp.int32, cos.shape, 1) % B_HEAD_DIM
    sin_lo = jnp.where(d < ROPE_DIM // 2, -sin, 0.0)
    sin_hi = jnp.where(d >= ROPE_DIM // 2, sin, 0.0)

    def rope(z):
        cols = []
        for c in range(SEG // LANES):
            zc = z[:, c * LANES:(c + 1) * LANES]
            up = pltpu.roll(zc, LANES - ROPE_DIM // 2, 1)
            dn = pltpu.roll(zc, ROPE_DIM // 2, 1)
            cols.append(zc * cos + up * sin_lo + dn * sin_hi)
        return jnp.concatenate(cols, axis=1)

    is_tail = (pl.program_id(0) % tail_every) == tail_every - 1

    qa = seg(0) * q_scale
    qa_ref[...] = (qa.T if channel_major else qa).astype(BF16)
    ka = seg(1)
    ka_ref[...] = ka.astype(BF16)
    va = seg(2)
    va_t = va.T if channel_major else va
    va_ref[...] = va_t.astype(BF16)

    @pl.when(is_tail)
    def _():
        kaf_ref[...] = ka.T if channel_major else ka
        vaf_ref[...] = va_t

    ga_ref[...] = seg(3).astype(BF16)
    qb = rope(seg(4)) * q_scale
    qb_ref[...] = (qb.T if channel_major else qb).astype(BF16)
    kb = rope(seg(5))
    kbf_ref[...] = kb.T if channel_major else kb
    kb_ref[...] = kb.astype(BF16)
    vb = seg(6)
    vbf_ref[...] = vb
    vb_ref[...] = (vb.T if channel_major else vb).astype(BF16)
    gb_ref[...] = seg(7).astype(BF16)


def _proj(x, gain, w_bf16, tables, *, tm, tiles_per_seq, tail_every, channel_major, q_scale):
    rows, d_model = x.shape
    n_tiles = rows // tm
    n_tail = n_tiles // tail_every
    n_seq = n_tiles // tiles_per_seq
    cr, sr, cb, sb = tables
    row_blk = lambda i: (i, 0)
    const2 = lambda i: (0, 0)
    base_blk = lambda i: (i % tiles_per_seq, 0, 0)
    bf = jax.ShapeDtypeStruct((rows, SEG), BF16)
    f32_full = jax.ShapeDtypeStruct((rows, SEG), F32)
    full_spec = pl.BlockSpec((tm, SEG), row_blk)
    if channel_major:
        assert tm == SEG
        tail = jax.ShapeDtypeStruct((n_tail, SEG, tm), F32)
        tail_spec = pl.BlockSpec((None, SEG, tm), lambda i: (i // tail_every, 0, 0))
        kbf = jax.ShapeDtypeStruct((n_seq, SEG, tiles_per_seq * tm), F32)
        kbf_spec = pl.BlockSpec((None, SEG, tm), lambda i: (i // tiles_per_seq, 0, i % tiles_per_seq))
        bf_t = jax.ShapeDtypeStruct((n_tiles, SEG, tm), BF16)
        bf_t_spec = pl.BlockSpec((None, SEG, tm), lambda i: (i, 0, 0))
    else:
        tail = jax.ShapeDtypeStruct((n_tail * tm, SEG), F32)
        tail_spec = pl.BlockSpec((tm, SEG), lambda i: (i // tail_every, 0))
        kbf, kbf_spec = f32_full, full_spec
        bf_t, bf_t_spec = bf, full_spec
    return pl.pallas_call(
        functools.partial(_proj_kernel, tail_every=tail_every, channel_major=channel_major,
                          q_scale=q_scale),
        grid=(n_tiles,),
        in_specs=[
            pl.BlockSpec((tm, d_model), row_blk),
            pl.BlockSpec((1, d_model), const2),
            pl.BlockSpec(w_bf16.shape, const2),
            pl.BlockSpec((tm, LANES), const2),
            pl.BlockSpec((tm, LANES), const2),
            pl.BlockSpec((1, 1, LANES), base_blk),
            pl.BlockSpec((1, 1, LANES), base_blk),
        ],
        out_specs=([bf_t_spec, full_spec, bf_t_spec, full_spec] * 2
                   + [tail_spec, tail_spec, kbf_spec, full_spec]),
        out_shape=[bf_t, bf, bf_t, bf] * 2 + [tail, tail, kbf, f32_full],
        compiler_params=pltpu.CompilerParams(
            dimension_semantics=("arbitrary",), vmem_limit_bytes=VMEM_LIMIT),
        name="proj",
    )(x, gain, w_bf16, cr, sr, cb, sb)


def _attn_a_prompt_kernel(qt_ref, kp_ref, kc_ref, vtp_ref, vtc_ref, g_ref, brow_ref, o_ref,
                          kcat_ref, vcat_ref, rhs_ref, bias_ref, ot_ref, sa_ref, sb_ref, sc_ref):
    n = pl.program_id(1)

    @pl.when((pl.program_id(0) == 0) & (n == 0))
    def _():
        j = lax.broadcasted_iota(jnp.int32, (A_WIN, A_QB), 0)
        i = lax.broadcasted_iota(jnp.int32, (A_WIN, A_QB), 1)
        gap = i // CHUNK + N_PREV_CHUNKS - j // CHUNK
        band = (gap >= 0) & (gap <= N_PREV_CHUNKS)
        for hd in range(A_HEADS):
            t = _toeplitz(brow_ref[hd:hd + 1, :], A_WIN)[:, :A_QB]
            bias_ref[hd] = jnp.where(band, t, NEG_INF)

    kcat_ref[0:ROW_TILE, :] = kp_ref[...]
    kcat_ref[ROW_TILE:2 * ROW_TILE, :] = kc_ref[...]
    vcat_ref[:, 0:ROW_TILE] = vtp_ref[...]
    vcat_ref[:, ROW_TILE:2 * ROW_TILE] = vtc_ref[...]
    for pair in range(A_HEADS // 2):
        qt = qt_ref[pair * LANES:(pair + 1) * LANES, :]
        row = lax.broadcasted_iota(jnp.int32, qt.shape, 0)
        zero = jnp.zeros_like(qt)
        rhs_ref[2 * pair] = jnp.where(row < A_HEAD_DIM, qt, zero)
        rhs_ref[2 * pair + 1] = jnp.where(row >= A_HEAD_DIM, qt, zero)

    items = [(q0, hd) for q0 in range(0, ROW_TILE, A_QB) for hd in range(A_HEADS)]

    def score_item(t, dst):
        q0, hd = items[t]
        pair = hd // 2
        half = A_WIN // A_QK_SPLIT
        for r in range(0, A_WIN, half):
            dst[r:r + half, :] = _dot(
                kcat_ref[q0 + r:q0 + r + half, pair * LANES:(pair + 1) * LANES],
                rhs_ref[hd, :, q0:q0 + A_QB]) + bias_ref[hd, r:r + half, :]

    def update_item(t, src, first_tile):
        q0, hd = items[t]
        r_min = BAND_ROWS - q0 if first_tile else 0
        rows = range(0, A_WIN, A_ROWS)
        live = [r for r in rows if r >= r_min]
        m = jnp.max(functools.reduce(jnp.maximum, [src[r:r + A_ROWS, :] for r in live]),
                    axis=0, keepdims=True)
        p = jnp.concatenate(
            [jnp.exp2((src[r:r + A_ROWS, :] - m).astype(BF16)) if r >= r_min
             else jnp.zeros((A_ROWS, A_QB), BF16) for r in rows], axis=0)
        vt1 = jnp.concatenate([vcat_ref[hd * A_HEAD_DIM:(hd + 1) * A_HEAD_DIM, q0:q0 + A_WIN],
                               jnp.ones((A_SUM_ROWS, A_WIN), BF16)], axis=0)
        o = _dot(vt1, p)
        ot_ref[hd * A_HEAD_DIM:(hd + 1) * A_HEAD_DIM, q0:q0 + A_QB] = (
            o[:A_HEAD_DIM] / o[A_HEAD_DIM:A_HEAD_DIM + 1])

    def run(first_tile):
        bufs = (sa_ref, sb_ref, sc_ref)
        depth = len(bufs) - 1
        for t in range(depth):
            score_item(t, bufs[t % len(bufs)])
        for t in range(len(items)):
            if t + depth < len(items):
                score_item(t + depth, bufs[(t + depth) % len(bufs)])
            update_item(t, bufs[t % len(bufs)], first_tile)

    @pl.when(n == 0)
    def _():
        run(True)

    @pl.when(n > 0)
    def _():
        run(False)

    o_ref[...] = (ot_ref[...].T * _silu(g_ref[...].astype(F32))).astype(BF16)


def _attn_a_prompt(qat, ka, vat, ga, bias_rows, *, batch, seq):
    tiles = seq // ROW_TILE
    cur = lambda b, n: (b * tiles + n, 0)
    prev = lambda b, n: (b * tiles + jnp.maximum(n - 1, 0), 0)
    cur_t = lambda b, n: (b * tiles + n, 0, 0)
    prev_t = lambda b, n: (b * tiles + jnp.maximum(n - 1, 0), 0, 0)
    blk = (ROW_TILE, A_WIDTH)
    blk_t = (None, A_WIDTH, ROW_TILE)
    return pl.pallas_call(
        _attn_a_prompt_kernel,
        grid=(batch, tiles),
        in_specs=[
            pl.BlockSpec(blk_t, cur_t),
            pl.BlockSpec(blk, prev), pl.BlockSpec(blk, cur),
            pl.BlockSpec(blk_t, prev_t), pl.BlockSpec(blk_t, cur_t),
            pl.BlockSpec(blk, cur),
            pl.BlockSpec(bias_rows.shape, lambda b, n: (0, 0)),
        ],
        out_specs=pl.BlockSpec(blk, cur),
        out_shape=jax.ShapeDtypeStruct(ka.shape, BF16),
        scratch_shapes=[pltpu.VMEM((2 * ROW_TILE, A_WIDTH), BF16),
                        pltpu.VMEM((A_WIDTH, 2 * ROW_TILE), BF16),
                        pltpu.VMEM((A_HEADS, LANES, ROW_TILE), BF16),
                        pltpu.VMEM((A_HEADS, A_WIN, A_QB), F32),
                        pltpu.VMEM((A_WIDTH, ROW_TILE), F32),
                        pltpu.VMEM((A_WIN, A_QB), F32),
                        pltpu.VMEM((A_WIN, A_QB), F32),
                        pltpu.VMEM((A_WIN, A_QB), F32)],
        compiler_params=pltpu.CompilerParams(
            dimension_semantics=("arbitrary", "arbitrary"), vmem_limit_bytes=VMEM_LIMIT),
        name="attn_a_prompt",
    )(qat, ka, ka, vat, vat, ga, bias_rows)


def _diff_lambda(lq1_ref, lk1_ref, lq2_ref, lk2_ref, lambda_init):
    e1 = jnp.exp(jnp.sum(lq1_ref[...] * lk1_ref[...], axis=-1, keepdims=True))
    e2 = jnp.exp(jnp.sum(lq2_ref[...] * lk2_ref[...], axis=-1, keepdims=True))
    return e1 - e2 + lambda_init


def _stack_maps(qh):
    lane = lax.broadcasted_iota(jnp.int32, qh.shape, 1)
    zero = jnp.zeros_like(qh)
    return jnp.concatenate([jnp.where(lane < B_HEAD_DIM, qh, zero),
                            jnp.where(lane >= B_HEAD_DIM, qh, zero)], axis=0)


def _subnorm_gate(o, subg_ref, gate, lambda_init):
    o = o * lax.rsqrt(jnp.mean(o * o, axis=-1, keepdims=True) + EPS) * subg_ref[...]
    return (o * (1.0 - lambda_init)) * _silu(gate)


def _attn_b_prompt_kernel(qt_ref, k_ref, vt_ref, g_ref, lq1_ref, lk1_ref, lq2_ref, lk2_ref,
                          subg_ref, o_ref, rhs_ref, acc_ref, m_ref, sa_ref, sb_ref, *, lambda_init):
    i = pl.program_id(2)
    heads = rhs_ref.shape[0]
    for h in range(heads):
        qt = qt_ref[h * B_V_DIM:(h + 1) * B_V_DIM, :]
        row = lax.broadcasted_iota(jnp.int32, qt.shape, 0)
        zero = jnp.zeros_like(qt)
        rhs_ref[h, :, :B_TQ] = jnp.where(row < B_HEAD_DIM, qt, zero)
        rhs_ref[h, :, B_TQ:] = jnp.where(row >= B_HEAD_DIM, qt, zero)
    m_ref[...] = jnp.full(m_ref.shape, NEG_INF, F32)
    acc_ref[...] = jnp.zeros(acc_ref.shape, F32)

    items = [(h, c) for h in range(heads) for c in range(2 * B_TQ // B_STRIP)]

    def score_item(j, n, dst):
        h, c = items[n]
        k0 = pl.multiple_of(j * B_TK, B_TK)
        dst[n] = _dot(k_ref[pl.ds(k0, B_TK), h * B_V_DIM:(h + 1) * B_V_DIM],
                      rhs_ref[h, :, c * B_STRIP:(c + 1) * B_STRIP])

    def update_item(j, n, src, masked):
        h, c = items[n]
        hs = slice(h * B_V_DIM, (h + 1) * B_V_DIM)
        cols = slice(c * B_STRIP, (c + 1) * B_STRIP)

        def chunk(r):
            s = src[n, r:r + B_ROWS, :]
            if masked:
                q0 = (c * B_STRIP) % B_TQ
                kc = (r + lax.broadcasted_iota(jnp.int32, s.shape, 0)) // CHUNK
                qc = (q0 + lax.broadcasted_iota(jnp.int32, s.shape, 1)) // CHUNK
                s = jnp.where(kc <= qc, s, NEG_INF)
            return s

        rows = range(0, B_TK, B_ROWS)
        m_blk = functools.reduce(jnp.maximum, [chunk(r) for r in rows])
        m_old = m_ref[h, :, cols]
        m_new = jnp.maximum(m_old, jnp.max(m_blk, axis=0, keepdims=True))
        alpha = jnp.exp2(m_old - m_new)
        m_ref[h, :, cols] = m_new
        p = jnp.concatenate([jnp.exp2((chunk(r) - m_new).astype(BF16)) for r in rows], axis=0)
        vt1 = jnp.concatenate([vt_ref[j, hs, :], jnp.ones((B_SUM_ROWS, B_TK), BF16)], axis=0)
        acc_ref[h, :, cols] = alpha * acc_ref[h, :, cols] + _dot(vt1, p)

    def stage(j_score, dst, j_update, src, masked=False):
        for n in range(len(items)):
            if dst is not None:
                score_item(j_score, n, dst)
            if src is not None:
                update_item(j_update, n, src, masked)

    def pair(jj, carry):
        stage(2 * jj + 1, sb_ref, 2 * jj, sa_ref)
        stage(2 * jj + 2, sa_ref, 2 * jj + 1, sb_ref)
        return carry

    stage(0, sa_ref, None, None)
    lax.fori_loop(0, i // 2, pair, 0)

    @pl.when(i % 2 == 0)
    def _():
        stage(None, None, i, sa_ref, masked=True)

    @pl.when(i % 2 == 1)
    def _():
        stage(i, sb_ref, i - 1, sa_ref)
        stage(None, None, i, sb_ref, masked=True)

    lam = _diff_lambda(lq1_ref, lk1_ref, lq2_ref, lk2_ref, lambda_init)
    for h in range(heads):
        hs = slice(h * B_V_DIM, (h + 1) * B_V_DIM)
        o = acc_ref[h, :B_V_DIM, :] / acc_ref[h, B_V_DIM:B_V_DIM + 1, :]
        o = (o[:, :B_TQ] - lam * o[:, B_TQ:]).T
        o_ref[:, hs] = _subnorm_gate(o, subg_ref, g_ref[:, hs].astype(F32),
                                     lambda_init).astype(BF16)


def _attn_b_prompt(qbt, kb, vbt, gb, lams, subg, *, batch, seq, lambda_init):
    assert B_TQ == B_TK == ROW_TILE and B_HEADS % B_HEADS_PER_STEP == 0
    tiles = seq // B_TQ
    rows = batch * seq
    width = B_HEADS_PER_STEP * B_V_DIM
    n_items = B_HEADS_PER_STEP * 2 * B_TQ // B_STRIP
    qblk = lambda b, h, i: (b * tiles + i, h)
    vec = pl.BlockSpec((1, B_HEAD_DIM), lambda b, h, i: (0, 0))
    vbt = vbt.reshape(batch, tiles, B_WIDTH, B_TK)
    return pl.pallas_call(
        functools.partial(_attn_b_prompt_kernel, lambda_init=lambda_init),
        grid=(batch, B_HEADS // B_HEADS_PER_STEP, tiles),
        in_specs=[
            pl.BlockSpec((None, width, B_TQ), lambda b, h, i: (b * tiles + i, h, 0)),
            pl.BlockSpec((seq, width), lambda b, h, i: (b, h)),
            pl.BlockSpec((None, tiles, width, B_TK), lambda b, h, i: (b, 0, h, 0)),
            pl.BlockSpec((B_TQ, width), qblk),
            vec, vec, vec, vec,
            pl.BlockSpec((1, B_V_DIM), lambda b, h, i: (0, 0)),
        ],
        out_specs=pl.BlockSpec((B_TQ, width), qblk),
        out_shape=jax.ShapeDtypeStruct((rows, B_WIDTH), BF16),
        scratch_shapes=[pltpu.VMEM((B_HEADS_PER_STEP, B_V_DIM, 2 * B_TQ), BF16),
                        pltpu.VMEM((B_HEADS_PER_STEP, B_V_DIM + B_SUM_ROWS, 2 * B_TQ), F32),
                        pltpu.VMEM((B_HEADS_PER_STEP, 1, 2 * B_TQ), F32),
                        pltpu.VMEM((n_items, B_TK, B_STRIP), F32),
                        pltpu.VMEM((n_items, B_TK, B_STRIP), F32)],
        compiler_params=pltpu.CompilerParams(
            dimension_semantics=("arbitrary", "arbitrary", "arbitrary"),
            vmem_limit_bytes=VMEM_LIMIT),
        name="attn_b_prompt",
    )(qbt, kb, vbt, gb, *lams, subg)


def _roll_in(cache_t, new_rows):
    t = new_rows.shape[0]
    keep = cache_t.shape[1]
    shifted = pltpu.roll(cache_t, keep - t, 1)
    pad = jnp.concatenate([jnp.zeros((LANES - t, new_rows.shape[1]), F32), new_rows], axis=0)
    new_t = pad.T
    lane = lax.broadcasted_iota(jnp.int32, new_t.shape, 1)
    last = jnp.where(lane >= LANES - t, new_t, shifted[:, keep - LANES:])
    return jnp.concatenate([shifted[:, :keep - LANES], last], axis=1)


def _attn_a_sample_kernel(q_ref, kn_ref, vn_ref, knf_ref, vnf_ref, kc_ref, vc_ref, g_ref,
                          brow_ref, o_ref, ko_ref, vo_ref):
    t = q_ref.shape[0]
    keep = kc_ref.shape[1]
    lane = lax.broadcasted_iota(jnp.int32, (t, LANES), 1)
    low_half = lane < A_HEAD_DIM
    for pair in range(A_HEADS // 2):
        cs = slice(pair * LANES, (pair + 1) * LANES)
        qp = q_ref[:, cs]
        kc = kc_ref[cs, :].astype(BF16)
        vc = vc_ref[cs, :].astype(BF16)
        kn = kn_ref[:, cs]
        vn = vn_ref[:, cs]
        outs = []
        for e in range(2):
            hd = 2 * pair + e
            bias = _toeplitz(brow_ref[hd:hd + 1, :], t)
            qm = jnp.where(low_half if e == 0 else ~low_half, qp, jnp.zeros_like(qp))
            s_c = _dot(qm, kc) + bias[:, :keep]
            s_n = _nt_dot(qm, kn) + bias[:, keep:keep + t]
            m = jnp.maximum(jnp.max(s_c, axis=-1, keepdims=True),
                            jnp.max(s_n, axis=-1, keepdims=True))
            p_c = jnp.exp(s_c - m)
            p_n = jnp.exp(s_n - m)
            l = jnp.sum(p_c, axis=-1, keepdims=True) + jnp.sum(p_n, axis=-1, keepdims=True)
            o = _nt_dot(p_c.astype(BF16), vc) + _dot(p_n.astype(BF16), vn)
            outs.append(o / l)
        o_pair = jnp.where(low_half, outs[0], outs[1])
        o_ref[:, cs] = (o_pair * _silu(g_ref[:, cs].astype(F32))).astype(BF16)
    ko_ref[...] = _roll_in(kc_ref[...], knf_ref[...])
    vo_ref[...] = _roll_in(vc_ref[...], vnf_ref[...])


def _attn_a_sample(qa, ka, va, kaf, vaf, cache_kt, cache_vt, ga, bias_rows, *, batch, t):
    keep = cache_kt.shape[2]
    new = pl.BlockSpec((t, A_WIDTH), lambda b: (b, 0))
    cache = pl.BlockSpec((None, A_WIDTH, keep), lambda b: (b, 0, 0))
    return pl.pallas_call(
        _attn_a_sample_kernel,
        grid=(batch,),
        in_specs=[new, new, new, new, new, cache, cache, new,
                  pl.BlockSpec(bias_rows.shape, lambda b: (0, 0))],
        out_specs=[new, cache, cache],
        out_shape=[jax.ShapeDtypeStruct(qa.shape, BF16),
                   jax.ShapeDtypeStruct(cache_kt.shape, F32),
                   jax.ShapeDtypeStruct(cache_vt.shape, F32)],
        compiler_params=pltpu.CompilerParams(
            dimension_semantics=("arbitrary",), vmem_limit_bytes=VMEM_LIMIT),
        name="attn_a_sample",
    )(qa, ka, va, kaf, vaf, cache_kt, cache_vt, ga, bias_rows)


def _attn_b_sample_kernel(q_ref, kn_ref, vn_ref, kc_ref, vc_ref, g_ref,
                          lq1_ref, lk1_ref, lq2_ref, lk2_ref, subg_ref, o_ref, *, lambda_init):
    t = q_ref.shape[0]
    past = kc_ref.shape[1]
    lam = _diff_lambda(lq1_ref, lk1_ref, lq2_ref, lk2_ref, lambda_init)
    for h in range(B_HEADS):
        cs = slice(h * B_V_DIM, (h + 1) * B_V_DIM)
        qs = _stack_maps(q_ref[:, cs])
        s_c = _dot(qs, kc_ref[cs, :].astype(BF16))
        s_n = _nt_dot(qs, kn_ref[:, cs])
        m = jnp.maximum(jnp.max(s_c, axis=-1, keepdims=True),
                        jnp.max(s_n, axis=-1, keepdims=True))
        p_c = jnp.exp(s_c - m)
        p_n = jnp.exp(s_n - m)
        l = jnp.sum(p_c, axis=-1, keepdims=True) + jnp.sum(p_n, axis=-1, keepdims=True)
        p_c = p_c / l
        p_n = p_n / l
        a_c = (p_c[:t] - lam * p_c[t:]).astype(BF16)
        a_n = (p_n[:t] - lam * p_n[t:]).astype(BF16)
        vc = vc_ref[pl.ds(h, past, stride=B_HEADS), :].astype(BF16)
        o = _dot(a_c, vc) + _dot(a_n, vn_ref[:, cs])
        o_ref[:, cs] = _subnorm_gate(o, subg_ref, g_ref[:, cs].astype(F32),
                                     lambda_init).astype(BF16)


def _attn_b_sample(qb, kb, vb, cache_kt, cache_v, gb, lams, subg, *, batch, t, lambda_init):
    past = cache_kt.shape[2]
    new = pl.BlockSpec((t, B_WIDTH), lambda b: (b, 0))
    vec = pl.BlockSpec((1, B_HEAD_DIM), lambda b: (0, 0))
    return pl.pallas_call(
        functools.partial(_attn_b_sample_kernel, lambda_init=lambda_init),
        grid=(batch,),
        in_specs=[new, new, new,
                  pl.BlockSpec((None, B_WIDTH, past), lambda b: (b, 0, 0)),
                  pl.BlockSpec((None, past * B_HEADS, B_V_DIM), lambda b: (b, 0, 0)),
                  new, vec, vec, vec, vec,
                  pl.BlockSpec((1, B_V_DIM), lambda b: (0, 0))],
        out_specs=new,
        out_shape=jax.ShapeDtypeStruct(qb.shape, BF16),
        compiler_params=pltpu.CompilerParams(
            dimension_semantics=("arbitrary",), vmem_limit_bytes=VMEM_LIMIT),
        name="attn_b_sample",
    )(qb, kb, vb, cache_kt, cache_v, gb, *lams, subg)


def _out_kernel(oa_ref, ob_ref, wa_ref, wb_ref, x_ref, fg_ref, y_ref, *, final_norm):
    y = x_ref[...] + _dot(oa_ref[...], wa_ref[...]) + _dot(ob_ref[...], wb_ref[...])
    if final_norm:
        y = y * lax.rsqrt(jnp.mean(y * y, axis=-1, keepdims=True) + EPS) * fg_ref[...]
    y_ref[...] = y


def _out_proj(oa, ob, w_a, w_b, x, final_gain, *, tm, final_norm):
    rows, d_model = x.shape
    row_blk = lambda i: (i, 0)
    const2 = lambda i: (0, 0)
    return pl.pallas_call(
        functools.partial(_out_kernel, final_norm=final_norm),
        grid=(rows // tm,),
        in_specs=[
            pl.BlockSpec((tm, A_WIDTH), row_blk),
            pl.BlockSpec((tm, B_WIDTH), row_blk),
            pl.BlockSpec(w_a.shape, const2),
            pl.BlockSpec(w_b.shape, const2),
            pl.BlockSpec((tm, d_model), row_blk),
            pl.BlockSpec((1, d_model), const2),
        ],
        out_specs=pl.BlockSpec((tm, d_model), row_blk),
        out_shape=jax.ShapeDtypeStruct(x.shape, F32),
        compiler_params=pltpu.CompilerParams(
            dimension_semantics=("arbitrary",), vmem_limit_bytes=VMEM_LIMIT),
        name="out_proj",
    )(oa, ob, w_a, w_b, x, final_gain)


def _channel_major(a):
    n, pos = a.shape[:2]
    return jnp.moveaxis(a.reshape(n, pos, -1), 1, 2)


def _position_major(a, channel_dims):
    n, _, pos = a.shape
    return jnp.moveaxis(a, 2, 1).reshape(n, pos, *channel_dims)


def kernel(x_prompt, x_sample, cache_a_k, cache_a_v, cache_b_k, cache_b_v,
           norm_gain, w_in, w_out, rel_bias, lambda_q1, lambda_k1, lambda_q2, lambda_k2,
           subln_gain, final_gain):
    batch, seq, d_model = x_prompt.shape
    dec_batch, t_sample, _ = x_sample.shape
    depth = w_in.shape[0]
    past_len = cache_b_k.shape[2]
    a_keep = cache_a_k.shape[2]
    keep_prompt = min(BAND_ROWS, seq)
    assert seq % ROW_TILE == 0 and keep_prompt == ROW_TILE and seq % B_TQ == 0
    assert t_sample <= a_keep and t_sample <= LANES and d_model == 2 * SEG

    rows_p = batch * seq
    rows_s = dec_batch * t_sample
    tiles_per_seq = seq // ROW_TILE
    tables_p = _rope_tables(np.arange(ROW_TILE), np.arange(tiles_per_seq) * ROW_TILE)
    tables_s = _rope_tables(past_len + np.arange(rows_s) % t_sample, np.zeros(1))
    a_dims = (A_HEADS, A_HEAD_DIM)
    bk_dims = (B_HEADS, 2, B_HEAD_DIM)
    bv_dims = (B_HEADS, B_V_DIM)

    yp = x_prompt.reshape(rows_p, d_model)
    ys = x_sample.reshape(rows_s, d_model)
    fg = final_gain.reshape(1, d_model)
    outs = [[] for _ in range(8)]
    for l in range(depth):
        lambda_init = 0.8 - 0.6 * math.exp(-0.3 * l)
        last = l == depth - 1
        w_l = w_in[l].astype(BF16)
        w_a = w_out[l, :A_WIDTH].astype(BF16)
        w_b = w_out[l, A_WIDTH:].astype(BF16)
        gain = norm_gain[l].reshape(1, d_model)
        lams = [a[l].reshape(1, B_HEAD_DIM).astype(F32)
                for a in (lambda_q1, lambda_k1, lambda_q2, lambda_k2)]
        subg = subln_gain[l].reshape(1, B_V_DIM).astype(F32)

        qa, ka, va, ga, qb, kb, vb, gb, kaf, vaf, kbf, vbf = _proj(
            yp, gain, w_l, tables_p, tm=ROW_TILE, tiles_per_seq=tiles_per_seq,
            tail_every=tiles_per_seq, channel_major=True, q_scale=LOG2E * B_HEAD_DIM ** -0.5)
        bias_p = _bias_rows(rel_bias[l], 1 - A_WIN, A_QB - 1, lambda t: t + BAND_ROWS, LOG2E)
        oa = _attn_a_prompt(qa, ka, va, ga, bias_p, batch=batch, seq=seq)
        ob = _attn_b_prompt(qb, kb, vb, gb, lams, subg, batch=batch, seq=seq,
                            lambda_init=lambda_init)
        yp = _out_proj(oa, ob, w_a, w_b, yp, fg, tm=ROW_TILE, final_norm=last)
        outs[0].append(_position_major(kaf, a_dims))
        outs[1].append(_position_major(vaf, a_dims))
        outs[2].append(_position_major(kbf, bk_dims))
        outs[3].append(vbf.reshape(batch, seq, *bv_dims))

        qa, ka, va, ga, qb, kb, vb, gb, kaf, vaf, kbf, vbf = _proj(
            ys, gain, w_l, tables_s, tm=rows_s, tiles_per_seq=1, tail_every=1,
            channel_major=False, q_scale=B_HEAD_DIM ** -0.5)
        oa, ak_new, av_new = _attn_a_sample(
            qa, ka, va, kaf, vaf, _channel_major(cache_a_k[l]), _channel_major(cache_a_v[l]),
            ga, _bias_rows(rel_bias[l], 1 - t_sample, a_keep + t_sample - 1, lambda t: a_keep - t),
            batch=dec_batch, t=t_sample)
        ob = _attn_b_sample(
            qb, kb, vb, _channel_major(cache_b_k[l]),
            cache_b_v[l].reshape(dec_batch, past_len * B_HEADS, B_V_DIM),
            gb, lams, subg, batch=dec_batch, t=t_sample, lambda_init=lambda_init)
        ys = _out_proj(oa, ob, w_a, w_b, ys, fg, tm=rows_s, final_norm=last)
        outs[4].append(_position_major(ak_new, a_dims))
        outs[5].append(_position_major(av_new, a_dims))
        outs[6].append(kbf.reshape(dec_batch, t_sample, *bk_dims))
        outs[7].append(vbf.reshape(dec_batch, t_sample, *bv_dims))

    return (yp.reshape(batch, seq, d_model), ys.reshape(dec_batch, t_sample, d_model),
            *[jnp.stack(o) for o in outs])
```

```python
import functools
import math

import numpy as np
import jax
import jax.numpy as jnp
from jax import lax
from jax.experimental import pallas as pl
from jax.experimental.pallas import tpu as pltpu

F32 = jnp.float32
BF16 = jnp.bfloat16
NEG_INF = float("-inf")

CHUNK = 64
N_PREV_CHUNKS = 8
A_HEADS = 8
A_HEAD_DIM = 64
A_WIDTH = A_HEADS * A_HEAD_DIM
B_HEADS = 4
B_HEAD_DIM = 64
B_V_DIM = 2 * B_HEAD_DIM
B_WIDTH = B_HEADS * B_V_DIM
SEG = 512
MAX_REL = 128
ROPE_THETA = 500000.0
ROPE_DIM = 16
EPS = 1e-6
LOG2E = math.log2(math.e)
LANES = 128
VMEM_LIMIT = 56 * 1024 * 1024

ROW_TILE = 512
OUT_ROW_TILE = 1024
BAND_ROWS = N_PREV_CHUNKS * CHUNK
A_QB = 256
A_WIN = BAND_ROWS + A_QB
A_ROWS = 64
A_QK_SPLIT = 3
A_SUM_ROWS = 16
B_TQ = 512
B_TK = 512
B_HEADS_PER_STEP = 2
B_STRIP = 256
B_ROWS = 64
B_QK_SPLIT = 2
B_SUM_ROWS = 16

NT_DIMS = (((1,), (1,)), ((), ()))


def _nt_dot(a, b):
    return lax.dot_general(a, b, NT_DIMS, preferred_element_type=F32)


def _dot(a, b):
    return jnp.dot(a, b, preferred_element_type=F32)


def _silu(g):
    return g / (1.0 + jnp.exp(-g))


def _round_up(n, m):
    return -(-n // m) * m


def _rope_lane_freq():
    d = np.arange(LANES) % B_HEAD_DIM
    inv = ROPE_THETA ** (-np.arange(0, ROPE_DIM, 2, dtype=np.float64) / ROPE_DIM)
    return np.where(d < ROPE_DIM, inv[d % (ROPE_DIM // 2)], 0.0)


def _rope_tables(row_pos, base_pos):
    f = _rope_lane_freq()[None, :]
    ar = np.asarray(row_pos, np.float64)[:, None] * f
    ab = np.asarray(base_pos, np.float64)[:, None] * f
    as32 = lambda a: jnp.asarray(a.astype(np.float32))
    return (as32(np.cos(ar)), as32(np.sin(ar)),
            as32(np.cos(ab))[:, None, :], as32(np.sin(ab))[:, None, :])


def _bias_rows(rel_bias, t_min, t_max, rel_of_t, scale=1.0):
    width = _round_up(t_max - t_min + 1, LANES)
    t = np.arange(width)
    t = np.where(t <= t_max, t, t - width)
    idx = np.clip(rel_of_t(t), -MAX_REL, MAX_REL) + MAX_REL
    return rel_bias.astype(F32)[:, idx] * scale


def _toeplitz(row, n_rows):
    return pltpu.roll(jnp.broadcast_to(row, (n_rows, row.shape[-1])), 0, 1, stride=1, stride_axis=0)


def _proj_kernel(x_ref, g_ref, w_ref, cr_ref, sr_ref, cb_ref, sb_ref, *out_refs,
                 tail_every, channel_major, q_scale):
    (qa_ref, ka_ref, va_ref, ga_ref, qb_ref, kb_ref, vb_ref, gb_ref,
     kaf_ref, vaf_ref, kbf_ref, vbf_ref) = out_refs
    x = x_ref[...]
    inv = lax.rsqrt(jnp.mean(x * x, axis=-1, keepdims=True) + EPS)
    h = (x * inv * g_ref[...]).astype(BF16)

    def seg(k):
        return _dot(h, w_ref[:, k * SEG:(k + 1) * SEG])

    cb, sb = cb_ref[0], sb_ref[0]
    cr, sr = cr_ref[...], sr_ref[...]
    cos = cb * cr - sb * sr
    sin = sb * cr + cb * sr
    d = lax.broadcasted_iota(jnp.int32, cos.shape, 1) % B_HEAD_DIM
    sin_lo = jnp.where(d < ROPE_DIM // 2, -sin, 0.0)
    sin_hi = jnp.where(d >= ROPE_DIM // 2, sin, 0.0)

    def rope(z):
        cols = []
        for c in range(SEG // LANES):
            zc = z[:, c * LANES:(c + 1) * LANES]
            up = pltpu.roll(zc, LANES - ROPE_DIM // 2, 1)
            dn = pltpu.roll(zc, ROPE_DIM // 2, 1)
            cols.append(zc * cos + up * sin_lo + dn * sin_hi)
        return jnp.concatenate(cols, axis=1)

    is_tail = (pl.program_id(0) % tail_every) == tail_every - 1

    qa = seg(0) * q_scale
    qa_ref[...] = (qa.T if channel_major else qa).astype(BF16)
    ka = seg(1)
    ka_ref[...] = ka.astype(BF16)
    va = seg(2)
    va_t = va.T if channel_major else va
    va_ref[...] = va_t.astype(BF16)

    @pl.when(is_tail)
    def _():
        kaf_ref[...] = ka.T if channel_major else ka
        vaf_ref[...] = va_t

    ga_ref[...] = seg(3).astype(BF16)
    qb = rope(seg(4)) * q_scale
    qb_ref[...] = (qb.T if channel_major else qb).astype(BF16)
    kb = rope(seg(5))
    kbf_ref[...] = kb.T if channel_major else kb
    kb_ref[...] = kb.astype(BF16)
    vb = seg(6)
    n_col = SEG // LANES
    for c in range(n_col):
        vbf_ref[pl.ds(c, vb.shape[0], stride=n_col), :] = vb[:, c * LANES:(c + 1) * LANES]
    vb_ref[...] = (vb.T if channel_major else vb).astype(BF16)
    gb_ref[...] = seg(7).astype(BF16)


def _proj(x, gain, w_bf16, tables, *, tm, tiles_per_seq, tail_every, channel_major, q_scale):
    rows, d_model = x.shape
    n_tiles = rows // tm
    n_tail = n_tiles // tail_every
    n_seq = n_tiles // tiles_per_seq
    cr, sr, cb, sb = tables
    row_blk = lambda i: (i, 0)
    const2 = lambda i: (0, 0)
    base_blk = lambda i: (i % tiles_per_seq, 0, 0)
    bf = jax.ShapeDtypeStruct((rows, SEG), BF16)
    f32_full = jax.ShapeDtypeStruct((rows, SEG), F32)
    full_spec = pl.BlockSpec((tm, SEG), row_blk)
    if channel_major:
        assert tm == SEG
        tail = jax.ShapeDtypeStruct((n_tail, SEG, tm), F32)
        tail_spec = pl.BlockSpec((None, SEG, tm), lambda i: (i // tail_every, 0, 0))
        kbf = jax.ShapeDtypeStruct((n_seq, SEG, tiles_per_seq * tm), F32)
        kbf_spec = pl.BlockSpec((None, SEG, tm), lambda i: (i // tiles_per_seq, 0, i % tiles_per_seq))
        bf_t = jax.ShapeDtypeStruct((n_tiles, SEG, tm), BF16)
        bf_t_spec = pl.BlockSpec((None, SEG, tm), lambda i: (i, 0, 0))
    else:
        tail = jax.ShapeDtypeStruct((n_tail * tm, SEG), F32)
        tail_spec = pl.BlockSpec((tm, SEG), lambda i: (i // tail_every, 0))
        kbf, kbf_spec = f32_full, full_spec
        bf_t, bf_t_spec = bf, full_spec
    return pl.pallas_call(
        functools.partial(_proj_kernel, tail_every=tail_every, channel_major=channel_major,
                          q_scale=q_scale),
        grid=(n_tiles,),
        in_specs=[
            pl.BlockSpec((tm, d_model), row_blk),
            pl.BlockSpec((1, d_model), const2),
            pl.BlockSpec(w_bf16.shape, const2),
            pl.BlockSpec((tm, LANES), const2),
            pl.BlockSpec((tm, LANES), const2),
            pl.BlockSpec((1, 1, LANES), base_blk),
            pl.BlockSpec((1, 1, LANES), base_blk),
        ],
        out_specs=([bf_t_spec, full_spec, bf_t_spec, full_spec] * 2
                   + [tail_spec, tail_spec, kbf_spec,
                      pl.BlockSpec((tm * SEG // LANES, LANES), row_blk)]),
        out_shape=([bf_t, bf, bf_t, bf] * 2
                   + [tail, tail, kbf, jax.ShapeDtypeStruct((rows * SEG // LANES, LANES), F32)]),
        compiler_params=pltpu.CompilerParams(
            dimension_semantics=("arbitrary",), vmem_limit_bytes=VMEM_LIMIT),
        name="proj",
    )(x, gain, w_bf16, cr, sr, cb, sb)


def _attn_a_prompt_kernel(qt_ref, kp_ref, kc_ref, vtp_ref, vtc_ref, g_ref, brow_ref, o_ref,
                          kcat_ref, vcat_ref, rhs_ref, bias_ref, ot_ref, sa_ref, sb_ref, sc_ref):
    n = pl.program_id(1)

    @pl.when((pl.program_id(0) == 0) & (n == 0))
    def _():
        j = lax.broadcasted_iota(jnp.int32, (A_WIN, A_QB), 0)
        i = lax.broadcasted_iota(jnp.int32, (A_WIN, A_QB), 1)
        gap = i // CHUNK + N_PREV_CHUNKS - j // CHUNK
        band = (gap >= 0) & (gap <= N_PREV_CHUNKS)
        for hd in range(A_HEADS):
            t = _toeplitz(brow_ref[hd:hd + 1, :], A_WIN)[:, :A_QB]
            bias_ref[hd] = jnp.where(band, t, NEG_INF)

    kcat_ref[0:ROW_TILE, :] = kp_ref[...]
    kcat_ref[ROW_TILE:2 * ROW_TILE, :] = kc_ref[...]
    vcat_ref[:, 0:ROW_TILE] = vtp_ref[...]
    vcat_ref[:, ROW_TILE:2 * ROW_TILE] = vtc_ref[...]
    for pair in range(A_HEADS // 2):
        qt = qt_ref[pair * LANES:(pair + 1) * LANES, :]
        row = lax.broadcasted_iota(jnp.int32, qt.shape, 0)
        zero = jnp.zeros_like(qt)
        rhs_ref[2 * pair] = jnp.where(row < A_HEAD_DIM, qt, zero)
        rhs_ref[2 * pair + 1] = jnp.where(row >= A_HEAD_DIM, qt, zero)

    items = [(q0, hd) for q0 in range(0, ROW_TILE, A_QB) for hd in range(A_HEADS)]

    def score_item(t, dst):
        q0, hd = items[t]
        pair = hd // 2
        half = A_WIN // A_QK_SPLIT
        for r in range(0, A_WIN, half):
            dst[r:r + half, :] = _dot(
                kcat_ref[q0 + r:q0 + r + half, pair * LANES:(pair + 1) * LANES],
                rhs_ref[hd, :, q0:q0 + A_QB]) + bias_ref[hd, r:r + half, :]

    def update_item(t, src, first_tile):
        q0, hd = items[t]
        r_min = BAND_ROWS - q0 if first_tile else 0
        rows = range(0, A_WIN, A_ROWS)
        live = [r for r in rows if r >= r_min]
        m = jnp.max(functools.reduce(jnp.maximum, [src[r:r + A_ROWS, :] for r in live]),
                    axis=0, keepdims=True)
        p = jnp.concatenate(
            [jnp.exp2((src[r:r + A_ROWS, :] - m).astype(BF16)) if r >= r_min
             else jnp.zeros((A_ROWS, A_QB), BF16) for r in rows], axis=0)
        vt1 = jnp.concatenate([vcat_ref[hd * A_HEAD_DIM:(hd + 1) * A_HEAD_DIM, q0:q0 + A_WIN],
                               jnp.ones((A_SUM_ROWS, A_WIN), BF16)], axis=0)
        o = _dot(vt1, p)
        ot_ref[hd * A_HEAD_DIM:(hd + 1) * A_HEAD_DIM, q0:q0 + A_QB] = (
            o[:A_HEAD_DIM] / o[A_HEAD_DIM:A_HEAD_DIM + 1])

    def run(first_tile):
        bufs = (sa_ref, sb_ref, sc_ref)
        depth = len(bufs) - 1
        for t in range(depth):
            score_item(t, bufs[t % len(bufs)])
        for t in range(len(items)):
            if t + depth < len(items):
                score_item(t + depth, bufs[(t + depth) % len(bufs)])
            update_item(t, bufs[t % len(bufs)], first_tile)

    @pl.when(n == 0)
    def _():
        run(True)

    @pl.when(n > 0)
    def _():
        run(False)

    o_ref[...] = (ot_ref[...].T * _silu(g_ref[...].astype(F32))).astype(BF16)


def _attn_a_prompt(qat, ka, vat, ga, bias_rows, *, batch, seq):
    tiles = seq // ROW_TILE
    cur = lambda b, n: (b * tiles + n, 0)
    prev = lambda b, n: (b * tiles + jnp.maximum(n - 1, 0), 0)
    cur_t = lambda b, n: (b * tiles + n, 0, 0)
    prev_t = lambda b, n: (b * tiles + jnp.maximum(n - 1, 0), 0, 0)
    blk = (ROW_TILE, A_WIDTH)
    blk_t = (None, A_WIDTH, ROW_TILE)
    return pl.pallas_call(
        _attn_a_prompt_kernel,
        grid=(batch, tiles),
        in_specs=[
            pl.BlockSpec(blk_t, cur_t),
            pl.BlockSpec(blk, prev), pl.BlockSpec(blk, cur),
            pl.BlockSpec(blk_t, prev_t), pl.BlockSpec(blk_t, cur_t),
            pl.BlockSpec(blk, cur),
            pl.BlockSpec(bias_rows.shape, lambda b, n: (0, 0)),
        ],
        out_specs=pl.BlockSpec(blk, cur),
        out_shape=jax.ShapeDtypeStruct(ka.shape, BF16),
        scratch_shapes=[pltpu.VMEM((2 * ROW_TILE, A_WIDTH), BF16),
                        pltpu.VMEM((A_WIDTH, 2 * ROW_TILE), BF16),
                        pltpu.VMEM((A_HEADS, LANES, ROW_TILE), BF16),
                        pltpu.VMEM((A_HEADS, A_WIN, A_QB), F32),
                        pltpu.VMEM((A_WIDTH, ROW_TILE), F32),
                        pltpu.VMEM((A_WIN, A_QB), F32),
                        pltpu.VMEM((A_WIN, A_QB), F32),
                        pltpu.VMEM((A_WIN, A_QB), F32)],
        compiler_params=pltpu.CompilerParams(
            dimension_semantics=("arbitrary", "arbitrary"), vmem_limit_bytes=VMEM_LIMIT),
        name="attn_a_prompt",
    )(qat, ka, ka, vat, vat, ga, bias_rows)


def _diff_lambda(lq1_ref, lk1_ref, lq2_ref, lk2_ref, lambda_init):
    e1 = jnp.exp(jnp.sum(lq1_ref[...] * lk1_ref[...], axis=-1, keepdims=True))
    e2 = jnp.exp(jnp.sum(lq2_ref[...] * lk2_ref[...], axis=-1, keepdims=True))
    return e1 - e2 + lambda_init


def _stack_maps(qh):
    lane = lax.broadcasted_iota(jnp.int32, qh.shape, 1)
    zero = jnp.zeros_like(qh)
    return jnp.concatenate([jnp.where(lane < B_HEAD_DIM, qh, zero),
                            jnp.where(lane >= B_HEAD_DIM, qh, zero)], axis=0)


def _subnorm_gate(o, subg_ref, gate, lambda_init):
    o = o * lax.rsqrt(jnp.mean(o * o, axis=-1, keepdims=True) + EPS) * subg_ref[...]
    return (o * (1.0 - lambda_init)) * _silu(gate)


def _attn_b_prompt_kernel(qt_ref, k_ref, vt_ref, g_ref, lq1_ref, lk1_ref, lq2_ref, lk2_ref,
                          subg_ref, o_ref, rhs_ref, acc_ref, m_ref, sa_ref, sb_ref, *, lambda_init):
    i = pl.program_id(2)
    heads = rhs_ref.shape[0]
    for h in range(heads):
        qt = qt_ref[h * B_V_DIM:(h + 1) * B_V_DIM, :]
        row = lax.broadcasted_iota(jnp.int32, qt.shape, 0)
        zero = jnp.zeros_like(qt)
        rhs_ref[h, :, :B_TQ] = jnp.where(row < B_HEAD_DIM, qt, zero)
        rhs_ref[h, :, B_TQ:] = jnp.where(row >= B_HEAD_DIM, qt, zero)
    m_ref[...] = jnp.full(m_ref.shape, NEG_INF, F32)
    acc_ref[...] = jnp.zeros(acc_ref.shape, F32)

    items = [(h, c) for h in range(heads) for c in range(2 * B_TQ // B_STRIP)]

    def score_item(j, n, dst):
        h, c = items[n]
        part = B_TK // B_QK_SPLIT
        for r in range(0, B_TK, part):
            k0 = pl.multiple_of(j * B_TK + r, part)
            dst[n, r:r + part, :] = _dot(k_ref[pl.ds(k0, part), h * B_V_DIM:(h + 1) * B_V_DIM],
                                         rhs_ref[h, :, c * B_STRIP:(c + 1) * B_STRIP])

    def update_item(j, n, src, masked):
        h, c = items[n]
        hs = slice(h * B_V_DIM, (h + 1) * B_V_DIM)
        cols = slice(c * B_STRIP, (c + 1) * B_STRIP)

        def chunk(r):
            s = src[n, r:r + B_ROWS, :]
            if masked:
                q0 = (c * B_STRIP) % B_TQ
                kc = (r + lax.broadcasted_iota(jnp.int32, s.shape, 0)) // CHUNK
                qc = (q0 + lax.broadcasted_iota(jnp.int32, s.shape, 1)) // CHUNK
                s = jnp.where(kc <= qc, s, NEG_INF)
            return s

        rows = range(0, B_TK, B_ROWS)
        m_blk = functools.reduce(jnp.maximum, [chunk(r) for r in rows])
        m_old = m_ref[h, :, cols]
        m_new = jnp.maximum(m_old, jnp.max(m_blk, axis=0, keepdims=True))
        alpha = jnp.exp2(m_old - m_new)
        m_ref[h, :, cols] = m_new
        p = jnp.concatenate([jnp.exp2((chunk(r) - m_new).astype(BF16)) for r in rows], axis=0)
        vt1 = jnp.concatenate([vt_ref[j, hs, :], jnp.ones((B_SUM_ROWS, B_TK), BF16)], axis=0)
        acc_ref[h, :, cols] = alpha * acc_ref[h, :, cols] + _dot(vt1, p)

    def stage(j_score, dst, j_update, src, masked=False):
        for n in range(len(items)):
            if dst is not None:
                score_item(j_score, n, dst)
            if src is not None:
                update_item(j_update, n, src, masked)

    def pair(jj, carry):
        stage(2 * jj + 1, sb_ref, 2 * jj, sa_ref)
        stage(2 * jj + 2, sa_ref, 2 * jj + 1, sb_ref)
        return carry

    stage(0, sa_ref, None, None)
    lax.fori_loop(0, i // 2, pair, 0)

    @pl.when(i % 2 == 0)
    def _():
        stage(None, None, i, sa_ref, masked=True)

    @pl.when(i % 2 == 1)
    def _():
        stage(i, sb_ref, i - 1, sa_ref)
        stage(None, None, i, sb_ref, masked=True)

    lam = _diff_lambda(lq1_ref, lk1_ref, lq2_ref, lk2_ref, lambda_init)
    for h in range(heads):
        hs = slice(h * B_V_DIM, (h + 1) * B_V_DIM)
        o = acc_ref[h, :B_V_DIM, :] / acc_ref[h, B_V_DIM:B_V_DIM + 1, :]
        o = (o[:, :B_TQ] - lam * o[:, B_TQ:]).T
        o_ref[:, hs] = _subnorm_gate(o, subg_ref, g_ref[:, hs].astype(F32),
                                     lambda_init).astype(BF16)


def _attn_b_prompt(qbt, kb, vbt, gb, lams, subg, *, batch, seq, lambda_init):
    assert B_TQ == B_TK == ROW_TILE and B_HEADS % B_HEADS_PER_STEP == 0
    tiles = seq // B_TQ
    rows = batch * seq
    width = B_HEADS_PER_STEP * B_V_DIM
    n_items = B_HEADS_PER_STEP * 2 * B_TQ // B_STRIP
    qblk = lambda b, h, i: (b * tiles + i, h)
    vec = pl.BlockSpec((1, B_HEAD_DIM), lambda b, h, i: (0, 0))
    vbt = vbt.reshape(batch, tiles, B_WIDTH, B_TK)
    return pl.pallas_call(
        functools.partial(_attn_b_prompt_kernel, lambda_init=lambda_init),
        grid=(batch, B_HEADS // B_HEADS_PER_STEP, tiles),
        in_specs=[
            pl.BlockSpec((None, width, B_TQ), lambda b, h, i: (b * tiles + i, h, 0)),
            pl.BlockSpec((seq, width), lambda b, h, i: (b, h)),
            pl.BlockSpec((None, tiles, width, B_TK), lambda b, h, i: (b, 0, h, 0)),
            pl.BlockSpec((B_TQ, width), qblk),
            vec, vec, vec, vec,
            pl.BlockSpec((1, B_V_DIM), lambda b, h, i: (0, 0)),
        ],
        out_specs=pl.BlockSpec((B_TQ, width), qblk),
        out_shape=jax.ShapeDtypeStruct((rows, B_WIDTH), BF16),
        scratch_shapes=[pltpu.VMEM((B_HEADS_PER_STEP, B_V_DIM, 2 * B_TQ), BF16),
                        pltpu.VMEM((B_HEADS_PER_STEP, B_V_DIM + B_SUM_ROWS, 2 * B_TQ), F32),
                        pltpu.VMEM((B_HEADS_PER_STEP, 1, 2 * B_TQ), F32),
                        pltpu.VMEM((n_items, B_TK, B_STRIP), F32),
                        pltpu.VMEM((n_items, B_TK, B_STRIP), F32)],
        compiler_params=pltpu.CompilerParams(
            dimension_semantics=("arbitrary", "arbitrary", "arbitrary"),
            vmem_limit_bytes=VMEM_LIMIT),
        name="attn_b_prompt",
    )(qbt, kb, vbt, gb, *lams, subg)


def _roll_in(cache_t, new_rows):
    t = new_rows.shape[0]
    keep = cache_t.shape[1]
    shifted = pltpu.roll(cache_t, keep - t, 1)
    pad = jnp.concatenate([jnp.zeros((LANES - t, new_rows.shape[1]), F32), new_rows], axis=0)
    new_t = pad.T
    lane = lax.broadcasted_iota(jnp.int32, new_t.shape, 1)
    last = jnp.where(lane >= LANES - t, new_t, shifted[:, keep - LANES:])
    return jnp.concatenate([shifted[:, :keep - LANES], last], axis=1)


def _attn_a_sample_kernel(q_ref, kn_ref, vn_ref, knf_ref, vnf_ref, kc_ref, vc_ref, g_ref,
                          brow_ref, o_ref, ko_ref, vo_ref):
    t = q_ref.shape[0]
    keep = kc_ref.shape[1]
    lane = lax.broadcasted_iota(jnp.int32, (t, LANES), 1)
    low_half = lane < A_HEAD_DIM
    for pair in range(A_HEADS // 2):
        cs = slice(pair * LANES, (pair + 1) * LANES)
        qp = q_ref[:, cs]
        kc = kc_ref[cs, :].astype(BF16)
        vc = vc_ref[cs, :].astype(BF16)
        kn = kn_ref[:, cs]
        vn = vn_ref[:, cs]
        outs = []
        for e in range(2):
            hd = 2 * pair + e
            bias = _toeplitz(brow_ref[hd:hd + 1, :], t)
            qm = jnp.where(low_half if e == 0 else ~low_half, qp, jnp.zeros_like(qp))
            s_c = _dot(qm, kc) + bias[:, :keep]
            s_n = _nt_dot(qm, kn) + bias[:, keep:keep + t]
            m = jnp.maximum(jnp.max(s_c, axis=-1, keepdims=True),
                            jnp.max(s_n, axis=-1, keepdims=True))
            p_c = jnp.exp(s_c - m)
            p_n = jnp.exp(s_n - m)
            l = jnp.sum(p_c, axis=-1, keepdims=True) + jnp.sum(p_n, axis=-1, keepdims=True)
            o = _nt_dot(p_c.astype(BF16), vc) + _dot(p_n.astype(BF16), vn)
            outs.append(o / l)
        o_pair = jnp.where(low_half, outs[0], outs[1])
        o_ref[:, cs] = (o_pair * _silu(g_ref[:, cs].astype(F32))).astype(BF16)
    ko_ref[...] = _roll_in(kc_ref[...], knf_ref[...])
    vo_ref[...] = _roll_in(vc_ref[...], vnf_ref[...])


def _attn_a_sample(qa, ka, va, kaf, vaf, cache_kt, cache_vt, ga, bias_rows, *, batch, t):
    keep = cache_kt.shape[2]
    new = pl.BlockSpec((t, A_WIDTH), lambda b: (b, 0))
    cache = pl.BlockSpec((None, A_WIDTH, keep), lambda b: (b, 0, 0))
    return pl.pallas_call(
        _attn_a_sample_kernel,
        grid=(batch,),
        in_specs=[new, new, new, new, new, cache, cache, new,
                  pl.BlockSpec(bias_rows.shape, lambda b: (0, 0))],
        out_specs=[new, cache, cache],
        out_shape=[jax.ShapeDtypeStruct(qa.shape, BF16),
                   jax.ShapeDtypeStruct(cache_kt.shape, F32),
                   jax.ShapeDtypeStruct(cache_vt.shape, F32)],
        compiler_params=pltpu.CompilerParams(
            dimension_semantics=("arbitrary",), vmem_limit_bytes=VMEM_LIMIT),
        name="attn_a_sample",
    )(qa, ka, va, kaf, vaf, cache_kt, cache_vt, ga, bias_rows)


def _attn_b_sample_kernel(q_ref, kn_ref, vn_ref, kc_ref, vc_ref, g_ref,
                          lq1_ref, lk1_ref, lq2_ref, lk2_ref, subg_ref, o_ref, *, lambda_init):
    t = q_ref.shape[0]
    past = kc_ref.shape[1]
    lam = _diff_lambda(lq1_ref, lk1_ref, lq2_ref, lk2_ref, lambda_init)
    for h in range(B_HEADS):
        cs = slice(h * B_V_DIM, (h + 1) * B_V_DIM)
        qs = _stack_maps(q_ref[:, cs])
        s_c = _dot(qs, kc_ref[cs, :].astype(BF16))
        s_n = _nt_dot(qs, kn_ref[:, cs])
        m = jnp.maximum(jnp.max(s_c, axis=-1, keepdims=True),
                        jnp.max(s_n, axis=-1, keepdims=True))
        p_c = jnp.exp(s_c - m)
        p_n = jnp.exp(s_n - m)
        l = jnp.sum(p_c, axis=-1, keepdims=True) + jnp.sum(p_n, axis=-1, keepdims=True)
        p_c = p_c / l
        p_n = p_n / l
        a_c = (p_c[:t] - lam * p_c[t:]).astype(BF16)
        a_n = (p_n[:t] - lam * p_n[t:]).astype(BF16)
        vc = vc_ref[pl.ds(h, past, stride=B_HEADS), :].astype(BF16)
        o = _dot(a_c, vc) + _dot(a_n, vn_ref[:, cs])
        o_ref[:, cs] = _subnorm_gate(o, subg_ref, g_ref[:, cs].astype(F32),
                                     lambda_init).astype(BF16)


def _attn_b_sample(qb, kb, vb, cache_kt, cache_v, gb, lams, subg, *, batch, t, lambda_init):
    past = cache_kt.shape[2]
    new = pl.BlockSpec((t, B_WIDTH), lambda b: (b, 0))
    vec = pl.BlockSpec((1, B_HEAD_DIM), lambda b: (0, 0))
    return pl.pallas_call(
        functools.partial(_attn_b_sample_kernel, lambda_init=lambda_init),
        grid=(batch,),
        in_specs=[new, new, new,
                  pl.BlockSpec((None, B_WIDTH, past), lambda b: (b, 0, 0)),
                  pl.BlockSpec((None, past * B_HEADS, B_V_DIM), lambda b: (b, 0, 0)),
                  new, vec, vec, vec, vec,
                  pl.BlockSpec((1, B_V_DIM), lambda b: (0, 0))],
        out_specs=new,
        out_shape=jax.ShapeDtypeStruct(qb.shape, BF16),
        compiler_params=pltpu.CompilerParams(
            dimension_semantics=("arbitrary",), vmem_limit_bytes=VMEM_LIMIT),
        name="attn_b_sample",
    )(qb, kb, vb, cache_kt, cache_v, gb, *lams, subg)


def _out_kernel(oa_ref, ob_ref, wa_ref, wb_ref, x_ref, fg_ref, y_ref, *, final_norm):
    y = x_ref[...] + _dot(oa_ref[...], wa_ref[...]) + _dot(ob_ref[...], wb_ref[...])
    if final_norm:
        y = y * lax.rsqrt(jnp.mean(y * y, axis=-1, keepdims=True) + EPS) * fg_ref[...]
    y_ref[...] = y


def _out_proj(oa, ob, w_a, w_b, x, final_gain, *, tm, final_norm):
    rows, d_model = x.shape
    row_blk = lambda i: (i, 0)
    const2 = lambda i: (0, 0)
    return pl.pallas_call(
        functools.partial(_out_kernel, final_norm=final_norm),
        grid=(rows // tm,),
        in_specs=[
            pl.BlockSpec((tm, A_WIDTH), row_blk),
            pl.BlockSpec((tm, B_WIDTH), row_blk),
            pl.BlockSpec(w_a.shape, const2),
            pl.BlockSpec(w_b.shape, const2),
            pl.BlockSpec((tm, d_model), row_blk),
            pl.BlockSpec((1, d_model), const2),
        ],
        out_specs=pl.BlockSpec((tm, d_model), row_blk),
        out_shape=jax.ShapeDtypeStruct(x.shape, F32),
        compiler_params=pltpu.CompilerParams(
            dimension_semantics=("arbitrary",), vmem_limit_bytes=VMEM_LIMIT),
        name="out_proj",
    )(oa, ob, w_a, w_b, x, final_gain)


def _channel_major(a):
    n, pos = a.shape[:2]
    return jnp.moveaxis(a.reshape(n, pos, -1), 1, 2)


def _position_major(a, channel_dims):
    n, _, pos = a.shape
    return jnp.moveaxis(a, 2, 1).reshape(n, pos, *channel_dims)


def kernel(x_prompt, x_sample, cache_a_k, cache_a_v, cache_b_k, cache_b_v,
           norm_gain, w_in, w_out, rel_bias, lambda_q1, lambda_k1, lambda_q2, lambda_k2,
           subln_gain, final_gain):
    batch, seq, d_model = x_prompt.shape
    dec_batch, t_sample, _ = x_sample.shape
    depth = w_in.shape[0]
    past_len = cache_b_k.shape[2]
    a_keep = cache_a_k.shape[2]
    keep_prompt = min(BAND_ROWS, seq)
    assert seq % ROW_TILE == 0 and keep_prompt == ROW_TILE and seq % B_TQ == 0
    assert t_sample <= a_keep and t_sample <= LANES and d_model == 2 * SEG

    rows_p = batch * seq
    rows_s = dec_batch * t_sample
    tiles_per_seq = seq // ROW_TILE
    tables_p = _rope_tables(np.arange(ROW_TILE), np.arange(tiles_per_seq) * ROW_TILE)
    tables_s = _rope_tables(past_len + np.arange(rows_s) % t_sample, np.zeros(1))
    a_dims = (A_HEADS, A_HEAD_DIM)
    bk_dims = (B_HEADS, 2, B_HEAD_DIM)
    bv_dims = (B_HEADS, B_V_DIM)

    yp = x_prompt.reshape(rows_p, d_model)
    ys = x_sample.reshape(rows_s, d_model)
    fg = final_gain.reshape(1, d_model)
    outs = [[] for _ in range(8)]
    for l in range(depth):
        lambda_init = 0.8 - 0.6 * math.exp(-0.3 * l)
        last = l == depth - 1
        w_l = w_in[l].astype(BF16)
        w_a = w_out[l, :A_WIDTH].astype(BF16)
        w_b = w_out[l, A_WIDTH:].astype(BF16)
        gain = norm_gain[l].reshape(1, d_model)
        lams = [a[l].reshape(1, B_HEAD_DIM).astype(F32)
                for a in (lambda_q1, lambda_k1, lambda_q2, lambda_k2)]
        subg = subln_gain[l].reshape(1, B_V_DIM).astype(F32)

        qa, ka, va, ga, qb, kb, vb, gb, kaf, vaf, kbf, vbf = _proj(
            yp, gain, w_l, tables_p, tm=ROW_TILE, tiles_per_seq=tiles_per_seq,
            tail_every=tiles_per_seq, channel_major=True, q_scale=LOG2E * B_HEAD_DIM ** -0.5)
        bias_p = _bias_rows(rel_bias[l], 1 - A_WIN, A_QB - 1, lambda t: t + BAND_ROWS, LOG2E)
        oa = _attn_a_prompt(qa, ka, va, ga, bias_p, batch=batch, seq=seq)
        ob = _attn_b_prompt(qb, kb, vb, gb, lams, subg, batch=batch, seq=seq,
                            lambda_init=lambda_init)
        yp = _out_proj(oa, ob, w_a, w_b, yp, fg, tm=OUT_ROW_TILE, final_norm=last)
        outs[0].append(_position_major(kaf, a_dims))
        outs[1].append(_position_major(vaf, a_dims))
        outs[2].append(_position_major(kbf, bk_dims))
        outs[3].append(vbf.reshape(batch, seq, *bv_dims))

        qa, ka, va, ga, qb, kb, vb, gb, kaf, vaf, kbf, vbf = _proj(
            ys, gain, w_l, tables_s, tm=rows_s, tiles_per_seq=1, tail_every=1,
            channel_major=False, q_scale=B_HEAD_DIM ** -0.5)
        oa, ak_new, av_new = _attn_a_sample(
            qa, ka, va, kaf, vaf, _channel_major(cache_a_k[l]), _channel_major(cache_a_v[l]),
            ga, _bias_rows(rel_bias[l], 1 - t_sample, a_keep + t_sample - 1, lambda t: a_keep - t),
            batch=dec_batch, t=t_sample)
        ob = _attn_b_sample(
            qb, kb, vb, _channel_major(cache_b_k[l]),
            cache_b_v[l].reshape(dec_batch, past_len * B_HEADS, B_V_DIM),
            gb, lams, subg, batch=dec_batch, t=t_sample, lambda_init=lambda_init)
        ys = _out_proj(oa, ob, w_a, w_b, ys, fg, tm=rows_s, final_norm=last)
        outs[4].append(_position_major(ak_new, a_dims))
        outs[5].append(_position_major(av_new, a_dims))
        outs[6].append(kbf.reshape(dec_batch, t_sample, *bk_dims))
        outs[7].append(vbf.reshape(dec_batch, t_sample, *bv_dims))

    return (yp.reshape(batch, seq, d_model), ys.reshape(dec_batch, t_sample, d_model),
            *[jnp.stack(o) for o in outs])
```

```python
import functools
import math

import numpy as np
import jax
import jax.numpy as jnp
from jax import lax
from jax.experimental import pallas as pl
from jax.experimental.pallas import tpu as pltpu

F32 = jnp.float32
BF16 = jnp.bfloat16
NEG_INF = float("-inf")

CHUNK = 64
N_PREV_CHUNKS = 8
A_HEADS = 8
A_HEAD_DIM = 64
A_WIDTH = A_HEADS * A_HEAD_DIM
B_HEADS = 4
B_HEAD_DIM = 64
B_V_DIM = 2 * B_HEAD_DIM
B_WIDTH = B_HEADS * B_V_DIM
SEG = 512
MAX_REL = 128
ROPE_THETA = 500000.0
ROPE_DIM = 16
EPS = 1e-6
LOG2E = math.log2(math.e)
LANES = 128
VMEM_LIMIT = 56 * 1024 * 1024

ROW_TILE = 512
OUT_ROW_TILE = 1024
BAND_ROWS = N_PREV_CHUNKS * CHUNK
A_QB = 256
A_WIN = BAND_ROWS + A_QB
A_ROWS = 64
A_QK_SPLIT = 3
A_SUM_ROWS = 16
B_TQ = 512
B_TK = 512
B_HEADS_PER_STEP = 4
B_STRIP = 256
B_ROWS = 64
B_QK_SPLIT = 2
B_SUM_ROWS = 16

NT_DIMS = (((1,), (1,)), ((), ()))


def _nt_dot(a, b):
    return lax.dot_general(a, b, NT_DIMS, preferred_element_type=F32)


def _dot(a, b):
    return jnp.dot(a, b, preferred_element_type=F32)


def _silu(g):
    return g / (1.0 + jnp.exp(-g))


def _round_up(n, m):
    return -(-n // m) * m


def _rope_lane_freq():
    d = np.arange(LANES) % B_HEAD_DIM
    inv = ROPE_THETA ** (-np.arange(0, ROPE_DIM, 2, dtype=np.float64) / ROPE_DIM)
    return np.where(d < ROPE_DIM, inv[d % (ROPE_DIM // 2)], 0.0)


def _rope_tables(row_pos, base_pos):
    f = _rope_lane_freq()[None, :]
    ar = np.asarray(row_pos, np.float64)[:, None] * f
    ab = np.asarray(base_pos, np.float64)[:, None] * f
    as32 = lambda a: jnp.asarray(a.astype(np.float32))
    return (as32(np.cos(ar)), as32(np.sin(ar)),
            as32(np.cos(ab))[:, None, :], as32(np.sin(ab))[:, None, :])


def _bias_rows(rel_bias, t_min, t_max, rel_of_t, scale=1.0):
    width = _round_up(t_max - t_min + 1, LANES)
    t = np.arange(width)
    t = np.where(t <= t_max, t, t - width)
    idx = np.clip(rel_of_t(t), -MAX_REL, MAX_REL) + MAX_REL
    return rel_bias.astype(F32)[:, idx] * scale


def _toeplitz(row, n_rows):
    return pltpu.roll(jnp.broadcast_to(row, (n_rows, row.shape[-1])), 0, 1, stride=1, stride_axis=0)


def _proj_kernel(x_ref, g_ref, w_ref, cr_ref, sr_ref, cb_ref, sb_ref, *out_refs,
                 tail_every, channel_major, q_scale):
    (qa_ref, ka_ref, va_ref, ga_ref, qb_ref, kb_ref, vb_ref, gb_ref,
     kaf_ref, vaf_ref, kbf_ref, vbf_ref) = out_refs
    x = x_ref[...]
    inv = lax.rsqrt(jnp.mean(x * x, axis=-1, keepdims=True) + EPS)
    h = (x * inv * g_ref[...]).astype(BF16)

    def seg(k):
        return _dot(h, w_ref[:, k * SEG:(k + 1) * SEG])

    cb, sb = cb_ref[0], sb_ref[0]
    cr, sr = cr_ref[...], sr_ref[...]
    cos = cb * cr - sb * sr
    sin = sb * cr + cb * sr
    d = lax.broadcasted_iota(jnp.int32, cos.shape, 1) % B_HEAD_DIM
    sin_lo = jnp.where(d < ROPE_DIM // 2, -sin, 0.0)
    sin_hi = jnp.where(d >= ROPE_DIM // 2, sin, 0.0)

    def rope(z):
        cols = []
        for c in range(SEG // LANES):
            zc = z[:, c * LANES:(c + 1) * LANES]
            up = pltpu.roll(zc, LANES - ROPE_DIM // 2, 1)
            dn = pltpu.roll(zc, ROPE_DIM // 2, 1)
            cols.append(zc * cos + up * sin_lo + dn * sin_hi)
        return jnp.concatenate(cols, axis=1)

    is_tail = (pl.program_id(0) % tail_every) == tail_every - 1

    qa = seg(0) * q_scale
    qa_ref[...] = (qa.T if channel_major else qa).astype(BF16)
    ka = seg(1)
    ka_ref[...] = ka.astype(BF16)
    va = seg(2)
    va_t = va.T if channel_major else va
    va_ref[...] = va_t.astype(BF16)

    @pl.when(is_tail)
    def _():
        kaf_ref[...] = ka.T if channel_major else ka
        vaf_ref[...] = va_t

    ga_ref[...] = seg(3).astype(BF16)
    qb = rope(seg(4)) * q_scale
    qb_ref[...] = (qb.T if channel_major else qb).astype(BF16)
    kb = rope(seg(5))
    kbf_ref[...] = kb.T if channel_major else kb
    kb_ref[...] = kb.astype(BF16)
    vb = seg(6)
    n_col = SEG // LANES
    for c in range(n_col):
        vbf_ref[pl.ds(c, vb.shape[0], stride=n_col), :] = vb[:, c * LANES:(c + 1) * LANES]
    vb_ref[...] = (vb.T if channel_major else vb).astype(BF16)
    gb_ref[...] = seg(7).astype(BF16)


def _proj(x, gain, w_bf16, tables, *, tm, tiles_per_seq, tail_every, channel_major, q_scale):
    rows, d_model = x.shape
    n_tiles = rows // tm
    n_tail = n_tiles // tail_every
    n_seq = n_tiles // tiles_per_seq
    cr, sr, cb, sb = tables
    row_blk = lambda i: (i, 0)
    const2 = lambda i: (0, 0)
    base_blk = lambda i: (i % tiles_per_seq, 0, 0)
    bf = jax.ShapeDtypeStruct((rows, SEG), BF16)
    f32_full = jax.ShapeDtypeStruct((rows, SEG), F32)
    full_spec = pl.BlockSpec((tm, SEG), row_blk)
    if channel_major:
        assert tm == SEG
        tail = jax.ShapeDtypeStruct((n_tail, SEG, tm), F32)
        tail_spec = pl.BlockSpec((None, SEG, tm), lambda i: (i // tail_every, 0, 0))
        kbf = jax.ShapeDtypeStruct((n_seq, SEG, tiles_per_seq * tm), F32)
        kbf_spec = pl.BlockSpec((None, SEG, tm), lambda i: (i // tiles_per_seq, 0, i % tiles_per_seq))
        bf_t = jax.ShapeDtypeStruct((n_tiles, SEG, tm), BF16)
        bf_t_spec = pl.BlockSpec((None, SEG, tm), lambda i: (i, 0, 0))
    else:
        tail = jax.ShapeDtypeStruct((n_tail * tm, SEG), F32)
        tail_spec = pl.BlockSpec((tm, SEG), lambda i: (i // tail_every, 0))
        kbf, kbf_spec = f32_full, full_spec
        bf_t, bf_t_spec = bf, full_spec
    return pl.pallas_call(
        functools.partial(_proj_kernel, tail_every=tail_every, channel_major=channel_major,
                          q_scale=q_scale),
        grid=(n_tiles,),
        in_specs=[
            pl.BlockSpec((tm, d_model), row_blk),
            pl.BlockSpec((1, d_model), const2),
            pl.BlockSpec(w_bf16.shape, const2),
            pl.BlockSpec((tm, LANES), const2),
            pl.BlockSpec((tm, LANES), const2),
            pl.BlockSpec((1, 1, LANES), base_blk),
            pl.BlockSpec((1, 1, LANES), base_blk),
        ],
        out_specs=([bf_t_spec, full_spec, bf_t_spec, full_spec] * 2
                   + [tail_spec, tail_spec, kbf_spec,
                      pl.BlockSpec((tm * SEG // LANES, LANES), row_blk)]),
        out_shape=([bf_t, bf, bf_t, bf] * 2
                   + [tail, tail, kbf, jax.ShapeDtypeStruct((rows * SEG // LANES, LANES), F32)]),
        compiler_params=pltpu.CompilerParams(
            dimension_semantics=("arbitrary",), vmem_limit_bytes=VMEM_LIMIT),
        name="proj",
    )(x, gain, w_bf16, cr, sr, cb, sb)


def _attn_a_prompt_kernel(qt_ref, kp_ref, kc_ref, vtp_ref, vtc_ref, g_ref, brow_ref, o_ref,
                          kcat_ref, vcat_ref, rhs_ref, bias_ref, ot_ref, sa_ref, sb_ref, sc_ref):
    n = pl.program_id(1)

    @pl.when((pl.program_id(0) == 0) & (n == 0))
    def _():
        j = lax.broadcasted_iota(jnp.int32, (A_WIN, A_QB), 0)
        i = lax.broadcasted_iota(jnp.int32, (A_WIN, A_QB), 1)
        gap = i // CHUNK + N_PREV_CHUNKS - j // CHUNK
        band = (gap >= 0) & (gap <= N_PREV_CHUNKS)
        for hd in range(A_HEADS):
            t = _toeplitz(brow_ref[hd:hd + 1, :], A_WIN)[:, :A_QB]
            bias_ref[hd] = jnp.where(band, t, NEG_INF)

    kcat_ref[0:ROW_TILE, :] = kp_ref[...]
    kcat_ref[ROW_TILE:2 * ROW_TILE, :] = kc_ref[...]
    vcat_ref[:, 0:ROW_TILE] = vtp_ref[...]
    vcat_ref[:, ROW_TILE:2 * ROW_TILE] = vtc_ref[...]
    for pair in range(A_HEADS // 2):
        qt = qt_ref[pair * LANES:(pair + 1) * LANES, :]
        row = lax.broadcasted_iota(jnp.int32, qt.shape, 0)
        zero = jnp.zeros_like(qt)
        rhs_ref[2 * pair] = jnp.where(row < A_HEAD_DIM, qt, zero)
        rhs_ref[2 * pair + 1] = jnp.where(row >= A_HEAD_DIM, qt, zero)

    items = [(q0, hd) for q0 in range(0, ROW_TILE, A_QB) for hd in range(A_HEADS)]

    def score_item(t, dst):
        q0, hd = items[t]
        pair = hd // 2
        half = A_WIN // A_QK_SPLIT
        for r in range(0, A_WIN, half):
            dst[r:r + half, :] = _dot(
                kcat_ref[q0 + r:q0 + r + half, pair * LANES:(pair + 1) * LANES],
                rhs_ref[hd, :, q0:q0 + A_QB]) + bias_ref[hd, r:r + half, :]

    def update_item(t, src, first_tile):
        q0, hd = items[t]
        r_min = BAND_ROWS - q0 if first_tile else 0
        rows = range(0, A_WIN, A_ROWS)
        live = [r for r in rows if r >= r_min]
        m = jnp.max(functools.reduce(jnp.maximum, [src[r:r + A_ROWS, :] for r in live]),
                    axis=0, keepdims=True)
        p = jnp.concatenate(
            [jnp.exp2((src[r:r + A_ROWS, :] - m).astype(BF16)) if r >= r_min
             else jnp.zeros((A_ROWS, A_QB), BF16) for r in rows], axis=0)
        vt1 = jnp.concatenate([vcat_ref[hd * A_HEAD_DIM:(hd + 1) * A_HEAD_DIM, q0:q0 + A_WIN],
                               jnp.ones((A_SUM_ROWS, A_WIN), BF16)], axis=0)
        o = _dot(vt1, p)
        ot_ref[hd * A_HEAD_DIM:(hd + 1) * A_HEAD_DIM, q0:q0 + A_QB] = (
            o[:A_HEAD_DIM] / o[A_HEAD_DIM:A_HEAD_DIM + 1])

    def run(first_tile):
        bufs = (sa_ref, sb_ref, sc_ref)
        depth = len(bufs) - 1
        for t in range(depth):
            score_item(t, bufs[t % len(bufs)])
        for t in range(len(items)):
            if t + depth < len(items):
                score_item(t + depth, bufs[(t + depth) % len(bufs)])
            update_item(t, bufs[t % len(bufs)], first_tile)

    @pl.when(n == 0)
    def _():
        run(True)

    @pl.when(n > 0)
    def _():
        run(False)

    o_ref[...] = (ot_ref[...].T * _silu(g_ref[...].astype(F32))).astype(BF16)


def _attn_a_prompt(qat, ka, vat, ga, bias_rows, *, batch, seq):
    tiles = seq // ROW_TILE
    cur = lambda b, n: (b * tiles + n, 0)
    prev = lambda b, n: (b * tiles + jnp.maximum(n - 1, 0), 0)
    cur_t = lambda b, n: (b * tiles + n, 0, 0)
    prev_t = lambda b, n: (b * tiles + jnp.maximum(n - 1, 0), 0, 0)
    blk = (ROW_TILE, A_WIDTH)
    blk_t = (None, A_WIDTH, ROW_TILE)
    return pl.pallas_call(
        _attn_a_prompt_kernel,
        grid=(batch, tiles),
        in_specs=[
            pl.BlockSpec(blk_t, cur_t),
            pl.BlockSpec(blk, prev), pl.BlockSpec(blk, cur),
            pl.BlockSpec(blk_t, prev_t), pl.BlockSpec(blk_t, cur_t),
            pl.BlockSpec(blk, cur),
            pl.BlockSpec(bias_rows.shape, lambda b, n: (0, 0)),
        ],
        out_specs=pl.BlockSpec(blk, cur),
        out_shape=jax.ShapeDtypeStruct(ka.shape, BF16),
        scratch_shapes=[pltpu.VMEM((2 * ROW_TILE, A_WIDTH), BF16),
                        pltpu.VMEM((A_WIDTH, 2 * ROW_TILE), BF16),
                        pltpu.VMEM((A_HEADS, LANES, ROW_TILE), BF16),
                        pltpu.VMEM((A_HEADS, A_WIN, A_QB), F32),
                        pltpu.VMEM((A_WIDTH, ROW_TILE), F32),
                        pltpu.VMEM((A_WIN, A_QB), F32),
                        pltpu.VMEM((A_WIN, A_QB), F32),
                        pltpu.VMEM((A_WIN, A_QB), F32)],
        compiler_params=pltpu.CompilerParams(
            dimension_semantics=("arbitrary", "arbitrary"), vmem_limit_bytes=VMEM_LIMIT),
        name="attn_a_prompt",
    )(qat, ka, ka, vat, vat, ga, bias_rows)


def _diff_lambda(lq1_ref, lk1_ref, lq2_ref, lk2_ref, lambda_init):
    e1 = jnp.exp(jnp.sum(lq1_ref[...] * lk1_ref[...], axis=-1, keepdims=True))
    e2 = jnp.exp(jnp.sum(lq2_ref[...] * lk2_ref[...], axis=-1, keepdims=True))
    return e1 - e2 + lambda_init


def _stack_maps(qh):
    lane = lax.broadcasted_iota(jnp.int32, qh.shape, 1)
    zero = jnp.zeros_like(qh)
    return jnp.concatenate([jnp.where(lane < B_HEAD_DIM, qh, zero),
                            jnp.where(lane >= B_HEAD_DIM, qh, zero)], axis=0)


def _subnorm_gate(o, subg_ref, gate, lambda_init):
    o = o * lax.rsqrt(jnp.mean(o * o, axis=-1, keepdims=True) + EPS) * subg_ref[...]
    return (o * (1.0 - lambda_init)) * _silu(gate)


def _attn_b_prompt_kernel(qta_ref, qtb_ref, k_ref, vt_ref, ga_ref, gb_ref,
                          lq1_ref, lk1_ref, lq2_ref, lk2_ref, subg_ref, oa_ref, ob_ref,
                          rhs_ref, acc_ref, m_ref, sa_ref, sb_ref, *, lambda_init, tiles):
    g = pl.program_id(2)
    heads = rhs_ref.shape[1]
    for t, qt_ref in enumerate((qta_ref, qtb_ref)):
        for h in range(heads):
            qt = qt_ref[h * B_V_DIM:(h + 1) * B_V_DIM, :]
            row = lax.broadcasted_iota(jnp.int32, qt.shape, 0)
            zero = jnp.zeros_like(qt)
            rhs_ref[t, h, :, :B_TQ] = jnp.where(row < B_HEAD_DIM, qt, zero)
            rhs_ref[t, h, :, B_TQ:] = jnp.where(row >= B_HEAD_DIM, qt, zero)
    m_ref[...] = jnp.full(m_ref.shape, NEG_INF, F32)
    acc_ref[...] = jnp.zeros(acc_ref.shape, F32)

    items = [(h, c) for h in range(heads) for c in range(2 * B_TQ // B_STRIP)]

    def score_item(tile, j, n, dst):
        h, c = items[n]
        part = B_TK // B_QK_SPLIT
        for r in range(0, B_TK, part):
            k0 = pl.multiple_of(j * B_TK + r, part)
            dst[n, r:r + part, :] = _dot(k_ref[pl.ds(k0, part), h * B_V_DIM:(h + 1) * B_V_DIM],
                                         rhs_ref[tile, h, :, c * B_STRIP:(c + 1) * B_STRIP])

    def update_item(tile, j, n, src, masked):
        h, c = items[n]
        hs = slice(h * B_V_DIM, (h + 1) * B_V_DIM)
        cols = slice(c * B_STRIP, (c + 1) * B_STRIP)

        def chunk(r):
            s = src[n, r:r + B_ROWS, :]
            if masked:
                q0 = (c * B_STRIP) % B_TQ
                kc = (r + lax.broadcasted_iota(jnp.int32, s.shape, 0)) // CHUNK
                qc = (q0 + lax.broadcasted_iota(jnp.int32, s.shape, 1)) // CHUNK
                s = jnp.where(kc <= qc, s, NEG_INF)
            return s

        rows = range(0, B_TK, B_ROWS)
        m_blk = functools.reduce(jnp.maximum, [chunk(r) for r in rows])
        m_old = m_ref[tile, h, :, cols]
        m_new = jnp.maximum(m_old, jnp.max(m_blk, axis=0, keepdims=True))
        alpha = jnp.exp2(m_old - m_new)
        m_ref[tile, h, :, cols] = m_new
        p = jnp.concatenate([jnp.exp2((chunk(r) - m_new).astype(BF16)) for r in rows], axis=0)
        vt1 = jnp.concatenate([vt_ref[j, hs, :], jnp.ones((B_SUM_ROWS, B_TK), BF16)], axis=0)
        acc_ref[tile, h, :, cols] = alpha * acc_ref[tile, h, :, cols] + _dot(vt1, p)

    def stage(score, dst, update, src, masked=False):
        for n in range(len(items)):
            if score is not None:
                score_item(*score, n, dst)
            if update is not None:
                update_item(*update, n, src, masked)

    first = (0, g)
    last = (1, tiles - 1 - g)

    def at(s):
        in_lo = s <= g
        return jnp.where(in_lo, 0, 1), jnp.where(in_lo, s - 1, s - 1 - g)

    def pair(p, carry):
        s = 2 * p + 1
        stage(at(s + 1), sa_ref, at(s), sb_ref)
        stage(at(s + 2), sb_ref, at(s + 1), sa_ref)
        return carry

    stage(first, sa_ref, None, None)
    stage(at(1), sb_ref, first, sa_ref, masked=True)
    lax.fori_loop(0, (tiles - 2) // 2, pair, 0)
    stage(last, sa_ref, at(tiles - 1), sb_ref)
    stage(None, None, last, sa_ref, masked=True)

    lam = _diff_lambda(lq1_ref, lk1_ref, lq2_ref, lk2_ref, lambda_init)
    for t, (g_ref, o_ref) in enumerate(((ga_ref, oa_ref), (gb_ref, ob_ref))):
        for h in range(heads):
            hs = slice(h * B_V_DIM, (h + 1) * B_V_DIM)
            o = acc_ref[t, h, :B_V_DIM, :] / acc_ref[t, h, B_V_DIM:B_V_DIM + 1, :]
            o = (o[:, :B_TQ] - lam * o[:, B_TQ:]).T
            o_ref[:, hs] = _subnorm_gate(o, subg_ref, g_ref[:, hs].astype(F32),
                                         lambda_init).astype(BF16)


def _attn_b_prompt(qbt, kb, vbt, gb, lams, subg, *, batch, seq, lambda_init):
    assert B_TQ == B_TK == ROW_TILE and B_HEADS % B_HEADS_PER_STEP == 0
    tiles = seq // B_TQ
    assert tiles % 2 == 0
    half = tiles // 2
    width = B_HEADS_PER_STEP * B_V_DIM
    n_items = B_HEADS_PER_STEP * 2 * B_TQ // B_STRIP
    lo = lambda b, g: b * tiles + g
    hi = lambda b, g: b * tiles + tiles - 1 - g
    vec = pl.BlockSpec((1, B_HEAD_DIM), lambda b, h, g: (0, 0))
    vbt = vbt.reshape(batch, tiles, B_WIDTH, B_TK)
    out = jax.ShapeDtypeStruct((batch * half * B_TQ, B_WIDTH), BF16)
    return pl.pallas_call(
        functools.partial(_attn_b_prompt_kernel, lambda_init=lambda_init, tiles=tiles),
        grid=(batch, B_HEADS // B_HEADS_PER_STEP, half),
        in_specs=[
            pl.BlockSpec((None, width, B_TQ), lambda b, h, g: (lo(b, g), h, 0)),
            pl.BlockSpec((None, width, B_TQ), lambda b, h, g: (hi(b, g), h, 0)),
            pl.BlockSpec((seq, width), lambda b, h, g: (b, h), pipeline_mode=pl.Buffered(1)),
            pl.BlockSpec((None, tiles, width, B_TK), lambda b, h, g: (b, 0, h, 0),
                         pipeline_mode=pl.Buffered(1)),
            pl.BlockSpec((B_TQ, width), lambda b, h, g: (lo(b, g), h)),
            pl.BlockSpec((B_TQ, width), lambda b, h, g: (hi(b, g), h)),
            vec, vec, vec, vec,
            pl.BlockSpec((1, B_V_DIM), lambda b, h, g: (0, 0)),
        ],
        out_specs=[pl.BlockSpec((B_TQ, width), lambda b, h, g: (b * half + g, h)),
                   pl.BlockSpec((B_TQ, width), lambda b, h, g: (b * half + half - 1 - g, h))],
        out_shape=[out, out],
        scratch_shapes=[pltpu.VMEM((2, B_HEADS_PER_STEP, B_V_DIM, 2 * B_TQ), BF16),
                        pltpu.VMEM((2, B_HEADS_PER_STEP, B_V_DIM + B_SUM_ROWS, 2 * B_TQ), F32),
                        pltpu.VMEM((2, B_HEADS_PER_STEP, 1, 2 * B_TQ), F32),
                        pltpu.VMEM((n_items, B_TK, B_STRIP), F32),
                        pltpu.VMEM((n_items, B_TK, B_STRIP), F32)],
        compiler_params=pltpu.CompilerParams(
            dimension_semantics=("arbitrary", "arbitrary", "arbitrary"),
            vmem_limit_bytes=VMEM_LIMIT),
        name="attn_b_prompt",
    )(qbt, qbt, kb, vbt, gb, gb, *lams, subg)


def _roll_in(cache_t, new_rows):
    t = new_rows.shape[0]
    keep = cache_t.shape[1]
    shifted = pltpu.roll(cache_t, keep - t, 1)
    pad = jnp.concatenate([jnp.zeros((LANES - t, new_rows.shape[1]), F32), new_rows], axis=0)
    new_t = pad.T
    lane = lax.broadcasted_iota(jnp.int32, new_t.shape, 1)
    last = jnp.where(lane >= LANES - t, new_t, shifted[:, keep - LANES:])
    return jnp.concatenate([shifted[:, :keep - LANES], last], axis=1)


def _attn_a_sample_kernel(q_ref, kn_ref, vn_ref, knf_ref, vnf_ref, kc_ref, vc_ref, g_ref,
                          brow_ref, o_ref, ko_ref, vo_ref):
    t = q_ref.shape[0]
    keep = kc_ref.shape[1]
    lane = lax.broadcasted_iota(jnp.int32, (t, LANES), 1)
    low_half = lane < A_HEAD_DIM
    for pair in range(A_HEADS // 2):
        cs = slice(pair * LANES, (pair + 1) * LANES)
        qp = q_ref[:, cs]
        kc = kc_ref[cs, :].astype(BF16)
        vc = vc_ref[cs, :].astype(BF16)
        kn = kn_ref[:, cs]
        vn = vn_ref[:, cs]
        outs = []
        for e in range(2):
            hd = 2 * pair + e
            bias = _toeplitz(brow_ref[hd:hd + 1, :], t)
            qm = jnp.where(low_half if e == 0 else ~low_half, qp, jnp.zeros_like(qp))
            s_c = _dot(qm, kc) + bias[:, :keep]
            s_n = _nt_dot(qm, kn) + bias[:, keep:keep + t]
            m = jnp.maximum(jnp.max(s_c, axis=-1, keepdims=True),
                            jnp.max(s_n, axis=-1, keepdims=True))
            p_c = jnp.exp(s_c - m)
            p_n = jnp.exp(s_n - m)
            l = jnp.sum(p_c, axis=-1, keepdims=True) + jnp.sum(p_n, axis=-1, keepdims=True)
            o = _nt_dot(p_c.astype(BF16), vc) + _dot(p_n.astype(BF16), vn)
            outs.append(o / l)
        o_pair = jnp.where(low_half, outs[0], outs[1])
        o_ref[:, cs] = (o_pair * _silu(g_ref[:, cs].astype(F32))).astype(BF16)
    ko_ref[...] = _roll_in(kc_ref[...], knf_ref[...])
    vo_ref[...] = _roll_in(vc_ref[...], vnf_ref[...])


def _attn_a_sample(qa, ka, va, kaf, vaf, cache_kt, cache_vt, ga, bias_rows, *, batch, t):
    keep = cache_kt.shape[2]
    new = pl.BlockSpec((t, A_WIDTH), lambda b: (b, 0))
    cache = pl.BlockSpec((None, A_WIDTH, keep), lambda b: (b, 0, 0))
    return pl.pallas_call(
        _attn_a_sample_kernel,
        grid=(batch,),
        in_specs=[new, new, new, new, new, cache, cache, new,
                  pl.BlockSpec(bias_rows.shape, lambda b: (0, 0))],
        out_specs=[new, cache, cache],
        out_shape=[jax.ShapeDtypeStruct(qa.shape, BF16),
                   jax.ShapeDtypeStruct(cache_kt.shape, F32),
                   jax.ShapeDtypeStruct(cache_vt.shape, F32)],
        compiler_params=pltpu.CompilerParams(
            dimension_semantics=("arbitrary",), vmem_limit_bytes=VMEM_LIMIT),
        name="attn_a_sample",
    )(qa, ka, va, kaf, vaf, cache_kt, cache_vt, ga, bias_rows)


def _attn_b_sample_kernel(q_ref, kn_ref, vn_ref, kc_ref, vc_ref, g_ref,
                          lq1_ref, lk1_ref, lq2_ref, lk2_ref, subg_ref, o_ref, *, lambda_init):
    t = q_ref.shape[0]
    past = kc_ref.shape[1]
    lam = _diff_lambda(lq1_ref, lk1_ref, lq2_ref, lk2_ref, lambda_init)
    for h in range(B_HEADS):
        cs = slice(h * B_V_DIM, (h + 1) * B_V_DIM)
        qs = _stack_maps(q_ref[:, cs])
        s_c = _dot(qs, kc_ref[cs, :].astype(BF16))
        s_n = _nt_dot(qs, kn_ref[:, cs])
        m = jnp.maximum(jnp.max(s_c, axis=-1, keepdims=True),
                        jnp.max(s_n, axis=-1, keepdims=True))
        p_c = jnp.exp(s_c - m)
        p_n = jnp.exp(s_n - m)
        l = jnp.sum(p_c, axis=-1, keepdims=True) + jnp.sum(p_n, axis=-1, keepdims=True)
        p_c = p_c / l
        p_n = p_n / l
        a_c = (p_c[:t] - lam * p_c[t:]).astype(BF16)
        a_n = (p_n[:t] - lam * p_n[t:]).astype(BF16)
        vc = vc_ref[pl.ds(h, past, stride=B_HEADS), :].astype(BF16)
        o = _dot(a_c, vc) + _dot(a_n, vn_ref[:, cs])
        o_ref[:, cs] = _subnorm_gate(o, subg_ref, g_ref[:, cs].astype(F32),
                                     lambda_init).astype(BF16)


def _attn_b_sample(qb, kb, vb, cache_kt, cache_v, gb, lams, subg, *, batch, t, lambda_init):
    past = cache_kt.shape[2]
    new = pl.BlockSpec((t, B_WIDTH), lambda b: (b, 0))
    vec = pl.BlockSpec((1, B_HEAD_DIM), lambda b: (0, 0))
    return pl.pallas_call(
        functools.partial(_attn_b_sample_kernel, lambda_init=lambda_init),
        grid=(batch,),
        in_specs=[new, new, new,
                  pl.BlockSpec((None, B_WIDTH, past), lambda b: (b, 0, 0)),
                  pl.BlockSpec((None, past * B_HEADS, B_V_DIM), lambda b: (b, 0, 0)),
                  new, vec, vec, vec, vec,
                  pl.BlockSpec((1, B_V_DIM), lambda b: (0, 0))],
        out_specs=new,
        out_shape=jax.ShapeDtypeStruct(qb.shape, BF16),
        compiler_params=pltpu.CompilerParams(
            dimension_semantics=("arbitrary",), vmem_limit_bytes=VMEM_LIMIT),
        name="attn_b_sample",
    )(qb, kb, vb, cache_kt, cache_v, gb, *lams, subg)


def _out_kernel(oa_ref, *refs, final_norm, steps_per_seq):
    *ob_refs, wa_ref, wb_ref, x_ref, fg_ref, y_ref = refs
    if len(ob_refs) == 2:
        in_lo = pl.program_id(0) % steps_per_seq < steps_per_seq // 2
        ob = jnp.where(in_lo, ob_refs[0][...], ob_refs[1][...])
    else:
        ob = ob_refs[0][...]
    y = x_ref[...] + _dot(oa_ref[...], wa_ref[...]) + _dot(ob, wb_ref[...])
    if final_norm:
        y = y * lax.rsqrt(jnp.mean(y * y, axis=-1, keepdims=True) + EPS) * fg_ref[...]
    y_ref[...] = y


def _out_proj(oa, ob_parts, w_a, w_b, x, final_gain, *, tm, final_norm, steps_per_seq=1):
    rows, d_model = x.shape
    row_blk = lambda i: (i, 0)
    const2 = lambda i: (0, 0)
    if len(ob_parts) == 2:
        assert steps_per_seq % 2 == 0
        hs = steps_per_seq // 2
        seq_of = lambda i: i // steps_per_seq
        step_of = lambda i: i % steps_per_seq
        ob_specs = [
            pl.BlockSpec((tm, B_WIDTH),
                         lambda i: (seq_of(i) * hs + jnp.minimum(step_of(i), hs - 1), 0)),
            pl.BlockSpec((tm, B_WIDTH),
                         lambda i: (seq_of(i) * hs + jnp.maximum(step_of(i) - hs, 0), 0)),
        ]
    else:
        ob_specs = [pl.BlockSpec((tm, B_WIDTH), row_blk)]
    return pl.pallas_call(
        functools.partial(_out_kernel, final_norm=final_norm, steps_per_seq=steps_per_seq),
        grid=(rows // tm,),
        in_specs=[
            pl.BlockSpec((tm, A_WIDTH), row_blk),
            *ob_specs,
            pl.BlockSpec(w_a.shape, const2),
            pl.BlockSpec(w_b.shape, const2),
            pl.BlockSpec((tm, d_model), row_blk),
            pl.BlockSpec((1, d_model), const2),
        ],
        out_specs=pl.BlockSpec((tm, d_model), row_blk),
        out_shape=jax.ShapeDtypeStruct(x.shape, F32),
        compiler_params=pltpu.CompilerParams(
            dimension_semantics=("arbitrary",), vmem_limit_bytes=VMEM_LIMIT),
        name="out_proj",
    )(oa, *ob_parts, w_a, w_b, x, final_gain)


def _channel_major(a):
    n, pos = a.shape[:2]
    return jnp.moveaxis(a.reshape(n, pos, -1), 1, 2)


def _position_major(a, channel_dims):
    n, _, pos = a.shape
    return jnp.moveaxis(a, 2, 1).reshape(n, pos, *channel_dims)


def kernel(x_prompt, x_sample, cache_a_k, cache_a_v, cache_b_k, cache_b_v,
           norm_gain, w_in, w_out, rel_bias, lambda_q1, lambda_k1, lambda_q2, lambda_k2,
           subln_gain, final_gain):
    batch, seq, d_model = x_prompt.shape
    dec_batch, t_sample, _ = x_sample.shape
    depth = w_in.shape[0]
    past_len = cache_b_k.shape[2]
    a_keep = cache_a_k.shape[2]
    keep_prompt = min(BAND_ROWS, seq)
    assert seq % ROW_TILE == 0 and keep_prompt == ROW_TILE and seq % B_TQ == 0
    assert t_sample <= a_keep and t_sample <= LANES and d_model == 2 * SEG

    rows_p = batch * seq
    rows_s = dec_batch * t_sample
    tiles_per_seq = seq // ROW_TILE
    tables_p = _rope_tables(np.arange(ROW_TILE), np.arange(tiles_per_seq) * ROW_TILE)
    tables_s = _rope_tables(past_len + np.arange(rows_s) % t_sample, np.zeros(1))
    a_dims = (A_HEADS, A_HEAD_DIM)
    bk_dims = (B_HEADS, 2, B_HEAD_DIM)
    bv_dims = (B_HEADS, B_V_DIM)

    yp = x_prompt.reshape(rows_p, d_model)
    ys = x_sample.reshape(rows_s, d_model)
    fg = final_gain.reshape(1, d_model)
    outs = [[] for _ in range(8)]
    for l in range(depth):
        lambda_init = 0.8 - 0.6 * math.exp(-0.3 * l)
        last = l == depth - 1
        w_l = w_in[l].astype(BF16)
        w_a = w_out[l, :A_WIDTH].astype(BF16)
        w_b = w_out[l, A_WIDTH:].astype(BF16)
        gain = norm_gain[l].reshape(1, d_model)
        lams = [a[l].reshape(1, B_HEAD_DIM).astype(F32)
                for a in (lambda_q1, lambda_k1, lambda_q2, lambda_k2)]
        subg = subln_gain[l].reshape(1, B_V_DIM).astype(F32)

        qa, ka, va, ga, qb, kb, vb, gb, kaf, vaf, kbf, vbf = _proj(
            yp, gain, w_l, tables_p, tm=ROW_TILE, tiles_per_seq=tiles_per_seq,
            tail_every=tiles_per_seq, channel_major=True, q_scale=LOG2E * B_HEAD_DIM ** -0.5)
        bias_p = _bias_rows(rel_bias[l], 1 - A_WIN, A_QB - 1, lambda t: t + BAND_ROWS, LOG2E)
        oa = _attn_a_prompt(qa, ka, va, ga, bias_p, batch=batch, seq=seq)
        ob = _attn_b_prompt(qb, kb, vb, gb, lams, subg, batch=batch, seq=seq,
                            lambda_init=lambda_init)
        yp = _out_proj(oa, ob, w_a, w_b, yp, fg, tm=OUT_ROW_TILE, final_norm=last,
                       steps_per_seq=seq // OUT_ROW_TILE)
        outs[0].append(_position_major(kaf, a_dims))
        outs[1].append(_position_major(vaf, a_dims))
        outs[2].append(_position_major(kbf, bk_dims))
        outs[3].append(vbf.reshape(batch, seq, *bv_dims))

        qa, ka, va, ga, qb, kb, vb, gb, kaf, vaf, kbf, vbf = _proj(
            ys, gain, w_l, tables_s, tm=rows_s, tiles_per_seq=1, tail_every=1,
            channel_major=False, q_scale=B_HEAD_DIM ** -0.5)
        oa, ak_new, av_new = _attn_a_sample(
            qa, ka, va, kaf, vaf, _channel_major(cache_a_k[l]), _channel_major(cache_a_v[l]),
            ga, _bias_rows(rel_bias[l], 1 - t_sample, a_keep + t_sample - 1, lambda t: a_keep - t),
            batch=dec_batch, t=t_sample)
        ob = _attn_b_sample(
            qb, kb, vb, _channel_major(cache_b_k[l]),
            cache_b_v[l].reshape(dec_batch, past_len * B_HEADS, B_V_DIM),
            gb, lams, subg, batch=dec_batch, t=t_sample, lambda_init=lambda_init)
        ys = _out_proj(oa, (ob,), w_a, w_b, ys, fg, tm=rows_s, final_norm=last)
        outs[4].append(_position_major(ak_new, a_dims))
        outs[5].append(_position_major(av_new, a_dims))
        outs[6].append(kbf.reshape(dec_batch, t_sample, *bk_dims))
        outs[7].append(vbf.reshape(dec_batch, t_sample, *bv_dims))

    return (yp.reshape(batch, seq, d_model), ys.reshape(dec_batch, t_sample, d_model),
            *[jnp.stack(o) for o in outs])
```

```python
import functools
import math

import numpy as np
import jax
import jax.numpy as jnp
from jax import lax
from jax.experimental import pallas as pl
from jax.experimental.pallas import tpu as pltpu

F32 = jnp.float32
BF16 = jnp.bfloat16
NEG_INF = float("-inf")

CHUNK = 64
N_PREV_CHUNKS = 8
A_HEADS = 8
A_HEAD_DIM = 64
A_WIDTH = A_HEADS * A_HEAD_DIM
B_HEADS = 4
B_HEAD_DIM = 64
B_V_DIM = 2 * B_HEAD_DIM
B_WIDTH = B_HEADS * B_V_DIM
SEG = 512
MAX_REL = 128
ROPE_THETA = 500000.0
ROPE_DIM = 16
EPS = 1e-6
LOG2E = math.log2(math.e)
LANES = 128
VMEM_LIMIT = 56 * 1024 * 1024
ROW_TILE = 512
OUT_ROW_TILE = 1024
BAND_ROWS = N_PREV_CHUNKS * CHUNK
A_QB = 256
A_WIN = BAND_ROWS + A_QB
A_ROWS = 64
A_QK_SPLIT = 3
A_SUM_ROWS = 16
B_TQ = 512
B_TK = 512
B_HEADS_PER_STEP = 4
B_STRIP = 256
B_ROWS = 64
B_QK_SPLIT = 2
B_SUM_ROWS = 16

NT_DIMS = (((1,), (1,)), ((), ()))


def _nt_dot(a, b):
    return lax.dot_general(a, b, NT_DIMS, preferred_element_type=F32)


def _dot(a, b):
    return jnp.dot(a, b, preferred_element_type=F32)


def _silu(g):
    return g / (1.0 + jnp.exp(-g))


def _round_up(n, m):
    return -(-n // m) * m


def _rope_lane_freq():
    d = np.arange(LANES) % B_HEAD_DIM
    inv = ROPE_THETA ** (-np.arange(0, ROPE_DIM, 2, dtype=np.float64) / ROPE_DIM)
    return np.where(d < ROPE_DIM, inv[d % (ROPE_DIM // 2)], 0.0)


def _rope_tables(row_pos, base_pos):
    f = _rope_lane_freq()[None, :]
    ar = np.asarray(row_pos, np.float64)[:, None] * f
    ab = np.asarray(base_pos, np.float64)[:, None] * f
    as32 = lambda a: jnp.asarray(a.astype(np.float32))
    return (as32(np.cos(ar)), as32(np.sin(ar)),
            as32(np.cos(ab))[:, None, :], as32(np.sin(ab))[:, None, :])


def _bias_rows(rel_bias, t_min, t_max, rel_of_t, scale=1.0):
    width = _round_up(t_max - t_min + 1, LANES)
    t = np.arange(width)
    t = np.where(t <= t_max, t, t - width)
    idx = np.clip(rel_of_t(t), -MAX_REL, MAX_REL) + MAX_REL
    return rel_bias.astype(F32)[:, idx] * scale


def _toeplitz(row, n_rows):
    return pltpu.roll(jnp.broadcast_to(row, (n_rows, row.shape[-1])), 0, 1, stride=1, stride_axis=0)


def _proj_kernel(x_ref, g_ref, w_ref, cr_ref, sr_ref, cb_ref, sb_ref, *out_refs,
                 channel_major, q_scale):
    (qa_ref, ka_ref, va_ref, ga_ref, qb_ref, kb_ref, vb_ref, gb_ref,
     kaf_ref, vaf_ref, kbf_ref, vbf_ref) = out_refs
    x = x_ref[...]
    inv = lax.rsqrt(jnp.mean(x * x, axis=-1, keepdims=True) + EPS)
    h = (x * inv * g_ref[...]).astype(BF16)

    def seg(k):
        return _dot(h, w_ref[:, k * SEG:(k + 1) * SEG])

    cb, sb = cb_ref[0], sb_ref[0]
    cr, sr = cr_ref[...], sr_ref[...]
    cos = cb * cr - sb * sr
    sin = sb * cr + cb * sr
    d = lax.broadcasted_iota(jnp.int32, cos.shape, 1) % B_HEAD_DIM
    sin_lo = jnp.where(d < ROPE_DIM // 2, -sin, 0.0)
    sin_hi = jnp.where(d >= ROPE_DIM // 2, sin, 0.0)

    def rope(z):
        cols = []
        for c in range(SEG // LANES):
            zc = z[:, c * LANES:(c + 1) * LANES]
            up = pltpu.roll(zc, LANES - ROPE_DIM // 2, 1)
            dn = pltpu.roll(zc, ROPE_DIM // 2, 1)
            cols.append(zc * cos + up * sin_lo + dn * sin_hi)
        return jnp.concatenate(cols, axis=1)

    qa = seg(0) * q_scale
    qa_ref[...] = (qa.T if channel_major else qa).astype(BF16)
    ka = seg(1)
    ka_ref[...] = ka.astype(BF16)
    va = seg(2)
    va_t = va.T if channel_major else va
    va_ref[...] = va_t.astype(BF16)
    kaf_ref[...] = ka.T if channel_major else ka
    vaf_ref[...] = va_t

    ga_ref[...] = seg(3).astype(BF16)
    qb = rope(seg(4)) * q_scale
    qb_ref[...] = (qb.T if channel_major else qb).astype(BF16)
    kb = rope(seg(5))
    kbf_ref[...] = kb.T if channel_major else kb
    kb_ref[...] = kb.astype(BF16)
    vb = seg(6)
    n_col = SEG // LANES
    for c in range(n_col):
        vbf_ref[pl.ds(c, vb.shape[0], stride=n_col), :] = vb[:, c * LANES:(c + 1) * LANES]
    vb_ref[...] = (vb.T if channel_major else vb).astype(BF16)
    gb_ref[...] = seg(7).astype(BF16)


def _proj(x, gain, w_bf16, tables, *, tm, tiles_per_seq, tail_every, channel_major, q_scale):
    rows, d_model = x.shape
    n_tiles = rows // tm
    n_tail = n_tiles // tail_every
    n_seq = n_tiles // tiles_per_seq
    cr, sr, cb, sb = tables
    row_blk = lambda i: (i, 0)
    const2 = lambda i: (0, 0)
    base_blk = lambda i: (i % tiles_per_seq, 0, 0)
    bf = jax.ShapeDtypeStruct((rows, SEG), BF16)
    f32_full = jax.ShapeDtypeStruct((rows, SEG), F32)
    full_spec = pl.BlockSpec((tm, SEG), row_blk)
    if channel_major:
        assert tm == SEG
        tail = jax.ShapeDtypeStruct((n_tail, SEG, tm), F32)
        tail_spec = pl.BlockSpec((None, SEG, tm), lambda i: (i // tail_every, 0, 0))
        kbf = jax.ShapeDtypeStruct((n_seq, SEG, tiles_per_seq * tm), F32)
        kbf_spec = pl.BlockSpec((None, SEG, tm), lambda i: (i // tiles_per_seq, 0, i % tiles_per_seq))
        bf_t = jax.ShapeDtypeStruct((n_tiles, SEG, tm), BF16)
        bf_t_spec = pl.BlockSpec((None, SEG, tm), lambda i: (i, 0, 0))
    else:
        tail = jax.ShapeDtypeStruct((n_tail * tm, SEG), F32)
        tail_spec = pl.BlockSpec((tm, SEG), lambda i: (i // tail_every, 0))
        kbf, kbf_spec = f32_full, full_spec
        bf_t, bf_t_spec = bf, full_spec
    return pl.pallas_call(
        functools.partial(_proj_kernel, channel_major=channel_major,
                          q_scale=q_scale),
        grid=(n_tiles,),
        in_specs=[
            pl.BlockSpec((tm, d_model), row_blk),
            pl.BlockSpec((1, d_model), const2),
            pl.BlockSpec(w_bf16.shape, const2),
            pl.BlockSpec((tm, LANES), const2),
            pl.BlockSpec((tm, LANES), const2),
            pl.BlockSpec((1, 1, LANES), base_blk),
            pl.BlockSpec((1, 1, LANES), base_blk),
        ],
        out_specs=([bf_t_spec, full_spec, bf_t_spec, full_spec] * 2
                   + [tail_spec, tail_spec, kbf_spec,
                      pl.BlockSpec((tm * SEG // LANES, LANES), row_blk)]),
        out_shape=([bf_t, bf, bf_t, bf] * 2
                   + [tail, tail, kbf, jax.ShapeDtypeStruct((rows * SEG // LANES, LANES), F32)]),
        compiler_params=pltpu.CompilerParams(
            dimension_semantics=("arbitrary",), vmem_limit_bytes=VMEM_LIMIT),
        name="proj",
    )(x, gain, w_bf16, cr, sr, cb, sb)


def _attn_a_prompt_kernel(qt_ref, kp_ref, kc_ref, vtp_ref, vtc_ref, g_ref, brow_ref, o_ref,
                          kcat_ref, vcat_ref, rhs_ref, bias_ref, ot_ref, sa_ref, sb_ref, sc_ref):
    n = pl.program_id(1)

    @pl.when((pl.program_id(0) == 0) & (n == 0))
    def _():
        j = lax.broadcasted_iota(jnp.int32, (A_WIN, A_QB), 0)
        i = lax.broadcasted_iota(jnp.int32, (A_WIN, A_QB), 1)
        gap = i // CHUNK + N_PREV_CHUNKS - j // CHUNK
        band = (gap >= 0) & (gap <= N_PREV_CHUNKS)
        for hd in range(A_HEADS):
            t = _toeplitz(brow_ref[hd:hd + 1, :], A_WIN)[:, :A_QB]
            bias_ref[hd] = jnp.where(band, t, NEG_INF)

    kcat_ref[0:ROW_TILE, :] = kp_ref[...]
    kcat_ref[ROW_TILE:2 * ROW_TILE, :] = kc_ref[...]
    vcat_ref[:, 0:ROW_TILE] = vtp_ref[...]
    vcat_ref[:, ROW_TILE:2 * ROW_TILE] = vtc_ref[...]
    for pair in range(A_HEADS // 2):
        qt = qt_ref[pair * LANES:(pair + 1) * LANES, :]
        row = lax.broadcasted_iota(jnp.int32, qt.shape, 0)
        zero = jnp.zeros_like(qt)
        rhs_ref[2 * pair] = jnp.where(row < A_HEAD_DIM, qt, zero)
        rhs_ref[2 * pair + 1] = jnp.where(row >= A_HEAD_DIM, qt, zero)

    items = [(q0, hd) for q0 in range(0, ROW_TILE, A_QB) for hd in range(A_HEADS)]

    def score_item(t, dst):
        q0, hd = items[t]
        pair = hd // 2
        half = A_WIN // A_QK_SPLIT
        for r in range(0, A_WIN, half):
            dst[r:r + half, :] = _dot(
                kcat_ref[q0 + r:q0 + r + half, pair * LANES:(pair + 1) * LANES],
                rhs_ref[hd, :, q0:q0 + A_QB]) + bias_ref[hd, r:r + half, :]

    def update_item(t, src, first_tile):
        q0, hd = items[t]
        r_min = BAND_ROWS - q0 if first_tile else 0
        rows = range(0, A_WIN, A_ROWS)
        live = [r for r in rows if r >= r_min]
        m = jnp.max(functools.reduce(jnp.maximum, [src[r:r + A_ROWS, :] for r in live]),
                    axis=0, keepdims=True)
        p = jnp.concatenate(
            [jnp.exp2((src[r:r + A_ROWS, :] - m).astype(BF16)) if r >= r_min
             else jnp.zeros((A_ROWS, A_QB), BF16) for r in rows], axis=0)
        vt1 = jnp.concatenate([vcat_ref[hd * A_HEAD_DIM:(hd + 1) * A_HEAD_DIM, q0:q0 + A_WIN],
                               jnp.ones((A_SUM_ROWS, A_WIN), BF16)], axis=0)
        o = _dot(vt1, p)
        ot_ref[hd * A_HEAD_DIM:(hd + 1) * A_HEAD_DIM, q0:q0 + A_QB] = (
            o[:A_HEAD_DIM] / o[A_HEAD_DIM:A_HEAD_DIM + 1])

    def run(first_tile):
        bufs = (sa_ref, sb_ref, sc_ref)
        depth = len(bufs) - 1
        for t in range(depth):
            score_item(t, bufs[t % len(bufs)])
        for t in range(len(items)):
            if t + depth < len(items):
                score_item(t + depth, bufs[(t + depth) % len(bufs)])
            update_item(t, bufs[t % len(bufs)], first_tile)

    @pl.when(n == 0)
    def _():
        run(True)

    @pl.when(n > 0)
    def _():
        run(False)

    o_ref[...] = (ot_ref[...].T * _silu(g_ref[...].astype(F32))).astype(BF16)


def _attn_a_prompt(qat, ka, vat, ga, bias_rows, *, batch, seq):
    tiles = seq // ROW_TILE
    cur = lambda b, n: (b * tiles + n, 0)
    prev = lambda b, n: (b * tiles + jnp.maximum(n - 1, 0), 0)
    cur_t = lambda b, n: (b * tiles + n, 0, 0)
    prev_t = lambda b, n: (b * tiles + jnp.maximum(n - 1, 0), 0, 0)
    blk = (ROW_TILE, A_WIDTH)
    blk_t = (None, A_WIDTH, ROW_TILE)
    return pl.pallas_call(
        _attn_a_prompt_kernel,
        grid=(batch, tiles),
        in_specs=[
            pl.BlockSpec(blk_t, cur_t),
            pl.BlockSpec(blk, prev), pl.BlockSpec(blk, cur),
            pl.BlockSpec(blk_t, prev_t), pl.BlockSpec(blk_t, cur_t),
            pl.BlockSpec(blk, cur),
            pl.BlockSpec(bias_rows.shape, lambda b, n: (0, 0)),
        ],
        out_specs=pl.BlockSpec(blk, cur),
        out_shape=jax.ShapeDtypeStruct(ka.shape, BF16),
        scratch_shapes=[pltpu.VMEM((2 * ROW_TILE, A_WIDTH), BF16),
                        pltpu.VMEM((A_WIDTH, 2 * ROW_TILE), BF16),
                        pltpu.VMEM((A_HEADS, LANES, ROW_TILE), BF16),
                        pltpu.VMEM((A_HEADS, A_WIN, A_QB), F32),
                        pltpu.VMEM((A_WIDTH, ROW_TILE), F32),
                        pltpu.VMEM((A_WIN, A_QB), F32),
                        pltpu.VMEM((A_WIN, A_QB), F32),
                        pltpu.VMEM((A_WIN, A_QB), F32)],
        compiler_params=pltpu.CompilerParams(
            dimension_semantics=("arbitrary", "arbitrary"), vmem_limit_bytes=VMEM_LIMIT),
        name="attn_a_prompt",
    )(qat, ka, ka, vat, vat, ga, bias_rows)


def _diff_lambda(lq1_ref, lk1_ref, lq2_ref, lk2_ref, lambda_init):
    e1 = jnp.exp(jnp.sum(lq1_ref[...] * lk1_ref[...], axis=-1, keepdims=True))
    e2 = jnp.exp(jnp.sum(lq2_ref[...] * lk2_ref[...], axis=-1, keepdims=True))
    return e1 - e2 + lambda_init


def _stack_maps(qh):
    lane = lax.broadcasted_iota(jnp.int32, qh.shape, 1)
    zero = jnp.zeros_like(qh)
    return jnp.concatenate([jnp.where(lane < B_HEAD_DIM, qh, zero),
                            jnp.where(lane >= B_HEAD_DIM, qh, zero)], axis=0)


def _subnorm_gate(o, subg_ref, gate, lambda_init):
    o = o * lax.rsqrt(jnp.mean(o * o, axis=-1, keepdims=True) + EPS) * subg_ref[...]
    return (o * (1.0 - lambda_init)) * _silu(gate)


def _attn_b_prompt_kernel(qta_ref, qtb_ref, k_ref, vt_ref, ga_ref, gb_ref,
                          lq1_ref, lk1_ref, lq2_ref, lk2_ref, subg_ref, oa_ref, ob_ref,
                          rhs_ref, acc_ref, m_ref, sa_ref, sb_ref, *, lambda_init, tiles):
    g = pl.program_id(2)
    heads = rhs_ref.shape[1]
    for t, qt_ref in enumerate((qta_ref, qtb_ref)):
        for h in range(heads):
            qt = qt_ref[h * B_V_DIM:(h + 1) * B_V_DIM, :]
            row = lax.broadcasted_iota(jnp.int32, qt.shape, 0)
            zero = jnp.zeros_like(qt)
            rhs_ref[t, h, :, :B_TQ] = jnp.where(row < B_HEAD_DIM, qt, zero)
            rhs_ref[t, h, :, B_TQ:] = jnp.where(row >= B_HEAD_DIM, qt, zero)
    m_ref[...] = jnp.full(m_ref.shape, NEG_INF, F32)
    acc_ref[...] = jnp.zeros(acc_ref.shape, F32)

    items = [(h, c) for h in range(heads) for c in range(2 * B_TQ // B_STRIP)]

    def live_keys(c, diagonal):
        q0 = (c * B_STRIP) % B_TQ
        return min(B_TK, q0 + B_STRIP) if diagonal else B_TK

    def score_item(tile, j, n, dst, diagonal):
        h, c = items[n]
        part = B_TK // B_QK_SPLIT
        for r in range(0, live_keys(c, diagonal), part):
            k0 = pl.multiple_of(j * B_TK + r, part)
            dst[n, r:r + part, :] = _dot(k_ref[pl.ds(k0, part), h * B_V_DIM:(h + 1) * B_V_DIM],
                                         rhs_ref[tile, h, :, c * B_STRIP:(c + 1) * B_STRIP])

    def update_item(tile, j, n, src, diagonal):
        h, c = items[n]
        hs = slice(h * B_V_DIM, (h + 1) * B_V_DIM)
        cols = slice(c * B_STRIP, (c + 1) * B_STRIP)
        live = live_keys(c, diagonal)

        def chunk(r):
            s = src[n, r:r + B_ROWS, :]
            if diagonal:
                q0 = (c * B_STRIP) % B_TQ
                kc = (r + lax.broadcasted_iota(jnp.int32, s.shape, 0)) // CHUNK
                qc = (q0 + lax.broadcasted_iota(jnp.int32, s.shape, 1)) // CHUNK
                s = jnp.where(kc <= qc, s, NEG_INF)
            return s

        rows = range(0, live, B_ROWS)
        m_blk = functools.reduce(jnp.maximum, [chunk(r) for r in rows])
        m_old = m_ref[tile, h, :, cols]
        m_new = jnp.maximum(m_old, jnp.max(m_blk, axis=0, keepdims=True))
        alpha = jnp.exp2(m_old - m_new)
        m_ref[tile, h, :, cols] = m_new
        p = jnp.concatenate([jnp.exp2((chunk(r) - m_new).astype(BF16)) for r in rows], axis=0)
        vt1 = jnp.concatenate([vt_ref[j, hs, :live], jnp.ones((B_SUM_ROWS, live), BF16)], axis=0)
        acc_ref[tile, h, :, cols] = alpha * acc_ref[tile, h, :, cols] + _dot(vt1, p)

    def stage(score, dst, update, src, score_diagonal=False, update_diagonal=False):
        for n in range(len(items)):
            if score is not None:
                score_item(*score, n, dst, score_diagonal)
            if update is not None:
                update_item(*update, n, src, update_diagonal)

    first = (0, g)
    last = (1, tiles - 1 - g)

    def at(s):
        in_lo = s <= g
        return jnp.where(in_lo, 0, 1), jnp.where(in_lo, s - 1, s - 1 - g)

    def pair(p, carry):
        s = 2 * p + 1
        stage(at(s + 1), sa_ref, at(s), sb_ref)
        stage(at(s + 2), sb_ref, at(s + 1), sa_ref)
        return carry

    stage(first, sa_ref, None, None, score_diagonal=True)
    stage(at(1), sb_ref, first, sa_ref, update_diagonal=True)
    lax.fori_loop(0, (tiles - 2) // 2, pair, 0)
    stage(last, sa_ref, at(tiles - 1), sb_ref, score_diagonal=True)
    stage(None, None, last, sa_ref, update_diagonal=True)

    lam = _diff_lambda(lq1_ref, lk1_ref, lq2_ref, lk2_ref, lambda_init)
    for t, (g_ref, o_ref) in enumerate(((ga_ref, oa_ref), (gb_ref, ob_ref))):
        for h in range(heads):
            hs = slice(h * B_V_DIM, (h + 1) * B_V_DIM)
            o = acc_ref[t, h, :B_V_DIM, :] / acc_ref[t, h, B_V_DIM:B_V_DIM + 1, :]
            o = (o[:, :B_TQ] - lam * o[:, B_TQ:]).T
            o_ref[:, hs] = _subnorm_gate(o, subg_ref, g_ref[:, hs].astype(F32),
                                         lambda_init).astype(BF16)


def _attn_b_prompt(qbt, kb, vbt, gb, lams, subg, *, batch, seq, lambda_init):
    assert B_TQ == B_TK == ROW_TILE and B_HEADS % B_HEADS_PER_STEP == 0
    tiles = seq // B_TQ
    assert tiles % 2 == 0
    half = tiles // 2
    width = B_HEADS_PER_STEP * B_V_DIM
    n_items = B_HEADS_PER_STEP * 2 * B_TQ // B_STRIP
    lo = lambda b, g: b * tiles + g
    hi = lambda b, g: b * tiles + tiles - 1 - g
    vec = pl.BlockSpec((1, B_HEAD_DIM), lambda b, h, g: (0, 0))
    vbt = vbt.reshape(batch, tiles, B_WIDTH, B_TK)
    out = jax.ShapeDtypeStruct((batch * half * B_TQ, B_WIDTH), BF16)
    return pl.pallas_call(
        functools.partial(_attn_b_prompt_kernel, lambda_init=lambda_init, tiles=tiles),
        grid=(batch, B_HEADS // B_HEADS_PER_STEP, half),
        in_specs=[
            pl.BlockSpec((None, width, B_TQ), lambda b, h, g: (lo(b, g), h, 0)),
            pl.BlockSpec((None, width, B_TQ), lambda b, h, g: (hi(b, g), h, 0)),
            pl.BlockSpec((seq, width), lambda b, h, g: (b, h), pipeline_mode=pl.Buffered(1)),
            pl.BlockSpec((None, tiles, width, B_TK), lambda b, h, g: (b, 0, h, 0),
                         pipeline_mode=pl.Buffered(1)),
            pl.BlockSpec((B_TQ, width), lambda b, h, g: (lo(b, g), h)),
            pl.BlockSpec((B_TQ, width), lambda b, h, g: (hi(b, g), h)),
            vec, vec, vec, vec,
            pl.BlockSpec((1, B_V_DIM), lambda b, h, g: (0, 0)),
        ],
        out_specs=[pl.BlockSpec((B_TQ, width), lambda b, h, g: (b * half + g, h)),
                   pl.BlockSpec((B_TQ, width), lambda b, h, g: (b * half + half - 1 - g, h))],
        out_shape=[out, out],
        scratch_shapes=[pltpu.VMEM((2, B_HEADS_PER_STEP, B_V_DIM, 2 * B_TQ), BF16),
                        pltpu.VMEM((2, B_HEADS_PER_STEP, B_V_DIM + B_SUM_ROWS, 2 * B_TQ), F32),
                        pltpu.VMEM((2, B_HEADS_PER_STEP, 1, 2 * B_TQ), F32),
                        pltpu.VMEM((n_items, B_TK, B_STRIP), F32),
                        pltpu.VMEM((n_items, B_TK, B_STRIP), F32)],
        compiler_params=pltpu.CompilerParams(
            dimension_semantics=("arbitrary", "arbitrary", "arbitrary"),
            vmem_limit_bytes=VMEM_LIMIT),
        name="attn_b_prompt",
    )(qbt, qbt, kb, vbt, gb, gb, *lams, subg)


def _roll_in(cache_t, new_rows):
    t = new_rows.shape[0]
    keep = cache_t.shape[1]
    shifted = pltpu.roll(cache_t, keep - t, 1)
    pad = jnp.concatenate([jnp.zeros((LANES - t, new_rows.shape[1]), F32), new_rows], axis=0)
    new_t = pad.T
    lane = lax.broadcasted_iota(jnp.int32, new_t.shape, 1)
    last = jnp.where(lane >= LANES - t, new_t, shifted[:, keep - LANES:])
    return jnp.concatenate([shifted[:, :keep - LANES], last], axis=1)


def _attn_a_sample_kernel(q_ref, kn_ref, vn_ref, knf_ref, vnf_ref, kc_ref, vc_ref, g_ref,
                          brow_ref, o_ref, ko_ref, vo_ref):
    t = q_ref.shape[0]
    keep = kc_ref.shape[1]
    lane = lax.broadcasted_iota(jnp.int32, (t, LANES), 1)
    low_half = lane < A_HEAD_DIM
    for pair in range(A_HEADS // 2):
        cs = slice(pair * LANES, (pair + 1) * LANES)
        qp = q_ref[:, cs]
        kc = kc_ref[cs, :].astype(BF16)
        vc = vc_ref[cs, :].astype(BF16)
        kn = kn_ref[:, cs]
        vn = vn_ref[:, cs]
        outs = []
        for e in range(2):
            hd = 2 * pair + e
            bias = _toeplitz(brow_ref[hd:hd + 1, :], t)
            qm = jnp.where(low_half if e == 0 else ~low_half, qp, jnp.zeros_like(qp))
            s_c = _dot(qm, kc) + bias[:, :keep]
            s_n = _nt_dot(qm, kn) + bias[:, keep:keep + t]
            m = jnp.maximum(jnp.max(s_c, axis=-1, keepdims=True),
                            jnp.max(s_n, axis=-1, keepdims=True))
            p_c = jnp.exp(s_c - m)
            p_n = jnp.exp(s_n - m)
            l = jnp.sum(p_c, axis=-1, keepdims=True) + jnp.sum(p_n, axis=-1, keepdims=True)
            o = _nt_dot(p_c.astype(BF16), vc) + _dot(p_n.astype(BF16), vn)
            outs.append(o / l)
        o_pair = jnp.where(low_half, outs[0], outs[1])
        o_ref[:, cs] = (o_pair * _silu(g_ref[:, cs].astype(F32))).astype(BF16)
    ko_ref[...] = _roll_in(kc_ref[...], knf_ref[...])
    vo_ref[...] = _roll_in(vc_ref[...], vnf_ref[...])


def _attn_a_sample(qa, ka, va, kaf, vaf, cache_kt, cache_vt, ga, bias_rows, *, batch, t):
    keep = cache_kt.shape[2]
    new = pl.BlockSpec((t, A_WIDTH), lambda b: (b, 0))
    cache = pl.BlockSpec((None, A_WIDTH, keep), lambda b: (b, 0, 0))
    return pl.pallas_call(
        _attn_a_sample_kernel,
        grid=(batch,),
        in_specs=[new, new, new, new, new, cache, cache, new,
                  pl.BlockSpec(bias_rows.shape, lambda b: (0, 0))],
        out_specs=[new, cache, cache],
        out_shape=[jax.ShapeDtypeStruct(qa.shape, BF16),
                   jax.ShapeDtypeStruct(cache_kt.shape, F32),
                   jax.ShapeDtypeStruct(cache_vt.shape, F32)],
        compiler_params=pltpu.CompilerParams(
            dimension_semantics=("arbitrary",), vmem_limit_bytes=VMEM_LIMIT),
        name="attn_a_sample",
    )(qa, ka, va, kaf, vaf, cache_kt, cache_vt, ga, bias_rows)


def _attn_b_sample_kernel(q_ref, kn_ref, vn_ref, kc_ref, vc_ref, g_ref,
                          lq1_ref, lk1_ref, lq2_ref, lk2_ref, subg_ref, o_ref, *, lambda_init):
    t = q_ref.shape[0]
    past = kc_ref.shape[1]
    lam = _diff_lambda(lq1_ref, lk1_ref, lq2_ref, lk2_ref, lambda_init)
    for h in range(B_HEADS):
        cs = slice(h * B_V_DIM, (h + 1) * B_V_DIM)
        qs = _stack_maps(q_ref[:, cs])
        s_c = _dot(qs, kc_ref[cs, :].astype(BF16))
        s_n = _nt_dot(qs, kn_ref[:, cs])
        m = jnp.maximum(jnp.max(s_c, axis=-1, keepdims=True),
                        jnp.max(s_n, axis=-1, keepdims=True))
        p_c = jnp.exp(s_c - m)
        p_n = jnp.exp(s_n - m)
        l = jnp.sum(p_c, axis=-1, keepdims=True) + jnp.sum(p_n, axis=-1, keepdims=True)
        p_c = p_c / l
        p_n = p_n / l
        a_c = (p_c[:t] - lam * p_c[t:]).astype(BF16)
        a_n = (p_n[:t] - lam * p_n[t:]).astype(BF16)
        vc = vc_ref[pl.ds(h, past, stride=B_HEADS), :].astype(BF16)
        o = _dot(a_c, vc) + _dot(a_n, vn_ref[:, cs])
        o_ref[:, cs] = _subnorm_gate(o, subg_ref, g_ref[:, cs].astype(F32),
                                     lambda_init).astype(BF16)


def _attn_b_sample(qb, kb, vb, cache_kt, cache_v, gb, lams, subg, *, batch, t, lambda_init):
    past = cache_kt.shape[2]
    new = pl.BlockSpec((t, B_WIDTH), lambda b: (b, 0))
    vec = pl.BlockSpec((1, B_HEAD_DIM), lambda b: (0, 0))
    return pl.pallas_call(
        functools.partial(_attn_b_sample_kernel, lambda_init=lambda_init),
        grid=(batch,),
        in_specs=[new, new, new,
                  pl.BlockSpec((None, B_WIDTH, past), lambda b: (b, 0, 0)),
                  pl.BlockSpec((None, past * B_HEADS, B_V_DIM), lambda b: (b, 0, 0)),
                  new, vec, vec, vec, vec,
                  pl.BlockSpec((1, B_V_DIM), lambda b: (0, 0))],
        out_specs=new,
        out_shape=jax.ShapeDtypeStruct(qb.shape, BF16),
        compiler_params=pltpu.CompilerParams(
            dimension_semantics=("arbitrary",), vmem_limit_bytes=VMEM_LIMIT),
        name="attn_b_sample",
    )(qb, kb, vb, cache_kt, cache_v, gb, *lams, subg)


def _out_kernel(oa_ref, *refs, final_norm, steps_per_seq):
    *ob_refs, wa_ref, wb_ref, x_ref, fg_ref, y_ref = refs
    if len(ob_refs) == 2:
        in_lo = pl.program_id(0) % steps_per_seq < steps_per_seq // 2
        ob = jnp.where(in_lo, ob_refs[0][...], ob_refs[1][...])
    else:
        ob = ob_refs[0][...]
    y = x_ref[...] + _dot(oa_ref[...], wa_ref[...]) + _dot(ob, wb_ref[...])
    if final_norm:
        y = y * lax.rsqrt(jnp.mean(y * y, axis=-1, keepdims=True) + EPS) * fg_ref[...]
    y_ref[...] = y


def _out_proj(oa, ob_parts, w_a, w_b, x, final_gain, *, tm, final_norm, steps_per_seq=1):
    rows, d_model = x.shape
    row_blk = lambda i: (i, 0)
    const2 = lambda i: (0, 0)
    if len(ob_parts) == 2:
        assert steps_per_seq % 2 == 0
        hs = steps_per_seq // 2
        seq_of = lambda i: i // steps_per_seq
        step_of = lambda i: i % steps_per_seq
        ob_specs = [
            pl.BlockSpec((tm, B_WIDTH),
                         lambda i: (seq_of(i) * hs + jnp.minimum(step_of(i), hs - 1), 0)),
            pl.BlockSpec((tm, B_WIDTH),
                         lambda i: (seq_of(i) * hs + jnp.maximum(step_of(i) - hs, 0), 0)),
        ]
    else:
        ob_specs = [pl.BlockSpec((tm, B_WIDTH), row_blk)]
    return pl.pallas_call(
        functools.partial(_out_kernel, final_norm=final_norm, steps_per_seq=steps_per_seq),
        grid=(rows // tm,),
        in_specs=[
            pl.BlockSpec((tm, A_WIDTH), row_blk),
            *ob_specs,
            pl.BlockSpec(w_a.shape, const2),
            pl.BlockSpec(w_b.shape, const2),
            pl.BlockSpec((tm, d_model), row_blk),
            pl.BlockSpec((1, d_model), const2),
        ],
        out_specs=pl.BlockSpec((tm, d_model), row_blk),
        out_shape=jax.ShapeDtypeStruct(x.shape, F32),
        compiler_params=pltpu.CompilerParams(
            dimension_semantics=("arbitrary",), vmem_limit_bytes=VMEM_LIMIT),
        name="out_proj",
    )(oa, *ob_parts, w_a, w_b, x, final_gain)


def _channel_major(a):
    n, pos = a.shape[:2]
    return jnp.moveaxis(a.reshape(n, pos, -1), 1, 2)


def _position_major(a, channel_dims):
    n, _, pos = a.shape
    return jnp.moveaxis(a, 2, 1).reshape(n, pos, *channel_dims)


def kernel(x_prompt, x_sample, cache_a_k, cache_a_v, cache_b_k, cache_b_v,
           norm_gain, w_in, w_out, rel_bias, lambda_q1, lambda_k1, lambda_q2, lambda_k2,
           subln_gain, final_gain):
    batch, seq, d_model = x_prompt.shape
    dec_batch, t_sample, _ = x_sample.shape
    depth = w_in.shape[0]
    past_len = cache_b_k.shape[2]
    a_keep = cache_a_k.shape[2]
    keep_prompt = min(BAND_ROWS, seq)
    assert seq % ROW_TILE == 0 and keep_prompt == ROW_TILE and seq % B_TQ == 0
    assert t_sample <= a_keep and t_sample <= LANES and d_model == 2 * SEG

    rows_p = batch * seq
    rows_s = dec_batch * t_sample
    tiles_per_seq = seq // ROW_TILE
    tables_p = _rope_tables(np.arange(ROW_TILE), np.arange(tiles_per_seq) * ROW_TILE)
    tables_s = _rope_tables(past_len + np.arange(rows_s) % t_sample, np.zeros(1))
    a_dims = (A_HEADS, A_HEAD_DIM)
    bk_dims = (B_HEADS, 2, B_HEAD_DIM)
    bv_dims = (B_HEADS, B_V_DIM)

    yp = x_prompt.reshape(rows_p, d_model)
    ys = x_sample.reshape(rows_s, d_model)
    fg = final_gain.reshape(1, d_model)
    outs = [[] for _ in range(8)]
    for l in range(depth):
        lambda_init = 0.8 - 0.6 * math.exp(-0.3 * l)
        last = l == depth - 1
        w_l = w_in[l].astype(BF16)
        w_a = w_out[l, :A_WIDTH].astype(BF16)
        w_b = w_out[l, A_WIDTH:].astype(BF16)
        gain = norm_gain[l].reshape(1, d_model)
        lams = [a[l].reshape(1, B_HEAD_DIM).astype(F32)
                for a in (lambda_q1, lambda_k1, lambda_q2, lambda_k2)]
        subg = subln_gain[l].reshape(1, B_V_DIM).astype(F32)

        qa, ka, va, ga, qb, kb, vb, gb, kaf, vaf, kbf, vbf = _proj(
            yp, gain, w_l, tables_p, tm=ROW_TILE, tiles_per_seq=tiles_per_seq,
            tail_every=tiles_per_seq, channel_major=True, q_scale=LOG2E * B_HEAD_DIM ** -0.5)
        bias_p = _bias_rows(rel_bias[l], 1 - A_WIN, A_QB - 1, lambda t: t + BAND_ROWS, LOG2E)
        oa = _attn_a_prompt(qa, ka, va, ga, bias_p, batch=batch, seq=seq)
        ob = _attn_b_prompt(qb, kb, vb, gb, lams, subg, batch=batch, seq=seq,
                            lambda_init=lambda_init)
        yp = _out_proj(oa, ob, w_a, w_b, yp, fg, tm=OUT_ROW_TILE, final_norm=last,
                       steps_per_seq=seq // OUT_ROW_TILE)
        outs[0].append(_position_major(kaf, a_dims))
        outs[1].append(_position_major(vaf, a_dims))
        outs[2].append(_position_major(kbf, bk_dims))
        outs[3].append(vbf.reshape(batch, seq, *bv_dims))

        qa, ka, va, ga, qb, kb, vb, gb, kaf, vaf, kbf, vbf = _proj(
            ys, gain, w_l, tables_s, tm=rows_s, tiles_per_seq=1, tail_every=1,
            channel_major=False, q_scale=B_HEAD_DIM ** -0.5)
        oa, ak_new, av_new = _attn_a_sample(
            qa, ka, va, kaf, vaf, _channel_major(cache_a_k[l]), _channel_major(cache_a_v[l]),
            ga, _bias_rows(rel_bias[l], 1 - t_sample, a_keep + t_sample - 1, lambda t: a_keep - t),
            batch=dec_batch, t=t_sample)
        ob = _attn_b_sample(
            qb, kb, vb, _channel_major(cache_b_k[l]),
            cache_b_v[l].reshape(dec_batch, past_len * B_HEADS, B_V_DIM),
            gb, lams, subg, batch=dec_batch, t=t_sample, lambda_init=lambda_init)
        ys = _out_proj(oa, (ob,), w_a, w_b, ys, fg, tm=rows_s, final_norm=last)
        outs[4].append(_position_major(ak_new, a_dims))
        outs[5].append(_position_major(av_new, a_dims))
        outs[6].append(kbf.reshape(dec_batch, t_sample, *bk_dims))
        outs[7].append(vbf.reshape(dec_batch, t_sample, *bv_dims))

    return (yp.reshape(batch, seq, d_model), ys.reshape(dec_batch, t_sample, d_model),
            *[jnp.stack(o) for o in outs])
```

```python
import functools
import math

import numpy as np
import jax
import jax.numpy as jnp
from jax import lax
from jax.experimental import pallas as pl
from jax.experimental.pallas import tpu as pltpu

F32 = jnp.float32
BF16 = jnp.bfloat16
NEG_INF = float("-inf")

CHUNK = 64
N_PREV_CHUNKS = 8
A_HEADS = 8
A_HEAD_DIM = 64
A_WIDTH = A_HEADS * A_HEAD_DIM
B_HEADS = 4
B_HEAD_DIM = 64
B_V_DIM = 2 * B_HEAD_DIM
B_WIDTH = B_HEADS * B_V_DIM
SEG = 512
MAX_REL = 128
ROPE_THETA = 500000.0
ROPE_DIM = 16
EPS = 1e-6
LOG2E = math.log2(math.e)
LANES = 128
VMEM_LIMIT = 56 * 1024 * 1024
ROW_TILE = 512
OUT_ROW_TILE = 1024
BAND_ROWS = N_PREV_CHUNKS * CHUNK
A_QB = 256
A_WIN = BAND_ROWS + A_QB
A_ROWS = 64
A_QK_SPLIT = 3
A_SUM_ROWS = 16
B_TQ = 512
B_TK = 512
B_HEADS_PER_STEP = 4
B_STRIP = 256
B_ROWS = 64
B_QK_SPLIT = 2
B_SUM_ROWS = 16

NT_DIMS = (((1,), (1,)), ((), ()))


def _nt_dot(a, b):
    return lax.dot_general(a, b, NT_DIMS, preferred_element_type=F32)


def _dot(a, b):
    return jnp.dot(a, b, preferred_element_type=F32)


def _silu(g):
    return g / (1.0 + jnp.exp(-g))


def _round_up(n, m):
    return -(-n // m) * m


def _rope_lane_freq():
    d = np.arange(LANES) % B_HEAD_DIM
    inv = ROPE_THETA ** (-np.arange(0, ROPE_DIM, 2, dtype=np.float64) / ROPE_DIM)
    return np.where(d < ROPE_DIM, inv[d % (ROPE_DIM // 2)], 0.0)


def _rope_tables(row_pos, base_pos):
    f = _rope_lane_freq()[None, :]
    ar = np.asarray(row_pos, np.float64)[:, None] * f
    ab = np.asarray(base_pos, np.float64)[:, None] * f
    as32 = lambda a: jnp.asarray(a.astype(np.float32))
    return (as32(np.cos(ar)), as32(np.sin(ar)),
            as32(np.cos(ab))[:, None, :], as32(np.sin(ab))[:, None, :])


def _bias_rows(rel_bias, t_min, t_max, rel_of_t, scale=1.0):
    width = _round_up(t_max - t_min + 1, LANES)
    t = np.arange(width)
    t = np.where(t <= t_max, t, t - width)
    idx = np.clip(rel_of_t(t), -MAX_REL, MAX_REL) + MAX_REL
    return rel_bias.astype(F32)[:, idx] * scale


def _toeplitz(row, n_rows):
    return pltpu.roll(jnp.broadcast_to(row, (n_rows, row.shape[-1])), 0, 1, stride=1, stride_axis=0)


def _proj_kernel(x_ref, g_ref, w_ref, cr_ref, sr_ref, cb_ref, sb_ref, *out_refs,
                 channel_major, q_scale):
    (qa_ref, ka_ref, va_ref, ga_ref, qb_ref, kb_ref, vb_ref, gb_ref,
     kaf_ref, vaf_ref, kbf_ref, vbf_ref) = out_refs
    x = x_ref[...]
    inv = lax.rsqrt(jnp.mean(x * x, axis=-1, keepdims=True) + EPS)
    h = (x * inv * g_ref[...]).astype(BF16)

    def seg(k):
        return _dot(h, w_ref[:, k * SEG:(k + 1) * SEG])

    cb, sb = cb_ref[0], sb_ref[0]
    cr, sr = cr_ref[...], sr_ref[...]
    cos = cb * cr - sb * sr
    sin = sb * cr + cb * sr
    d = lax.broadcasted_iota(jnp.int32, cos.shape, 1) % B_HEAD_DIM
    sin_lo = jnp.where(d < ROPE_DIM // 2, -sin, 0.0)
    sin_hi = jnp.where(d >= ROPE_DIM // 2, sin, 0.0)

    def rope(z):
        cols = []
        for c in range(SEG // LANES):
            zc = z[:, c * LANES:(c + 1) * LANES]
            up = pltpu.roll(zc, LANES - ROPE_DIM // 2, 1)
            dn = pltpu.roll(zc, ROPE_DIM // 2, 1)
            cols.append(zc * cos + up * sin_lo + dn * sin_hi)
        return jnp.concatenate(cols, axis=1)

    qa = seg(0) * q_scale
    qa_ref[...] = (qa.T if channel_major else qa).astype(BF16)
    ka = seg(1)
    ka_ref[...] = ka.astype(BF16)
    va = seg(2)
    va_t = va.T if channel_major else va
    va_ref[...] = va_t.astype(BF16)
    kaf_ref[...] = ka.T if channel_major else ka
    vaf_ref[...] = va_t

    ga_ref[...] = seg(3).astype(BF16)
    qb = rope(seg(4)) * q_scale
    qb_ref[...] = (qb.T if channel_major else qb).astype(BF16)
    kb = rope(seg(5))
    kbf_ref[...] = kb.T if channel_major else kb
    kb_ref[...] = kb.astype(BF16)
    vb = seg(6)
    n_col = SEG // LANES
    for c in range(n_col):
        vbf_ref[pl.ds(c, vb.shape[0], stride=n_col), :] = vb[:, c * LANES:(c + 1) * LANES]
    vb_ref[...] = (vb.T if channel_major else vb).astype(BF16)
    gb_ref[...] = seg(7).astype(BF16)


def _proj(x, gain, w_bf16, tables, *, tm, tiles_per_seq, tail_every, channel_major, q_scale):
    rows, d_model = x.shape
    n_tiles = rows // tm
    n_tail = n_tiles // tail_every
    n_seq = n_tiles // tiles_per_seq
    cr, sr, cb, sb = tables
    row_blk = lambda i: (i, 0)
    const2 = lambda i: (0, 0)
    base_blk = lambda i: (i % tiles_per_seq, 0, 0)
    bf = jax.ShapeDtypeStruct((rows, SEG), BF16)
    f32_full = jax.ShapeDtypeStruct((rows, SEG), F32)
    full_spec = pl.BlockSpec((tm, SEG), row_blk)
    if channel_major:
        assert tm == SEG
        tail = jax.ShapeDtypeStruct((n_tail, SEG, tm), F32)
        tail_spec = pl.BlockSpec((None, SEG, tm), lambda i: (i // tail_every, 0, 0))
        kbf = jax.ShapeDtypeStruct((n_seq, SEG, tiles_per_seq * tm), F32)
        kbf_spec = pl.BlockSpec((None, SEG, tm), lambda i: (i // tiles_per_seq, 0, i % tiles_per_seq))
        bf_t = jax.ShapeDtypeStruct((n_tiles, SEG, tm), BF16)
        bf_t_spec = pl.BlockSpec((None, SEG, tm), lambda i: (i, 0, 0))
    else:
        tail = jax.ShapeDtypeStruct((n_tail * tm, SEG), F32)
        tail_spec = pl.BlockSpec((tm, SEG), lambda i: (i // tail_every, 0))
        kbf, kbf_spec = f32_full, full_spec
        bf_t, bf_t_spec = bf, full_spec
    return pl.pallas_call(
        functools.partial(_proj_kernel, channel_major=channel_major,
                          q_scale=q_scale),
        grid=(n_tiles,),
        in_specs=[
            pl.BlockSpec((tm, d_model), row_blk),
            pl.BlockSpec((1, d_model), const2),
            pl.BlockSpec(w_bf16.shape, const2),
            pl.BlockSpec((tm, LANES), const2),
            pl.BlockSpec((tm, LANES), const2),
            pl.BlockSpec((1, 1, LANES), base_blk),
            pl.BlockSpec((1, 1, LANES), base_blk),
        ],
        out_specs=([bf_t_spec, full_spec, bf_t_spec, full_spec] * 2
                   + [tail_spec, tail_spec, kbf_spec,
                      pl.BlockSpec((tm * SEG // LANES, LANES), row_blk)]),
        out_shape=([bf_t, bf, bf_t, bf] * 2
                   + [tail, tail, kbf, jax.ShapeDtypeStruct((rows * SEG // LANES, LANES), F32)]),
        compiler_params=pltpu.CompilerParams(
            dimension_semantics=("arbitrary",), vmem_limit_bytes=VMEM_LIMIT),
        name="proj",
    )(x, gain, w_bf16, cr, sr, cb, sb)


def _attn_a_prompt_kernel(qt_ref, kp_ref, kc_ref, vtp_ref, vtc_ref, g_ref, brow_ref, o_ref,
                          kcat_ref, vcat_ref, rhs_ref, bias_ref, ot_ref, sa_ref, sb_ref, sc_ref):
    n = pl.program_id(1)

    @pl.when((pl.program_id(0) == 0) & (n == 0))
    def _():
        j = lax.broadcasted_iota(jnp.int32, (A_WIN, A_QB), 0)
        i = lax.broadcasted_iota(jnp.int32, (A_WIN, A_QB), 1)
        gap = i // CHUNK + N_PREV_CHUNKS - j // CHUNK
        band = (gap >= 0) & (gap <= N_PREV_CHUNKS)
        for hd in range(A_HEADS):
            t = _toeplitz(brow_ref[hd:hd + 1, :], A_WIN)[:, :A_QB]
            bias_ref[hd] = jnp.where(band, t, NEG_INF)

    kcat_ref[0:ROW_TILE, :] = kp_ref[...]
    kcat_ref[ROW_TILE:2 * ROW_TILE, :] = kc_ref[...]
    vcat_ref[:, 0:ROW_TILE] = vtp_ref[...]
    vcat_ref[:, ROW_TILE:2 * ROW_TILE] = vtc_ref[...]
    for pair in range(A_HEADS // 2):
        qt = qt_ref[pair * LANES:(pair + 1) * LANES, :]
        row = lax.broadcasted_iota(jnp.int32, qt.shape, 0)
        zero = jnp.zeros_like(qt)
        rhs_ref[2 * pair] = jnp.where(row < A_HEAD_DIM, qt, zero)
        rhs_ref[2 * pair + 1] = jnp.where(row >= A_HEAD_DIM, qt, zero)

    items = [(q0, hd) for q0 in range(0, ROW_TILE, A_QB) for hd in range(A_HEADS)]

    def score_item(t, dst):
        q0, hd = items[t]
        pair = hd // 2
        half = A_WIN // A_QK_SPLIT
        for r in range(0, A_WIN, half):
            dst[r:r + half, :] = _dot(
                kcat_ref[q0 + r:q0 + r + half, pair * LANES:(pair + 1) * LANES],
                rhs_ref[hd, :, q0:q0 + A_QB]) + bias_ref[hd, r:r + half, :]

    def update_item(t, src, first_tile):
        q0, hd = items[t]
        r_min = BAND_ROWS - q0 if first_tile else 0
        rows = range(0, A_WIN, A_ROWS)
        live = [r for r in rows if r >= r_min]
        m = jnp.max(functools.reduce(jnp.maximum, [src[r:r + A_ROWS, :] for r in live]),
                    axis=0, keepdims=True)
        p = jnp.concatenate(
            [jnp.exp2((src[r:r + A_ROWS, :] - m).astype(BF16)) if r >= r_min
             else jnp.zeros((A_ROWS, A_QB), BF16) for r in rows], axis=0)
        vt1 = jnp.concatenate([vcat_ref[hd * A_HEAD_DIM:(hd + 1) * A_HEAD_DIM, q0:q0 + A_WIN],
                               jnp.ones((A_SUM_ROWS, A_WIN), BF16)], axis=0)
        o = _dot(vt1, p)
        ot_ref[hd * A_HEAD_DIM:(hd + 1) * A_HEAD_DIM, q0:q0 + A_QB] = (
            o[:A_HEAD_DIM] / o[A_HEAD_DIM:A_HEAD_DIM + 1])

    def run(first_tile):
        bufs = (sa_ref, sb_ref, sc_ref)
        depth = len(bufs) - 1
        for t in range(depth):
            score_item(t, bufs[t % len(bufs)])
        for t in range(len(items)):
            if t + depth < len(items):
                score_item(t + depth, bufs[(t + depth) % len(bufs)])
            update_item(t, bufs[t % len(bufs)], first_tile)

    @pl.when(n == 0)
    def _():
        run(True)

    @pl.when(n > 0)
    def _():
        run(False)

    o_ref[...] = (ot_ref[...].T * _silu(g_ref[...].astype(F32))).astype(BF16)


def _attn_a_prompt(qat, ka, vat, ga, bias_rows, *, batch, seq):
    tiles = seq // ROW_TILE
    cur = lambda b, n: (b * tiles + n, 0)
    prev = lambda b, n: (b * tiles + jnp.maximum(n - 1, 0), 0)
    cur_t = lambda b, n: (b * tiles + n, 0, 0)
    prev_t = lambda b, n: (b * tiles + jnp.maximum(n - 1, 0), 0, 0)
    blk = (ROW_TILE, A_WIDTH)
    blk_t = (None, A_WIDTH, ROW_TILE)
    return pl.pallas_call(
        _attn_a_prompt_kernel,
        grid=(batch, tiles),
        in_specs=[
            pl.BlockSpec(blk_t, cur_t),
            pl.BlockSpec(blk, prev), pl.BlockSpec(blk, cur),
            pl.BlockSpec(blk_t, prev_t), pl.BlockSpec(blk_t, cur_t),
            pl.BlockSpec(blk, cur),
            pl.BlockSpec(bias_rows.shape, lambda b, n: (0, 0)),
        ],
        out_specs=pl.BlockSpec(blk, cur),
        out_shape=jax.ShapeDtypeStruct(ka.shape, BF16),
        scratch_shapes=[pltpu.VMEM((2 * ROW_TILE, A_WIDTH), BF16),
                        pltpu.VMEM((A_WIDTH, 2 * ROW_TILE), BF16),
                        pltpu.VMEM((A_HEADS, LANES, ROW_TILE), BF16),
                        pltpu.VMEM((A_HEADS, A_WIN, A_QB), F32),
                        pltpu.VMEM((A_WIDTH, ROW_TILE), F32),
                        pltpu.VMEM((A_WIN, A_QB), F32),
                        pltpu.VMEM((A_WIN, A_QB), F32),
                        pltpu.VMEM((A_WIN, A_QB), F32)],
        compiler_params=pltpu.CompilerParams(
            dimension_semantics=("arbitrary", "arbitrary"), vmem_limit_bytes=VMEM_LIMIT),
        name="attn_a_prompt",
    )(qat, ka, ka, vat, vat, ga, bias_rows)


def _diff_lambda(lq1_ref, lk1_ref, lq2_ref, lk2_ref, lambda_init):
    e1 = jnp.exp(jnp.sum(lq1_ref[...] * lk1_ref[...], axis=-1, keepdims=True))
    e2 = jnp.exp(jnp.sum(lq2_ref[...] * lk2_ref[...], axis=-1, keepdims=True))
    return e1 - e2 + lambda_init


def _stack_maps(qh):
    lane = lax.broadcasted_iota(jnp.int32, qh.shape, 1)
    zero = jnp.zeros_like(qh)
    return jnp.concatenate([jnp.where(lane < B_HEAD_DIM, qh, zero),
                            jnp.where(lane >= B_HEAD_DIM, qh, zero)], axis=0)


def _subnorm_gate(o, subg_ref, gate, lambda_init):
    o = o * lax.rsqrt(jnp.mean(o * o, axis=-1, keepdims=True) + EPS) * subg_ref[...]
    return (o * (1.0 - lambda_init)) * _silu(gate)


def _attn_b_prompt_kernel(qta_ref, qtb_ref, k_ref, vt_ref, ga_ref, gb_ref,
                          lq1_ref, lk1_ref, lq2_ref, lk2_ref, subg_ref, oa_ref, ob_ref,
                          rhs_ref, acc_ref, m_ref, sa_ref, sb_ref, ma_ref, mb_ref, *, lambda_init, tiles):
    g = pl.program_id(2)
    heads = rhs_ref.shape[1]
    for t, qt_ref in enumerate((qta_ref, qtb_ref)):
        for h in range(heads):
            qt = qt_ref[h * B_V_DIM:(h + 1) * B_V_DIM, :]
            row = lax.broadcasted_iota(jnp.int32, qt.shape, 0)
            zero = jnp.zeros_like(qt)
            rhs_ref[t, h, :, :B_TQ] = jnp.where(row < B_HEAD_DIM, qt, zero)
            rhs_ref[t, h, :, B_TQ:] = jnp.where(row >= B_HEAD_DIM, qt, zero)
    m_ref[...] = jnp.full(m_ref.shape, NEG_INF, F32)
    acc_ref[...] = jnp.zeros(acc_ref.shape, F32)

    items = [(h, c) for h in range(heads) for c in range(2 * B_TQ // B_STRIP)]

    def live_keys(c, diagonal):
        q0 = (c * B_STRIP) % B_TQ
        return min(B_TK, q0 + B_STRIP) if diagonal else B_TK

    def score_item(tile, j, n, dst, diagonal):
        h, c = items[n]
        dst_s, dst_m = dst
        part = B_TK // B_QK_SPLIT
        top = None
        for r in range(0, live_keys(c, diagonal), part):
            k0 = pl.multiple_of(j * B_TK + r, part)
            s = _dot(k_ref[pl.ds(k0, part), h * B_V_DIM:(h + 1) * B_V_DIM],
                     rhs_ref[tile, h, :, c * B_STRIP:(c + 1) * B_STRIP])
            if diagonal:
                q0 = (c * B_STRIP) % B_TQ
                kc = (r + lax.broadcasted_iota(jnp.int32, s.shape, 0)) // CHUNK
                qc = (q0 + lax.broadcasted_iota(jnp.int32, s.shape, 1)) // CHUNK
                s = jnp.where(kc <= qc, s, NEG_INF)
            dst_s[n, r:r + part, :] = s
            part_top = jnp.max(s, axis=0, keepdims=True)
            top = part_top if top is None else jnp.maximum(top, part_top)
        dst_m[n] = top

    def update_item(tile, j, n, src, diagonal):
        h, c = items[n]
        src_s, src_m = src
        hs = slice(h * B_V_DIM, (h + 1) * B_V_DIM)
        cols = slice(c * B_STRIP, (c + 1) * B_STRIP)
        live = live_keys(c, diagonal)
        rows = range(0, live, B_ROWS)
        m_old = m_ref[tile, h, :, cols]
        m_new = jnp.maximum(m_old, src_m[n])
        alpha = jnp.exp2(m_old - m_new)
        m_ref[tile, h, :, cols] = m_new
        p = jnp.concatenate(
            [jnp.exp2((src_s[n, r:r + B_ROWS, :] - m_new).astype(BF16)) for r in rows], axis=0)
        vt1 = jnp.concatenate([vt_ref[j, hs, :live], jnp.ones((B_SUM_ROWS, live), BF16)], axis=0)
        acc_ref[tile, h, :, cols] = alpha * acc_ref[tile, h, :, cols] + _dot(vt1, p)

    def stage(score, dst, update, src, score_diagonal=False, update_diagonal=False):
        for n in range(len(items)):
            if score is not None:
                score_item(*score, n, dst, score_diagonal)
            if update is not None:
                update_item(*update, n, src, update_diagonal)

    buf_a = (sa_ref, ma_ref)
    buf_b = (sb_ref, mb_ref)
    first = (0, g)
    last = (1, tiles - 1 - g)

    def at(s):
        in_lo = s <= g
        return jnp.where(in_lo, 0, 1), jnp.where(in_lo, s - 1, s - 1 - g)

    def pair(p, carry):
        s = 2 * p + 1
        stage(at(s + 1), buf_a, at(s), buf_b)
        stage(at(s + 2), buf_b, at(s + 1), buf_a)
        return carry

    stage(first, buf_a, None, None, score_diagonal=True)
    stage(at(1), buf_b, first, buf_a, update_diagonal=True)
    lax.fori_loop(0, (tiles - 2) // 2, pair, 0)
    stage(last, buf_a, at(tiles - 1), buf_b, score_diagonal=True)
    stage(None, None, last, buf_a, update_diagonal=True)

    lam = _diff_lambda(lq1_ref, lk1_ref, lq2_ref, lk2_ref, lambda_init)
    for t, (g_ref, o_ref) in enumerate(((ga_ref, oa_ref), (gb_ref, ob_ref))):
        for h in range(heads):
            hs = slice(h * B_V_DIM, (h + 1) * B_V_DIM)
            o = acc_ref[t, h, :B_V_DIM, :] / acc_ref[t, h, B_V_DIM:B_V_DIM + 1, :]
            o = (o[:, :B_TQ] - lam * o[:, B_TQ:]).T
            o_ref[:, hs] = _subnorm_gate(o, subg_ref, g_ref[:, hs].astype(F32),
                                         lambda_init).astype(BF16)


def _attn_b_prompt(qbt, kb, vbt, gb, lams, subg, *, batch, seq, lambda_init):
    assert B_TQ == B_TK == ROW_TILE and B_HEADS % B_HEADS_PER_STEP == 0
    tiles = seq // B_TQ
    assert tiles % 2 == 0
    half = tiles // 2
    width = B_HEADS_PER_STEP * B_V_DIM
    n_items = B_HEADS_PER_STEP * 2 * B_TQ // B_STRIP
    lo = lambda b, g: b * tiles + g
    hi = lambda b, g: b * tiles + tiles - 1 - g
    vec = pl.BlockSpec((1, B_HEAD_DIM), lambda b, h, g: (0, 0))
    vbt = vbt.reshape(batch, tiles, B_WIDTH, B_TK)
    out = jax.ShapeDtypeStruct((batch * half * B_TQ, B_WIDTH), BF16)
    return pl.pallas_call(
        functools.partial(_attn_b_prompt_kernel, lambda_init=lambda_init, tiles=tiles),
        grid=(batch, B_HEADS // B_HEADS_PER_STEP, half),
        in_specs=[
            pl.BlockSpec((None, width, B_TQ), lambda b, h, g: (lo(b, g), h, 0)),
            pl.BlockSpec((None, width, B_TQ), lambda b, h, g: (hi(b, g), h, 0)),
            pl.BlockSpec((seq, width), lambda b, h, g: (b, h), pipeline_mode=pl.Buffered(1)),
            pl.BlockSpec((None, tiles, width, B_TK), lambda b, h, g: (b, 0, h, 0),
                         pipeline_mode=pl.Buffered(1)),
            pl.BlockSpec((B_TQ, width), lambda b, h, g: (lo(b, g), h)),
            pl.BlockSpec((B_TQ, width), lambda b, h, g: (hi(b, g), h)),
            vec, vec, vec, vec,
            pl.BlockSpec((1, B_V_DIM), lambda b, h, g: (0, 0)),
        ],
        out_specs=[pl.BlockSpec((B_TQ, width), lambda b, h, g: (b * half + g, h)),
                   pl.BlockSpec((B_TQ, width), lambda b, h, g: (b * half + half - 1 - g, h))],
        out_shape=[out, out],
        scratch_shapes=[pltpu.VMEM((2, B_HEADS_PER_STEP, B_V_DIM, 2 * B_TQ), BF16),
                        pltpu.VMEM((2, B_HEADS_PER_STEP, B_V_DIM + B_SUM_ROWS, 2 * B_TQ), F32),
                        pltpu.VMEM((2, B_HEADS_PER_STEP, 1, 2 * B_TQ), F32),
                        pltpu.VMEM((n_items, B_TK, B_STRIP), F32),
                        pltpu.VMEM((n_items, B_TK, B_STRIP), F32),
                        pltpu.VMEM((n_items, 1, B_STRIP), F32),
                        pltpu.VMEM((n_items, 1, B_STRIP), F32)],
        compiler_params=pltpu.CompilerParams(
            dimension_semantics=("arbitrary", "arbitrary", "arbitrary"),
            vmem_limit_bytes=VMEM_LIMIT),
        name="attn_b_prompt",
    )(qbt, qbt, kb, vbt, gb, gb, *lams, subg)


def _roll_in(cache_t, new_rows):
    t = new_rows.shape[0]
    keep = cache_t.shape[1]
    shifted = pltpu.roll(cache_t, keep - t, 1)
    pad = jnp.concatenate([jnp.zeros((LANES - t, new_rows.shape[1]), F32), new_rows], axis=0)
    new_t = pad.T
    lane = lax.broadcasted_iota(jnp.int32, new_t.shape, 1)
    last = jnp.where(lane >= LANES - t, new_t, shifted[:, keep - LANES:])
    return jnp.concatenate([shifted[:, :keep - LANES], last], axis=1)


def _attn_a_sample_kernel(q_ref, kn_ref, vn_ref, knf_ref, vnf_ref, kc_ref, vc_ref, g_ref,
                          brow_ref, o_ref, ko_ref, vo_ref):
    t = q_ref.shape[0]
    keep = kc_ref.shape[1]
    lane = lax.broadcasted_iota(jnp.int32, (t, LANES), 1)
    low_half = lane < A_HEAD_DIM
    for pair in range(A_HEADS // 2):
        cs = slice(pair * LANES, (pair + 1) * LANES)
        qp = q_ref[:, cs]
        kc = kc_ref[cs, :].astype(BF16)
        vc = vc_ref[cs, :].astype(BF16)
        kn = kn_ref[:, cs]
        vn = vn_ref[:, cs]
        outs = []
        for e in range(2):
            hd = 2 * pair + e
            bias = _toeplitz(brow_ref[hd:hd + 1, :], t)
            qm = jnp.where(low_half if e == 0 else ~low_half, qp, jnp.zeros_like(qp))
            s_c = _dot(qm, kc) + bias[:, :keep]
            s_n = _nt_dot(qm, kn) + bias[:, keep:keep + t]
            m = jnp.maximum(jnp.max(s_c, axis=-1, keepdims=True),
                            jnp.max(s_n, axis=-1, keepdims=True))
            p_c = jnp.exp(s_c - m)
            p_n = jnp.exp(s_n - m)
            l = jnp.sum(p_c, axis=-1, keepdims=True) + jnp.sum(p_n, axis=-1, keepdims=True)
            o = _nt_dot(p_c.astype(BF16), vc) + _dot(p_n.astype(BF16), vn)
            outs.append(o / l)
        o_pair = jnp.where(low_half, outs[0], outs[1])
        o_ref[:, cs] = (o_pair * _silu(g_ref[:, cs].astype(F32))).astype(BF16)
    ko_ref[...] = _roll_in(kc_ref[...], knf_ref[...])
    vo_ref[...] = _roll_in(vc_ref[...], vnf_ref[...])


def _attn_a_sample(qa, ka, va, kaf, vaf, cache_kt, cache_vt, ga, bias_rows, *, batch, t):
    keep = cache_kt.shape[2]
    new = pl.BlockSpec((t, A_WIDTH), lambda b: (b, 0))
    cache = pl.BlockSpec((None, A_WIDTH, keep), lambda b: (b, 0, 0))
    return pl.pallas_call(
        _attn_a_sample_kernel,
        grid=(batch,),
        in_specs=[new, new, new, new, new, cache, cache, new,
                  pl.BlockSpec(bias_rows.shape, lambda b: (0, 0))],
        out_specs=[new, cache, cache],
        out_shape=[jax.ShapeDtypeStruct(qa.shape, BF16),
                   jax.ShapeDtypeStruct(cache_kt.shape, F32),
                   jax.ShapeDtypeStruct(cache_vt.shape, F32)],
        compiler_params=pltpu.CompilerParams(
            dimension_semantics=("arbitrary",), vmem_limit_bytes=VMEM_LIMIT),
        name="attn_a_sample",
    )(qa, ka, va, kaf, vaf, cache_kt, cache_vt, ga, bias_rows)


def _attn_b_sample_kernel(q_ref, kn_ref, vn_ref, kc_ref, vc_ref, g_ref,
                          lq1_ref, lk1_ref, lq2_ref, lk2_ref, subg_ref, o_ref, *, lambda_init):
    t = q_ref.shape[0]
    past = kc_ref.shape[1]
    lam = _diff_lambda(lq1_ref, lk1_ref, lq2_ref, lk2_ref, lambda_init)
    for h in range(B_HEADS):
        cs = slice(h * B_V_DIM, (h + 1) * B_V_DIM)
        qs = _stack_maps(q_ref[:, cs])
        s_c = _dot(qs, kc_ref[cs, :].astype(BF16))
        s_n = _nt_dot(qs, kn_ref[:, cs])
        m = jnp.maximum(jnp.max(s_c, axis=-1, keepdims=True),
                        jnp.max(s_n, axis=-1, keepdims=True))
        p_c = jnp.exp(s_c - m)
        p_n = jnp.exp(s_n - m)
        l = jnp.sum(p_c, axis=-1, keepdims=True) + jnp.sum(p_n, axis=-1, keepdims=True)
        p_c = p_c / l
        p_n = p_n / l
        a_c = (p_c[:t] - lam * p_c[t:]).astype(BF16)
        a_n = (p_n[:t] - lam * p_n[t:]).astype(BF16)
        vc = vc_ref[pl.ds(h, past, stride=B_HEADS), :].astype(BF16)
        o = _dot(a_c, vc) + _dot(a_n, vn_ref[:, cs])
        o_ref[:, cs] = _subnorm_gate(o, subg_ref, g_ref[:, cs].astype(F32),
                                     lambda_init).astype(BF16)


def _attn_b_sample(qb, kb, vb, cache_kt, cache_v, gb, lams, subg, *, batch, t, lambda_init):
    past = cache_kt.shape[2]
    new = pl.BlockSpec((t, B_WIDTH), lambda b: (b, 0))
    vec = pl.BlockSpec((1, B_HEAD_DIM), lambda b: (0, 0))
    return pl.pallas_call(
        functools.partial(_attn_b_sample_kernel, lambda_init=lambda_init),
        grid=(batch,),
        in_specs=[new, new, new,
                  pl.BlockSpec((None, B_WIDTH, past), lambda b: (b, 0, 0)),
                  pl.BlockSpec((None, past * B_HEADS, B_V_DIM), lambda b: (b, 0, 0)),
                  new, vec, vec, vec, vec,
                  pl.BlockSpec((1, B_V_DIM), lambda b: (0, 0))],
        out_specs=new,
        out_shape=jax.ShapeDtypeStruct(qb.shape, BF16),
        compiler_params=pltpu.CompilerParams(
            dimension_semantics=("arbitrary",), vmem_limit_bytes=VMEM_LIMIT),
        name="attn_b_sample",
    )(qb, kb, vb, cache_kt, cache_v, gb, *lams, subg)


def _out_kernel(oa_ref, *refs, final_norm, steps_per_seq):
    *ob_refs, wa_ref, wb_ref, x_ref, fg_ref, y_ref = refs
    if len(ob_refs) == 2:
        in_lo = pl.program_id(0) % steps_per_seq < steps_per_seq // 2
        ob = jnp.where(in_lo, ob_refs[0][...], ob_refs[1][...])
    else:
        ob = ob_refs[0][...]
    y = x_ref[...] + _dot(oa_ref[...], wa_ref[...]) + _dot(ob, wb_ref[...])
    if final_norm:
        y = y * lax.rsqrt(jnp.mean(y * y, axis=-1, keepdims=True) + EPS) * fg_ref[...]
    y_ref[...] = y


def _out_proj(oa, ob_parts, w_a, w_b, x, final_gain, *, tm, final_norm, steps_per_seq=1):
    rows, d_model = x.shape
    row_blk = lambda i: (i, 0)
    const2 = lambda i: (0, 0)
    if len(ob_parts) == 2:
        assert steps_per_seq % 2 == 0
        hs = steps_per_seq // 2
        seq_of = lambda i: i // steps_per_seq
        step_of = lambda i: i % steps_per_seq
        ob_specs = [
            pl.BlockSpec((tm, B_WIDTH),
                         lambda i: (seq_of(i) * hs + jnp.minimum(step_of(i), hs - 1), 0)),
            pl.BlockSpec((tm, B_WIDTH),
                         lambda i: (seq_of(i) * hs + jnp.maximum(step_of(i) - hs, 0), 0)),
        ]
    else:
        ob_specs = [pl.BlockSpec((tm, B_WIDTH), row_blk)]
    return pl.pallas_call(
        functools.partial(_out_kernel, final_norm=final_norm, steps_per_seq=steps_per_seq),
        grid=(rows // tm,),
        in_specs=[
            pl.BlockSpec((tm, A_WIDTH), row_blk),
            *ob_specs,
            pl.BlockSpec(w_a.shape, const2),
            pl.BlockSpec(w_b.shape, const2),
            pl.BlockSpec((tm, d_model), row_blk),
            pl.BlockSpec((1, d_model), const2),
        ],
        out_specs=pl.BlockSpec((tm, d_model), row_blk),
        out_shape=jax.ShapeDtypeStruct(x.shape, F32),
        compiler_params=pltpu.CompilerParams(
            dimension_semantics=("arbitrary",), vmem_limit_bytes=VMEM_LIMIT),
        name="out_proj",
    )(oa, *ob_parts, w_a, w_b, x, final_gain)


def _channel_major(a):
    n, pos = a.shape[:2]
    return jnp.moveaxis(a.reshape(n, pos, -1), 1, 2)


def _position_major(a, channel_dims):
    n, _, pos = a.shape
    return jnp.moveaxis(a, 2, 1).reshape(n, pos, *channel_dims)


def kernel(x_prompt, x_sample, cache_a_k, cache_a_v, cache_b_k, cache_b_v,
           norm_gain, w_in, w_out, rel_bias, lambda_q1, lambda_k1, lambda_q2, lambda_k2,
           subln_gain, final_gain):
    batch, seq, d_model = x_prompt.shape
    dec_batch, t_sample, _ = x_sample.shape
    depth = w_in.shape[0]
    past_len = cache_b_k.shape[2]
    a_keep = cache_a_k.shape[2]
    keep_prompt = min(BAND_ROWS, seq)
    assert seq % ROW_TILE == 0 and keep_prompt == ROW_TILE and seq % B_TQ == 0
    assert t_sample <= a_keep and t_sample <= LANES and d_model == 2 * SEG

    rows_p = batch * seq
    rows_s = dec_batch * t_sample
    tiles_per_seq = seq // ROW_TILE
    tables_p = _rope_tables(np.arange(ROW_TILE), np.arange(tiles_per_seq) * ROW_TILE)
    tables_s = _rope_tables(past_len + np.arange(rows_s) % t_sample, np.zeros(1))
    a_dims = (A_HEADS, A_HEAD_DIM)
    bk_dims = (B_HEADS, 2, B_HEAD_DIM)
    bv_dims = (B_HEADS, B_V_DIM)

    yp = x_prompt.reshape(rows_p, d_model)
    ys = x_sample.reshape(rows_s, d_model)
    fg = final_gain.reshape(1, d_model)
    outs = [[] for _ in range(8)]
    for l in range(depth):
        lambda_init = 0.8 - 0.6 * math.exp(-0.3 * l)
        last = l == depth - 1
        w_l = w_in[l].astype(BF16)
        w_a = w_out[l, :A_WIDTH].astype(BF16)
        w_b = w_out[l, A_WIDTH:].astype(BF16)
        gain = norm_gain[l].reshape(1, d_model)
        lams = [a[l].reshape(1, B_HEAD_DIM).astype(F32)
                for a in (lambda_q1, lambda_k1, lambda_q2, lambda_k2)]
        subg = subln_gain[l].reshape(1, B_V_DIM).astype(F32)

        qa, ka, va, ga, qb, kb, vb, gb, kaf, vaf, kbf, vbf = _proj(
            yp, gain, w_l, tables_p, tm=ROW_TILE, tiles_per_seq=tiles_per_seq,
            tail_every=tiles_per_seq, channel_major=True, q_scale=LOG2E * B_HEAD_DIM ** -0.5)
        bias_p = _bias_rows(rel_bias[l], 1 - A_WIN, A_QB - 1, lambda t: t + BAND_ROWS, LOG2E)
        oa = _attn_a_prompt(qa, ka, va, ga, bias_p, batch=batch, seq=seq)
        ob = _attn_b_prompt(qb, kb, vb, gb, lams, subg, batch=batch, seq=seq,
                            lambda_init=lambda_init)
        yp = _out_proj(oa, ob, w_a, w_b, yp, fg, tm=OUT_ROW_TILE, final_norm=last,
                       steps_per_seq=seq // OUT_ROW_TILE)
        outs[0].append(_position_major(kaf, a_dims))
        outs[1].append(_position_major(vaf, a_dims))
        outs[2].append(_position_major(kbf, bk_dims))
        outs[3].append(vbf.reshape(batch, seq, *bv_dims))

        qa, ka, va, ga, qb, kb, vb, gb, kaf, vaf, kbf, vbf = _proj(
            ys, gain, w_l, tables_s, tm=rows_s, tiles_per_seq=1, tail_every=1,
            channel_major=False, q_scale=B_HEAD_DIM ** -0.5)
        oa, ak_new, av_new = _attn_a_sample(
            qa, ka, va, kaf, vaf, _channel_major(cache_a_k[l]), _channel_major(cache_a_v[l]),
            ga, _bias_rows(rel_bias[l], 1 - t_sample, a_keep + t_sample - 1, lambda t: a_keep - t),
            batch=dec_batch, t=t_sample)
        ob = _attn_b_sample(
            qb, kb, vb, _channel_major(cache_b_k[l]),
            cache_b_v[l].reshape(dec_batch, past_len * B_HEADS, B_V_DIM),
            gb, lams, subg, batch=dec_batch, t=t_sample, lambda_init=lambda_init)
        ys = _out_proj(oa, (ob,), w_a, w_b, ys, fg, tm=rows_s, final_norm=last)
        outs[4].append(_position_major(ak_new, a_dims))
        outs[5].append(_position_major(av_new, a_dims))
        outs[6].append(kbf.reshape(dec_batch, t_sample, *bk_dims))
        outs[7].append(vbf.reshape(dec_batch, t_sample, *bv_dims))

    return (yp.reshape(batch, seq, d_model), ys.reshape(dec_batch, t_sample, d_model),
            *[jnp.stack(o) for o in outs])
```

```python
import functools
import math

import numpy as np
import jax
import jax.numpy as jnp
from jax import lax
from jax.experimental import pallas as pl
from jax.experimental.pallas import tpu as pltpu

F32 = jnp.float32
BF16 = jnp.bfloat16
NEG_INF = float("-inf")

CHUNK = 64
N_PREV_CHUNKS = 8
A_HEADS = 8
A_HEAD_DIM = 64
A_WIDTH = A_HEADS * A_HEAD_DIM
B_HEADS = 4
B_HEAD_DIM = 64
B_V_DIM = 2 * B_HEAD_DIM
B_WIDTH = B_HEADS * B_V_DIM
SEG = 512
MAX_REL = 128
ROPE_THETA = 500000.0
ROPE_DIM = 16
EPS = 1e-6
LOG2E = math.log2(math.e)
LANES = 128
VMEM_LIMIT = 56 * 1024 * 1024
ROW_TILE = 512
OUT_ROW_TILE = 1024
BAND_ROWS = N_PREV_CHUNKS * CHUNK
A_QB = 256
A_WIN = BAND_ROWS + A_QB
A_ROWS = 64
A_QK_SPLIT = 3
A_SUM_ROWS = 16
B_TQ = 512
B_TK = 512
B_HEADS_PER_STEP = 4
B_STRIP = 256
B_ROWS = 64
B_QK_SPLIT = 2
B_SUM_ROWS = 16

NT_DIMS = (((1,), (1,)), ((), ()))


def _nt_dot(a, b):
    return lax.dot_general(a, b, NT_DIMS, preferred_element_type=F32)


def _dot(a, b):
    return jnp.dot(a, b, preferred_element_type=F32)


def _silu(g):
    return g / (1.0 + jnp.exp(-g))


def _round_up(n, m):
    return -(-n // m) * m


def _rope_lane_freq():
    d = np.arange(LANES) % B_HEAD_DIM
    inv = ROPE_THETA ** (-np.arange(0, ROPE_DIM, 2, dtype=np.float64) / ROPE_DIM)
    return np.where(d < ROPE_DIM, inv[d % (ROPE_DIM // 2)], 0.0)


def _rope_tables(row_pos, base_pos):
    f = _rope_lane_freq()[None, :]
    ar = np.asarray(row_pos, np.float64)[:, None] * f
    ab = np.asarray(base_pos, np.float64)[:, None] * f
    as32 = lambda a: jnp.asarray(a.astype(np.float32))
    return (as32(np.cos(ar)), as32(np.sin(ar)),
            as32(np.cos(ab))[:, None, :], as32(np.sin(ab))[:, None, :])


def _bias_rows(rel_bias, t_min, t_max, rel_of_t, scale=1.0):
    width = _round_up(t_max - t_min + 1, LANES)
    t = np.arange(width)
    t = np.where(t <= t_max, t, t - width)
    idx = np.clip(rel_of_t(t), -MAX_REL, MAX_REL) + MAX_REL
    return rel_bias.astype(F32)[:, idx] * scale


def _toeplitz(row, n_rows):
    return pltpu.roll(jnp.broadcast_to(row, (n_rows, row.shape[-1])), 0, 1, stride=1, stride_axis=0)


def _proj_kernel(x_ref, g_ref, w_ref, cr_ref, sr_ref, cb_ref, sb_ref, *out_refs,
                 channel_major, q_scale):
    (qa_ref, ka_ref, va_ref, ga_ref, qb_ref, kb_ref, vb_ref, gb_ref,
     kaf_ref, vaf_ref, kbf_ref, vbf_ref) = out_refs
    x = x_ref[...]
    inv = lax.rsqrt(jnp.mean(x * x, axis=-1, keepdims=True) + EPS)
    h = (x * inv * g_ref[...]).astype(BF16)

    def seg(k):
        return _dot(h, w_ref[:, k * SEG:(k + 1) * SEG])

    cb, sb = cb_ref[0], sb_ref[0]
    cr, sr = cr_ref[...], sr_ref[...]
    cos = cb * cr - sb * sr
    sin = sb * cr + cb * sr
    d = lax.broadcasted_iota(jnp.int32, cos.shape, 1) % B_HEAD_DIM
    sin_lo = jnp.where(d < ROPE_DIM // 2, -sin, 0.0)
    sin_hi = jnp.where(d >= ROPE_DIM // 2, sin, 0.0)

    def rope(z):
        cols = []
        for c in range(SEG // LANES):
            zc = z[:, c * LANES:(c + 1) * LANES]
            up = pltpu.roll(zc, LANES - ROPE_DIM // 2, 1)
            dn = pltpu.roll(zc, ROPE_DIM // 2, 1)
            cols.append(zc * cos + up * sin_lo + dn * sin_hi)
        return jnp.concatenate(cols, axis=1)

    qa = seg(0) * q_scale
    qa_ref[...] = (qa.T if channel_major else qa).astype(BF16)
    ka = seg(1)
    ka_ref[...] = ka.astype(BF16)
    va = seg(2)
    va_t = va.T if channel_major else va
    va_ref[...] = va_t.astype(BF16)
    kaf_ref[...] = ka.T if channel_major else ka
    vaf_ref[...] = va_t

    ga_ref[...] = seg(3).astype(BF16)
    qb = rope(seg(4)) * q_scale
    qb_ref[...] = (qb.T if channel_major else qb).astype(BF16)
    kb = rope(seg(5))
    kbf_ref[...] = kb.T if channel_major else kb
    kb_ref[...] = kb.astype(BF16)
    vb = seg(6)
    n_col = SEG // LANES
    for c in range(n_col):
        vbf_ref[pl.ds(c, vb.shape[0], stride=n_col), :] = vb[:, c * LANES:(c + 1) * LANES]
    vb_ref[...] = (vb.T if channel_major else vb).astype(BF16)
    gb_ref[...] = seg(7).astype(BF16)


def _proj(x, gain, w_bf16, tables, *, tm, tiles_per_seq, tail_every, channel_major, q_scale):
    rows, d_model = x.shape
    n_tiles = rows // tm
    n_tail = n_tiles // tail_every
    n_seq = n_tiles // tiles_per_seq
    cr, sr, cb, sb = tables
    row_blk = lambda i: (i, 0)
    const2 = lambda i: (0, 0)
    base_blk = lambda i: (i % tiles_per_seq, 0, 0)
    bf = jax.ShapeDtypeStruct((rows, SEG), BF16)
    f32_full = jax.ShapeDtypeStruct((rows, SEG), F32)
    full_spec = pl.BlockSpec((tm, SEG), row_blk)
    if channel_major:
        assert tm == SEG
        tail = jax.ShapeDtypeStruct((n_tail, SEG, tm), F32)
        tail_spec = pl.BlockSpec((None, SEG, tm), lambda i: (i // tail_every, 0, 0))
        kbf = jax.ShapeDtypeStruct((n_seq, SEG, tiles_per_seq * tm), F32)
        kbf_spec = pl.BlockSpec((None, SEG, tm), lambda i: (i // tiles_per_seq, 0, i % tiles_per_seq))
        bf_t = jax.ShapeDtypeStruct((n_tiles, SEG, tm), BF16)
        bf_t_spec = pl.BlockSpec((None, SEG, tm), lambda i: (i, 0, 0))
    else:
        tail = jax.ShapeDtypeStruct((n_tail * tm, SEG), F32)
        tail_spec = pl.BlockSpec((tm, SEG), lambda i: (i // tail_every, 0))
        kbf, kbf_spec = f32_full, full_spec
        bf_t, bf_t_spec = bf, full_spec
    return pl.pallas_call(
        functools.partial(_proj_kernel, channel_major=channel_major,
                          q_scale=q_scale),
        grid=(n_tiles,),
        in_specs=[
            pl.BlockSpec((tm, d_model), row_blk),
            pl.BlockSpec((1, d_model), const2),
            pl.BlockSpec(w_bf16.shape, const2),
            pl.BlockSpec((tm, LANES), const2),
            pl.BlockSpec((tm, LANES), const2),
            pl.BlockSpec((1, 1, LANES), base_blk),
            pl.BlockSpec((1, 1, LANES), base_blk),
        ],
        out_specs=([bf_t_spec, full_spec, bf_t_spec, full_spec] * 2
                   + [tail_spec, tail_spec, kbf_spec,
                      pl.BlockSpec((tm * SEG // LANES, LANES), row_blk)]),
        out_shape=([bf_t, bf, bf_t, bf] * 2
                   + [tail, tail, kbf, jax.ShapeDtypeStruct((rows * SEG // LANES, LANES), F32)]),
        compiler_params=pltpu.CompilerParams(
            dimension_semantics=("arbitrary",), vmem_limit_bytes=VMEM_LIMIT),
        name="proj",
    )(x, gain, w_bf16, cr, sr, cb, sb)


def _attn_a_prompt_kernel(qt_ref, kp_ref, kc_ref, vtp_ref, vtc_ref, g_ref, brow_ref, o_ref,
                          bias_ref, sa_ref, sb_ref, sc_ref):
    n = pl.program_id(1)

    @pl.when((pl.program_id(0) == 0) & (n == 0))
    def _():
        j = lax.broadcasted_iota(jnp.int32, (A_WIN, A_QB), 0)
        i = lax.broadcasted_iota(jnp.int32, (A_WIN, A_QB), 1)
        gap = i // CHUNK + N_PREV_CHUNKS - j // CHUNK
        band = (gap >= 0) & (gap <= N_PREV_CHUNKS)
        for hd in range(A_HEADS):
            t = _toeplitz(brow_ref[hd:hd + 1, :], A_WIN)[:, :A_QB]
            bias_ref[hd] = jnp.where(band, t, NEG_INF)

    items = [(q0, hd) for q0 in range(0, ROW_TILE, A_QB) for hd in range(A_HEADS)]
    part = A_WIN // A_QK_SPLIT

    def window(q0, first_tile):
        pieces = []
        for r in range(0, A_WIN, part):
            pos = q0 - BAND_ROWS + r
            if pos >= 0:
                pieces.append((1, pos, r))
            elif not first_tile:
                pieces.append((0, pos + ROW_TILE, r))
        return pieces

    def score_item(t, dst, first_tile):
        q0, hd = items[t]
        lanes = slice(hd // 2 * LANES, (hd // 2 + 1) * LANES)
        qt = qt_ref[lanes, q0:q0 + A_QB]
        row = lax.broadcasted_iota(jnp.int32, qt.shape, 0)
        rhs = jnp.where((row >= A_HEAD_DIM) == bool(hd % 2), qt, jnp.zeros_like(qt))
        for which, start, r in window(q0, first_tile):
            k_ref = (kp_ref, kc_ref)[which]
            dst[r:r + part, :] = (_dot(k_ref[start:start + part, lanes], rhs)
                                  + bias_ref[hd, r:r + part, :])

    def update_item(t, src, first_tile):
        q0, hd = items[t]
        pieces = window(q0, first_tile)
        chunks = [r + c for _, _, r in pieces for c in range(0, part, A_ROWS)]
        m = jnp.max(functools.reduce(jnp.maximum, [src[r:r + A_ROWS, :] for r in chunks]),
                    axis=0, keepdims=True)
        o = None
        for which, start, r in pieces:
            p = jnp.concatenate([jnp.exp2((src[r + c:r + c + A_ROWS, :] - m).astype(BF16))
                                 for c in range(0, part, A_ROWS)], axis=0)
            vt_ref = (vtp_ref, vtc_ref)[which]
            vt1 = jnp.concatenate(
                [vt_ref[hd * A_HEAD_DIM:(hd + 1) * A_HEAD_DIM, start:start + part],
                 jnp.ones((A_SUM_ROWS, part), BF16)], axis=0)
            contrib = _dot(vt1, p)
            o = contrib if o is None else o + contrib
        return o[:A_HEAD_DIM] / o[A_HEAD_DIM:A_HEAD_DIM + 1]

    def run(first_tile):
        bufs = (sa_ref, sb_ref, sc_ref)
        depth = len(bufs) - 1
        for t in range(depth):
            score_item(t, bufs[t % len(bufs)], first_tile)
        o_even = None
        for t, (q0, hd) in enumerate(items):
            if t + depth < len(items):
                score_item(t + depth, bufs[(t + depth) % len(bufs)], first_tile)
            o_head = update_item(t, bufs[t % len(bufs)], first_tile)
            if hd % 2 == 0:
                o_even = o_head
            else:
                lanes = slice(hd // 2 * LANES, (hd // 2 + 1) * LANES)
                o_pair = jnp.concatenate([o_even, o_head], axis=0).T
                gate = g_ref[q0:q0 + A_QB, lanes].astype(F32)
                o_ref[q0:q0 + A_QB, lanes] = (o_pair * _silu(gate)).astype(BF16)

    @pl.when(n == 0)
    def _():
        run(True)

    @pl.when(n > 0)
    def _():
        run(False)


def _attn_a_prompt(qat, ka, vat, ga, bias_rows, *, batch, seq):
    tiles = seq // ROW_TILE
    cur = lambda b, n: (b * tiles + n, 0)
    prev = lambda b, n: (b * tiles + jnp.maximum(n - 1, 0), 0)
    cur_t = lambda b, n: (b * tiles + n, 0, 0)
    prev_t = lambda b, n: (b * tiles + jnp.maximum(n - 1, 0), 0, 0)
    blk = (ROW_TILE, A_WIDTH)
    blk_t = (None, A_WIDTH, ROW_TILE)
    return pl.pallas_call(
        _attn_a_prompt_kernel,
        grid=(batch, tiles),
        in_specs=[
            pl.BlockSpec(blk_t, cur_t),
            pl.BlockSpec(blk, prev), pl.BlockSpec(blk, cur),
            pl.BlockSpec(blk_t, prev_t), pl.BlockSpec(blk_t, cur_t),
            pl.BlockSpec(blk, cur),
            pl.BlockSpec(bias_rows.shape, lambda b, n: (0, 0)),
        ],
        out_specs=pl.BlockSpec(blk, cur),
        out_shape=jax.ShapeDtypeStruct(ka.shape, BF16),
        scratch_shapes=[pltpu.VMEM((A_HEADS, A_WIN, A_QB), F32),
                        pltpu.VMEM((A_WIN, A_QB), F32),
                        pltpu.VMEM((A_WIN, A_QB), F32),
                        pltpu.VMEM((A_WIN, A_QB), F32)],
        compiler_params=pltpu.CompilerParams(
            dimension_semantics=("arbitrary", "arbitrary"), vmem_limit_bytes=VMEM_LIMIT),
        name="attn_a_prompt",
    )(qat, ka, ka, vat, vat, ga, bias_rows)


def _diff_lambda(lq1_ref, lk1_ref, lq2_ref, lk2_ref, lambda_init):
    e1 = jnp.exp(jnp.sum(lq1_ref[...] * lk1_ref[...], axis=-1, keepdims=True))
    e2 = jnp.exp(jnp.sum(lq2_ref[...] * lk2_ref[...], axis=-1, keepdims=True))
    return e1 - e2 + lambda_init


def _stack_maps(qh):
    lane = lax.broadcasted_iota(jnp.int32, qh.shape, 1)
    zero = jnp.zeros_like(qh)
    return jnp.concatenate([jnp.where(lane < B_HEAD_DIM, qh, zero),
                            jnp.where(lane >= B_HEAD_DIM, qh, zero)], axis=0)


def _subnorm_gate(o, subg_ref, gate, lambda_init):
    o = o * lax.rsqrt(jnp.mean(o * o, axis=-1, keepdims=True) + EPS) * subg_ref[...]
    return (o * (1.0 - lambda_init)) * _silu(gate)


def _attn_b_prompt_kernel(qta_ref, qtb_ref, k_ref, vt_ref, ga_ref, gb_ref,
                          lq1_ref, lk1_ref, lq2_ref, lk2_ref, subg_ref, oa_ref, ob_ref,
                          rhs_ref, acc_ref, m_ref, sa_ref, sb_ref, *, lambda_init, tiles):
    g = pl.program_id(2)
    heads = rhs_ref.shape[1]
    for t, qt_ref in enumerate((qta_ref, qtb_ref)):
        for h in range(heads):
            qt = qt_ref[h * B_V_DIM:(h + 1) * B_V_DIM, :]
            row = lax.broadcasted_iota(jnp.int32, qt.shape, 0)
            zero = jnp.zeros_like(qt)
            rhs_ref[t, h, :, :B_TQ] = jnp.where(row < B_HEAD_DIM, qt, zero)
            rhs_ref[t, h, :, B_TQ:] = jnp.where(row >= B_HEAD_DIM, qt, zero)
    m_ref[...] = jnp.full(m_ref.shape, NEG_INF, F32)
    acc_ref[...] = jnp.zeros(acc_ref.shape, F32)

    items = [(h, c) for h in range(heads) for c in range(2 * B_TQ // B_STRIP)]

    def live_keys(c, diagonal):
        q0 = (c * B_STRIP) % B_TQ
        return min(B_TK, q0 + B_STRIP) if diagonal else B_TK

    def score_item(tile, j, n, dst, diagonal):
        h, c = items[n]
        part = B_TK // B_QK_SPLIT
        for r in range(0, live_keys(c, diagonal), part):
            k0 = pl.multiple_of(j * B_TK + r, part)
            dst[n, r:r + part, :] = _dot(k_ref[pl.ds(k0, part), h * B_V_DIM:(h + 1) * B_V_DIM],
                                         rhs_ref[tile, h, :, c * B_STRIP:(c + 1) * B_STRIP])

    def update_item(tile, j, n, src, diagonal):
        h, c = items[n]
        hs = slice(h * B_V_DIM, (h + 1) * B_V_DIM)
        cols = slice(c * B_STRIP, (c + 1) * B_STRIP)
        live = live_keys(c, diagonal)

        def chunk(r):
            s = src[n, r:r + B_ROWS, :]
            if diagonal:
                q0 = (c * B_STRIP) % B_TQ
                kc = (r + lax.broadcasted_iota(jnp.int32, s.shape, 0)) // CHUNK
                qc = (q0 + lax.broadcasted_iota(jnp.int32, s.shape, 1)) // CHUNK
                s = jnp.where(kc <= qc, s, NEG_INF)
            return s

        rows = range(0, live, B_ROWS)
        m_blk = functools.reduce(jnp.maximum, [chunk(r) for r in rows])
        m_old = m_ref[tile, h, :, cols]
        m_new = jnp.maximum(m_old, jnp.max(m_blk, axis=0, keepdims=True))
        alpha = jnp.exp2(m_old - m_new)
        m_ref[tile, h, :, cols] = m_new
        p = jnp.concatenate([jnp.exp2((chunk(r) - m_new).astype(BF16)) for r in rows], axis=0)
        vt1 = jnp.concatenate([vt_ref[j, hs, :live], jnp.ones((B_SUM_ROWS, live), BF16)], axis=0)
        acc_ref[tile, h, :, cols] = alpha * acc_ref[tile, h, :, cols] + _dot(vt1, p)

    def stage(score, dst, update, src, score_diagonal=False, update_diagonal=False):
        for n in range(len(items)):
            if score is not None:
                score_item(*score, n, dst, score_diagonal)
            if update is not None:
                update_item(*update, n, src, update_diagonal)

    buf_a, buf_b = sa_ref, sb_ref
    first = (0, g)
    last = (1, tiles - 1 - g)

    def at(s):
        in_lo = s <= g
        return jnp.where(in_lo, 0, 1), jnp.where(in_lo, s - 1, s - 1 - g)

    def pair(p, carry):
        s = 2 * p + 1
        stage(at(s + 1), buf_a, at(s), buf_b)
        stage(at(s + 2), buf_b, at(s + 1), buf_a)
        return carry

    stage(first, buf_a, None, None, score_diagonal=True)
    stage(at(1), buf_b, first, buf_a, update_diagonal=True)
    lax.fori_loop(0, (tiles - 2) // 2, pair, 0)
    stage(last, buf_a, at(tiles - 1), buf_b, score_diagonal=True)
    stage(None, None, last, buf_a, update_diagonal=True)

    lam = _diff_lambda(lq1_ref, lk1_ref, lq2_ref, lk2_ref, lambda_init)
    for t, (g_ref, o_ref) in enumerate(((ga_ref, oa_ref), (gb_ref, ob_ref))):
        for h in range(heads):
            hs = slice(h * B_V_DIM, (h + 1) * B_V_DIM)
            o = acc_ref[t, h, :B_V_DIM, :] / acc_ref[t, h, B_V_DIM:B_V_DIM + 1, :]
            o = (o[:, :B_TQ] - lam * o[:, B_TQ:]).T
            o_ref[:, hs] = _subnorm_gate(o, subg_ref, g_ref[:, hs].astype(F32),
                                         lambda_init).astype(BF16)


def _attn_b_prompt(qbt, kb, vbt, gb, lams, subg, *, batch, seq, lambda_init):
    assert B_TQ == B_TK == ROW_TILE and B_HEADS % B_HEADS_PER_STEP == 0
    tiles = seq // B_TQ
    assert tiles % 2 == 0
    half = tiles // 2
    width = B_HEADS_PER_STEP * B_V_DIM
    n_items = B_HEADS_PER_STEP * 2 * B_TQ // B_STRIP
    lo = lambda b, g: b * tiles + g
    hi = lambda b, g: b * tiles + tiles - 1 - g
    vec = pl.BlockSpec((1, B_HEAD_DIM), lambda b, h, g: (0, 0))
    vbt = vbt.reshape(batch, tiles, B_WIDTH, B_TK)
    out = jax.ShapeDtypeStruct((batch * half * B_TQ, B_WIDTH), BF16)
    return pl.pallas_call(
        functools.partial(_attn_b_prompt_kernel, lambda_init=lambda_init, tiles=tiles),
        grid=(batch, B_HEADS // B_HEADS_PER_STEP, half),
        in_specs=[
            pl.BlockSpec((None, width, B_TQ), lambda b, h, g: (lo(b, g), h, 0)),
            pl.BlockSpec((None, width, B_TQ), lambda b, h, g: (hi(b, g), h, 0)),
            pl.BlockSpec((seq, width), lambda b, h, g: (b, h), pipeline_mode=pl.Buffered(1)),
            pl.BlockSpec((None, tiles, width, B_TK), lambda b, h, g: (b, 0, h, 0),
                         pipeline_mode=pl.Buffered(1)),
            pl.BlockSpec((B_TQ, width), lambda b, h, g: (lo(b, g), h)),
            pl.BlockSpec((B_TQ, width), lambda b, h, g: (hi(b, g), h)),
            vec, vec, vec, vec,
            pl.BlockSpec((1, B_V_DIM), lambda b, h, g: (0, 0)),
        ],
        out_specs=[pl.BlockSpec((B_TQ, width), lambda b, h, g: (b * half + g, h)),
                   pl.BlockSpec((B_TQ, width), lambda b, h, g: (b * half + half - 1 - g, h))],
        out_shape=[out, out],
        scratch_shapes=[pltpu.VMEM((2, B_HEADS_PER_STEP, B_V_DIM, 2 * B_TQ), BF16),
                        pltpu.VMEM((2, B_HEADS_PER_STEP, B_V_DIM + B_SUM_ROWS, 2 * B_TQ), F32),
                        pltpu.VMEM((2, B_HEADS_PER_STEP, 1, 2 * B_TQ), F32),
                        pltpu.VMEM((n_items, B_TK, B_STRIP), F32),
                        pltpu.VMEM((n_items, B_TK, B_STRIP), F32)],
        compiler_params=pltpu.CompilerParams(
            dimension_semantics=("arbitrary", "arbitrary", "arbitrary"),
            vmem_limit_bytes=VMEM_LIMIT),
        name="attn_b_prompt",
    )(qbt, qbt, kb, vbt, gb, gb, *lams, subg)


def _roll_in(cache_t, new_rows):
    t = new_rows.shape[0]
    keep = cache_t.shape[1]
    shifted = pltpu.roll(cache_t, keep - t, 1)
    pad = jnp.concatenate([jnp.zeros((LANES - t, new_rows.shape[1]), F32), new_rows], axis=0)
    new_t = pad.T
    lane = lax.broadcasted_iota(jnp.int32, new_t.shape, 1)
    last = jnp.where(lane >= LANES - t, new_t, shifted[:, keep - LANES:])
    return jnp.concatenate([shifted[:, :keep - LANES], last], axis=1)


def _attn_a_sample_kernel(q_ref, kn_ref, vn_ref, knf_ref, vnf_ref, kc_ref, vc_ref, g_ref,
                          brow_ref, o_ref, ko_ref, vo_ref):
    t = q_ref.shape[0]
    keep = kc_ref.shape[1]
    lane = lax.broadcasted_iota(jnp.int32, (t, LANES), 1)
    low_half = lane < A_HEAD_DIM
    pair_cols = [slice(pair * LANES, (pair + 1) * LANES) for pair in range(A_HEADS // 2)]
    scores = []
    for hd in range(A_HEADS):
        cs = pair_cols[hd // 2]
        qp = q_ref[:, cs]
        bias = _toeplitz(brow_ref[hd:hd + 1, :], t)
        qm = jnp.where(low_half if hd % 2 == 0 else ~low_half, qp, jnp.zeros_like(qp))
        scores.append((_dot(qm, kc_ref[cs, :].astype(BF16)) + bias[:, :keep],
                       _nt_dot(qm, kn_ref[:, cs]) + bias[:, keep:keep + t]))
    probs = []
    for s_c, s_n in scores:
        m = jnp.maximum(jnp.max(s_c, axis=-1, keepdims=True), jnp.max(s_n, axis=-1, keepdims=True))
        p_c = jnp.exp(s_c - m)
        p_n = jnp.exp(s_n - m)
        l = jnp.sum(p_c, axis=-1, keepdims=True) + jnp.sum(p_n, axis=-1, keepdims=True)
        probs.append((p_c.astype(BF16), p_n.astype(BF16), l))
    outs = []
    for hd, (p_c, p_n, l) in enumerate(probs):
        cs = pair_cols[hd // 2]
        o = _nt_dot(p_c, vc_ref[cs, :].astype(BF16)) + _dot(p_n, vn_ref[:, cs])
        outs.append(o / l)
    for pair, cs in enumerate(pair_cols):
        o_pair = jnp.where(low_half, outs[2 * pair], outs[2 * pair + 1])
        o_ref[:, cs] = (o_pair * _silu(g_ref[:, cs].astype(F32))).astype(BF16)
    ko_ref[...] = _roll_in(kc_ref[...], knf_ref[...])
    vo_ref[...] = _roll_in(vc_ref[...], vnf_ref[...])


def _attn_a_sample(qa, ka, va, kaf, vaf, cache_kt, cache_vt, ga, bias_rows, *, batch, t):
    keep = cache_kt.shape[2]
    new = pl.BlockSpec((t, A_WIDTH), lambda b: (b, 0))
    cache = pl.BlockSpec((None, A_WIDTH, keep), lambda b: (b, 0, 0))
    return pl.pallas_call(
        _attn_a_sample_kernel,
        grid=(batch,),
        in_specs=[new, new, new, new, new, cache, cache, new,
                  pl.BlockSpec(bias_rows.shape, lambda b: (0, 0))],
        out_specs=[new, cache, cache],
        out_shape=[jax.ShapeDtypeStruct(qa.shape, BF16),
                   jax.ShapeDtypeStruct(cache_kt.shape, F32),
                   jax.ShapeDtypeStruct(cache_vt.shape, F32)],
        compiler_params=pltpu.CompilerParams(
            dimension_semantics=("arbitrary",), vmem_limit_bytes=VMEM_LIMIT),
        name="attn_a_sample",
    )(qa, ka, va, kaf, vaf, cache_kt, cache_vt, ga, bias_rows)


def _attn_b_sample_kernel(q_ref, kn_ref, vn_ref, kc_ref, vc_ref, g_ref,
                          lq1_ref, lk1_ref, lq2_ref, lk2_ref, subg_ref, o_ref, *, lambda_init):
    t = q_ref.shape[0]
    past = kc_ref.shape[1]
    lam = _diff_lambda(lq1_ref, lk1_ref, lq2_ref, lk2_ref, lambda_init)
    head_cols = [slice(h * B_V_DIM, (h + 1) * B_V_DIM) for h in range(B_HEADS)]
    scores = []
    for cs in head_cols:
        qs = _stack_maps(q_ref[:, cs])
        scores.append((_dot(qs, kc_ref[cs, :].astype(BF16)), _nt_dot(qs, kn_ref[:, cs])))
    attn = []
    for s_c, s_n in scores:
        m = jnp.maximum(jnp.max(s_c, axis=-1, keepdims=True),
                        jnp.max(s_n, axis=-1, keepdims=True))
        p_c = jnp.exp(s_c - m)
        p_n = jnp.exp(s_n - m)
        l = jnp.sum(p_c, axis=-1, keepdims=True) + jnp.sum(p_n, axis=-1, keepdims=True)
        p_c = p_c / l
        p_n = p_n / l
        attn.append(((p_c[:t] - lam * p_c[t:]).astype(BF16),
                     (p_n[:t] - lam * p_n[t:]).astype(BF16)))
    for h, (cs, (a_c, a_n)) in enumerate(zip(head_cols, attn)):
        vc = vc_ref[pl.ds(h, past, stride=B_HEADS), :].astype(BF16)
        o = _dot(a_c, vc) + _dot(a_n, vn_ref[:, cs])
        o_ref[:, cs] = _subnorm_gate(o, subg_ref, g_ref[:, cs].astype(F32),
                                     lambda_init).astype(BF16)


def _attn_b_sample(qb, kb, vb, cache_kt, cache_v, gb, lams, subg, *, batch, t, lambda_init):
    past = cache_kt.shape[2]
    new = pl.BlockSpec((t, B_WIDTH), lambda b: (b, 0))
    vec = pl.BlockSpec((1, B_HEAD_DIM), lambda b: (0, 0))
    return pl.pallas_call(
        functools.partial(_attn_b_sample_kernel, lambda_init=lambda_init),
        grid=(batch,),
        in_specs=[new, new, new,
                  pl.BlockSpec((None, B_WIDTH, past), lambda b: (b, 0, 0)),
                  pl.BlockSpec((None, past * B_HEADS, B_V_DIM), lambda b: (b, 0, 0)),
                  new, vec, vec, vec, vec,
                  pl.BlockSpec((1, B_V_DIM), lambda b: (0, 0))],
        out_specs=new,
        out_shape=jax.ShapeDtypeStruct(qb.shape, BF16),
        compiler_params=pltpu.CompilerParams(
            dimension_semantics=("arbitrary",), vmem_limit_bytes=VMEM_LIMIT),
        name="attn_b_sample",
    )(qb, kb, vb, cache_kt, cache_v, gb, *lams, subg)


def _out_kernel(oa_ref, *refs, final_norm, steps_per_seq):
    *ob_refs, wa_ref, wb_ref, x_ref, fg_ref, y_ref = refs
    if len(ob_refs) == 2:
        in_lo = pl.program_id(0) % steps_per_seq < steps_per_seq // 2
        ob = jnp.where(in_lo, ob_refs[0][...], ob_refs[1][...])
    else:
        ob = ob_refs[0][...]
    y = x_ref[...] + _dot(oa_ref[...], wa_ref[...]) + _dot(ob, wb_ref[...])
    if final_norm:
        y = y * lax.rsqrt(jnp.mean(y * y, axis=-1, keepdims=True) + EPS) * fg_ref[...]
    y_ref[...] = y


def _out_proj(oa, ob_parts, w_a, w_b, x, final_gain, *, tm, final_norm, steps_per_seq=1):
    rows, d_model = x.shape
    row_blk = lambda i: (i, 0)
    const2 = lambda i: (0, 0)
    if len(ob_parts) == 2:
        assert steps_per_seq % 2 == 0
        hs = steps_per_seq // 2
        seq_of = lambda i: i // steps_per_seq
        step_of = lambda i: i % steps_per_seq
        ob_specs = [
            pl.BlockSpec((tm, B_WIDTH),
                         lambda i: (seq_of(i) * hs + jnp.minimum(step_of(i), hs - 1), 0)),
            pl.BlockSpec((tm, B_WIDTH),
                         lambda i: (seq_of(i) * hs + jnp.maximum(step_of(i) - hs, 0), 0)),
        ]
    else:
        ob_specs = [pl.BlockSpec((tm, B_WIDTH), row_blk)]
    return pl.pallas_call(
        functools.partial(_out_kernel, final_norm=final_norm, steps_per_seq=steps_per_seq),
        grid=(rows // tm,),
        in_specs=[
            pl.BlockSpec((tm, A_WIDTH), row_blk),
            *ob_specs,
            pl.BlockSpec(w_a.shape, const2),
            pl.BlockSpec(w_b.shape, const2),
            pl.BlockSpec((tm, d_model), row_blk),
            pl.BlockSpec((1, d_model), const2),
        ],
        out_specs=pl.BlockSpec((tm, d_model), row_blk),
        out_shape=jax.ShapeDtypeStruct(x.shape, F32),
        compiler_params=pltpu.CompilerParams(
            dimension_semantics=("arbitrary",), vmem_limit_bytes=VMEM_LIMIT),
        name="out_proj",
    )(oa, *ob_parts, w_a, w_b, x, final_gain)


def _channel_major(a):
    n, pos = a.shape[:2]
    return jnp.moveaxis(a.reshape(n, pos, -1), 1, 2)


def _position_major(a, channel_dims):
    n, _, pos = a.shape
    return jnp.moveaxis(a, 2, 1).reshape(n, pos, *channel_dims)


def kernel(x_prompt, x_sample, cache_a_k, cache_a_v, cache_b_k, cache_b_v,
           norm_gain, w_in, w_out, rel_bias, lambda_q1, lambda_k1, lambda_q2, lambda_k2,
           subln_gain, final_gain):
    batch, seq, d_model = x_prompt.shape
    dec_batch, t_sample, _ = x_sample.shape
    depth = w_in.shape[0]
    past_len = cache_b_k.shape[2]
    a_keep = cache_a_k.shape[2]
    keep_prompt = min(BAND_ROWS, seq)
    assert seq % ROW_TILE == 0 and keep_prompt == ROW_TILE and seq % B_TQ == 0
    assert t_sample <= a_keep and t_sample <= LANES and d_model == 2 * SEG

    rows_p = batch * seq
    rows_s = dec_batch * t_sample
    tiles_per_seq = seq // ROW_TILE
    tables_p = _rope_tables(np.arange(ROW_TILE), np.arange(tiles_per_seq) * ROW_TILE)
    tables_s = _rope_tables(past_len + np.arange(rows_s) % t_sample, np.zeros(1))
    a_dims = (A_HEADS, A_HEAD_DIM)
    bk_dims = (B_HEADS, 2, B_HEAD_DIM)
    bv_dims = (B_HEADS, B_V_DIM)

    yp = x_prompt.reshape(rows_p, d_model)
    ys = x_sample.reshape(rows_s, d_model)
    fg = final_gain.reshape(1, d_model)
    outs = [[] for _ in range(8)]
    for l in range(depth):
        lambda_init = 0.8 - 0.6 * math.exp(-0.3 * l)
        last = l == depth - 1
        w_l = w_in[l].astype(BF16)
        w_a = w_out[l, :A_WIDTH].astype(BF16)
        w_b = w_out[l, A_WIDTH:].astype(BF16)
        gain = norm_gain[l].reshape(1, d_model)
        lams = [a[l].reshape(1, B_HEAD_DIM).astype(F32)
                for a in (lambda_q1, lambda_k1, lambda_q2, lambda_k2)]
        subg = subln_gain[l].reshape(1, B_V_DIM).astype(F32)

        qa, ka, va, ga, qb, kb, vb, gb, kaf, vaf, kbf, vbf = _proj(
            yp, gain, w_l, tables_p, tm=ROW_TILE, tiles_per_seq=tiles_per_seq,
            tail_every=tiles_per_seq, channel_major=True, q_scale=LOG2E * B_HEAD_DIM ** -0.5)
        bias_p = _bias_rows(rel_bias[l], 1 - A_WIN, A_QB - 1, lambda t: t + BAND_ROWS, LOG2E)
        oa = _attn_a_prompt(qa, ka, va, ga, bias_p, batch=batch, seq=seq)
        ob = _attn_b_prompt(qb, kb, vb, gb, lams, subg, batch=batch, seq=seq,
                            lambda_init=lambda_init)
        yp = _out_proj(oa, ob, w_a, w_b, yp, fg, tm=OUT_ROW_TILE, final_norm=last,
                       steps_per_seq=seq // OUT_ROW_TILE)
        outs[0].append(_position_major(kaf, a_dims))
        outs[1].append(_position_major(vaf, a_dims))
        outs[2].append(_position_major(kbf, bk_dims))
        outs[3].append(vbf.reshape(batch, seq, *bv_dims))

        qa, ka, va, ga, qb, kb, vb, gb, kaf, vaf, kbf, vbf = _proj(
            ys, gain, w_l, tables_s, tm=rows_s, tiles_per_seq=1, tail_every=1,
            channel_major=False, q_scale=B_HEAD_DIM ** -0.5)
        oa, ak_new, av_new = _attn_a_sample(
            qa, ka, va, kaf, vaf, _channel_major(cache_a_k[l]), _channel_major(cache_a_v[l]),
            ga, _bias_rows(rel_bias[l], 1 - t_sample, a_keep + t_sample - 1, lambda t: a_keep - t),
            batch=dec_batch, t=t_sample)
        ob = _attn_b_sample(
            qb, kb, vb, _channel_major(cache_b_k[l]),
            cache_b_v[l].reshape(dec_batch, past_len * B_HEADS, B_V_DIM),
            gb, lams, subg, batch=dec_batch, t=t_sample, lambda_init=lambda_init)
        ys = _out_proj(oa, (ob,), w_a, w_b, ys, fg, tm=rows_s, final_norm=last)
        outs[4].append(_position_major(ak_new, a_dims))
        outs[5].append(_position_major(av_new, a_dims))
        outs[6].append(kbf.reshape(dec_batch, t_sample, *bk_dims))
        outs[7].append(vbf.reshape(dec_batch, t_sample, *bv_dims))

    return (yp.reshape(batch, seq, d_model), ys.reshape(dec_batch, t_sample, d_model),
            *[jnp.stack(o) for o in outs])
```

```python
import functools
import math

import numpy as np
import jax
import jax.numpy as jnp
from jax import lax
from jax.experimental import pallas as pl
from jax.experimental.pallas import tpu as pltpu

F32 = jnp.float32
BF16 = jnp.bfloat16
NEG_INF = float("-inf")

CHUNK = 64
N_PREV_CHUNKS = 8
A_HEADS = 8
A_HEAD_DIM = 64
A_WIDTH = A_HEADS * A_HEAD_DIM
B_HEADS = 4
B_HEAD_DIM = 64
B_V_DIM = 2 * B_HEAD_DIM
B_WIDTH = B_HEADS * B_V_DIM
SEG = 512
MAX_REL = 128
ROPE_THETA = 500000.0
ROPE_DIM = 16
EPS = 1e-6
LOG2E = math.log2(math.e)
LANES = 128
VMEM_LIMIT = 56 * 1024 * 1024
ROW_TILE = 512
BAND_ROWS = N_PREV_CHUNKS * CHUNK
A_QB = 256
A_WIN = BAND_ROWS + A_QB
A_ROWS = 64
A_QK_SPLIT = 3
A_SUM_ROWS = 16
B_TQ = 512
B_TK = 512
B_HEADS_PER_STEP = 4
B_STRIP = 256
B_ROWS = 64
B_QK_SPLIT = 2
B_SUM_ROWS = 16

NT_DIMS = (((1,), (1,)), ((), ()))


def _nt_dot(a, b):
    return lax.dot_general(a, b, NT_DIMS, preferred_element_type=F32)


def _dot(a, b):
    return jnp.dot(a, b, preferred_element_type=F32)


def _silu(g):
    return g / (1.0 + jnp.exp(-g))


def _round_up(n, m):
    return -(-n // m) * m


def _rope_lane_freq():
    d = np.arange(LANES) % B_HEAD_DIM
    inv = ROPE_THETA ** (-np.arange(0, ROPE_DIM, 2, dtype=np.float64) / ROPE_DIM)
    return np.where(d < ROPE_DIM, inv[d % (ROPE_DIM // 2)], 0.0)


def _rope_tables(row_pos, base_pos):
    f = _rope_lane_freq()[None, :]
    ar = np.asarray(row_pos, np.float64)[:, None] * f
    ab = np.asarray(base_pos, np.float64)[:, None] * f
    as32 = lambda a: jnp.asarray(a.astype(np.float32))
    return (as32(np.cos(ar)), as32(np.sin(ar)),
            as32(np.cos(ab))[:, None, :], as32(np.sin(ab))[:, None, :])


def _bias_rows(rel_bias, t_min, t_max, rel_of_t, scale=1.0):
    width = _round_up(t_max - t_min + 1, LANES)
    t = np.arange(width)
    t = np.where(t <= t_max, t, t - width)
    idx = np.clip(rel_of_t(t), -MAX_REL, MAX_REL) + MAX_REL
    return rel_bias.astype(F32)[:, idx] * scale


def _toeplitz(row, n_rows):
    return pltpu.roll(jnp.broadcast_to(row, (n_rows, row.shape[-1])), 0, 1, stride=1, stride_axis=0)


def _proj_kernel(x_ref, g_ref, w_ref, cr_ref, sr_ref, cb_ref, sb_ref, *out_refs,
                 channel_major, q_scale):
    (qa_ref, ka_ref, va_ref, ga_ref, qb_ref, kb_ref, vb_ref, gb_ref,
     kaf_ref, vaf_ref, kbf_ref, vbf_ref) = out_refs
    x = x_ref[...]
    inv = lax.rsqrt(jnp.mean(x * x, axis=-1, keepdims=True) + EPS)
    h = (x * inv * g_ref[...]).astype(BF16)

    def seg(k):
        return _dot(h, w_ref[:, k * SEG:(k + 1) * SEG])

    cb, sb = cb_ref[0], sb_ref[0]
    cr, sr = cr_ref[...], sr_ref[...]
    cos = cb * cr - sb * sr
    sin = sb * cr + cb * sr
    d = lax.broadcasted_iota(jnp.int32, cos.shape, 1) % B_HEAD_DIM
    sin_lo = jnp.where(d < ROPE_DIM // 2, -sin, 0.0)
    sin_hi = jnp.where(d >= ROPE_DIM // 2, sin, 0.0)

    def rope(z):
        cols = []
        for c in range(SEG // LANES):
            zc = z[:, c * LANES:(c + 1) * LANES]
            up = pltpu.roll(zc, LANES - ROPE_DIM // 2, 1)
            dn = pltpu.roll(zc, ROPE_DIM // 2, 1)
            cols.append(zc * cos + up * sin_lo + dn * sin_hi)
        return jnp.concatenate(cols, axis=1)

    qa = seg(0) * q_scale
    qa_ref[...] = (qa.T if channel_major else qa).astype(BF16)
    ka = seg(1)
    ka_ref[...] = ka.astype(BF16)
    va = seg(2)
    va_t = va.T if channel_major else va
    va_ref[...] = va_t.astype(BF16)
    kaf_ref[...] = ka.T if channel_major else ka
    vaf_ref[...] = va_t

    ga_ref[...] = seg(3).astype(BF16)
    qb = rope(seg(4)) * q_scale
    qb_ref[...] = (qb.T if channel_major else qb).astype(BF16)
    kb = rope(seg(5))
    kbf_ref[...] = kb.T if channel_major else kb
    kb_ref[...] = kb.astype(BF16)
    vb = seg(6)
    n_col = SEG // LANES
    for c in range(n_col):
        vbf_ref[pl.ds(c, vb.shape[0], stride=n_col), :] = vb[:, c * LANES:(c + 1) * LANES]
    vb_ref[...] = (vb.T if channel_major else vb).astype(BF16)
    gb_ref[...] = seg(7).astype(BF16)


def _proj(x, gain, w_bf16, tables, *, tm, tiles_per_seq, tail_every, channel_major, q_scale):
    rows, d_model = x.shape
    n_tiles = rows // tm
    n_tail = n_tiles // tail_every
    n_seq = n_tiles // tiles_per_seq
    cr, sr, cb, sb = tables
    row_blk = lambda i: (i, 0)
    const2 = lambda i: (0, 0)
    base_blk = lambda i: (i % tiles_per_seq, 0, 0)
    bf = jax.ShapeDtypeStruct((rows, SEG), BF16)
    f32_full = jax.ShapeDtypeStruct((rows, SEG), F32)
    full_spec = pl.BlockSpec((tm, SEG), row_blk)
    if channel_major:
        assert tm == SEG
        tail = jax.ShapeDtypeStruct((n_tail, SEG, tm), F32)
        tail_spec = pl.BlockSpec((None, SEG, tm), lambda i: (i // tail_every, 0, 0))
        kbf = jax.ShapeDtypeStruct((n_seq, SEG, tiles_per_seq * tm), F32)
        kbf_spec = pl.BlockSpec((None, SEG, tm), lambda i: (i // tiles_per_seq, 0, i % tiles_per_seq))
        bf_t = jax.ShapeDtypeStruct((n_tiles, SEG, tm), BF16)
        bf_t_spec = pl.BlockSpec((None, SEG, tm), lambda i: (i, 0, 0))
    else:
        tail = jax.ShapeDtypeStruct((n_tail * tm, SEG), F32)
        tail_spec = pl.BlockSpec((tm, SEG), lambda i: (i // tail_every, 0))
        kbf, kbf_spec = f32_full, full_spec
        bf_t, bf_t_spec = bf, full_spec
    return pl.pallas_call(
        functools.partial(_proj_kernel, channel_major=channel_major,
                          q_scale=q_scale),
        grid=(n_tiles,),
        in_specs=[
            pl.BlockSpec((tm, d_model), row_blk),
            pl.BlockSpec((1, d_model), const2),
            pl.BlockSpec(w_bf16.shape, const2),
            pl.BlockSpec((tm, LANES), const2),
            pl.BlockSpec((tm, LANES), const2),
            pl.BlockSpec((1, 1, LANES), base_blk),
            pl.BlockSpec((1, 1, LANES), base_blk),
        ],
        out_specs=([bf_t_spec, full_spec, bf_t_spec, full_spec] * 2
                   + [tail_spec, tail_spec, kbf_spec,
                      pl.BlockSpec((tm * SEG // LANES, LANES), row_blk)]),
        out_shape=([bf_t, bf, bf_t, bf] * 2
                   + [tail, tail, kbf, jax.ShapeDtypeStruct((rows * SEG // LANES, LANES), F32)]),
        compiler_params=pltpu.CompilerParams(
            dimension_semantics=("arbitrary",), vmem_limit_bytes=VMEM_LIMIT),
        name="proj",
    )(x, gain, w_bf16, cr, sr, cb, sb)


def _attn_a_prompt_kernel(qt_ref, kp_ref, kc_ref, vtp_ref, vtc_ref, g_ref, brow_ref,
                          ob_lo_ref, ob_hi_ref, wa_ref, wb_ref, x_ref, fg_ref, y_ref,
                          bias_ref, oa_ref, sa_ref, sb_ref, sc_ref, *, final_norm, tiles):
    n = pl.program_id(1)

    @pl.when((pl.program_id(0) == 0) & (n == 0))
    def _():
        j = lax.broadcasted_iota(jnp.int32, (A_WIN, A_QB), 0)
        i = lax.broadcasted_iota(jnp.int32, (A_WIN, A_QB), 1)
        gap = i // CHUNK + N_PREV_CHUNKS - j // CHUNK
        band = (gap >= 0) & (gap <= N_PREV_CHUNKS)
        for hd in range(A_HEADS):
            t = _toeplitz(brow_ref[hd:hd + 1, :], A_WIN)[:, :A_QB]
            bias_ref[hd] = jnp.where(band, t, NEG_INF)

    items = [(q0, hd) for q0 in range(0, ROW_TILE, A_QB) for hd in range(A_HEADS)]
    part = A_WIN // A_QK_SPLIT

    def window(q0, first_tile):
        pieces = []
        for r in range(0, A_WIN, part):
            pos = q0 - BAND_ROWS + r
            if pos >= 0:
                pieces.append((1, pos, r))
            elif not first_tile:
                pieces.append((0, pos + ROW_TILE, r))
        return pieces

    def score_item(t, dst, first_tile):
        q0, hd = items[t]
        lanes = slice(hd // 2 * LANES, (hd // 2 + 1) * LANES)
        qt = qt_ref[lanes, q0:q0 + A_QB]
        row = lax.broadcasted_iota(jnp.int32, qt.shape, 0)
        rhs = jnp.where((row >= A_HEAD_DIM) == bool(hd % 2), qt, jnp.zeros_like(qt))
        for which, start, r in window(q0, first_tile):
            k_ref = (kp_ref, kc_ref)[which]
            dst[r:r + part, :] = (_dot(k_ref[start:start + part, lanes], rhs)
                                  + bias_ref[hd, r:r + part, :])

    def update_item(t, src, first_tile):
        q0, hd = items[t]
        pieces = window(q0, first_tile)
        chunks = [r + c for _, _, r in pieces for c in range(0, part, A_ROWS)]
        m = jnp.max(functools.reduce(jnp.maximum, [src[r:r + A_ROWS, :] for r in chunks]),
                    axis=0, keepdims=True)
        o = None
        for which, start, r in pieces:
            p = jnp.concatenate([jnp.exp2((src[r + c:r + c + A_ROWS, :] - m).astype(BF16))
                                 for c in range(0, part, A_ROWS)], axis=0)
            vt_ref = (vtp_ref, vtc_ref)[which]
            vt1 = jnp.concatenate(
                [vt_ref[hd * A_HEAD_DIM:(hd + 1) * A_HEAD_DIM, start:start + part],
                 jnp.ones((A_SUM_ROWS, part), BF16)], axis=0)
            contrib = _dot(vt1, p)
            o = contrib if o is None else o + contrib
        return o[:A_HEAD_DIM] / o[A_HEAD_DIM:A_HEAD_DIM + 1]

    def run(first_tile):
        bufs = (sa_ref, sb_ref, sc_ref)
        depth = len(bufs) - 1
        for t in range(depth):
            score_item(t, bufs[t % len(bufs)], first_tile)
        o_even = None
        for t, (q0, hd) in enumerate(items):
            if t + depth < len(items):
                score_item(t + depth, bufs[(t + depth) % len(bufs)], first_tile)
            o_head = update_item(t, bufs[t % len(bufs)], first_tile)
            if hd % 2 == 0:
                o_even = o_head
            else:
                lanes = slice(hd // 2 * LANES, (hd // 2 + 1) * LANES)
                o_pair = jnp.concatenate([o_even, o_head], axis=0).T
                gate = g_ref[q0:q0 + A_QB, lanes].astype(F32)
                oa_ref[q0:q0 + A_QB, lanes] = (o_pair * _silu(gate)).astype(BF16)
            if hd == A_HEADS - 1:
                rows = slice(q0, q0 + A_QB)
                ob = jnp.where(n < tiles // 2, ob_lo_ref[rows, :], ob_hi_ref[rows, :])
                y = x_ref[rows, :] + _dot(oa_ref[rows, :], wa_ref[...]) + _dot(ob, wb_ref[...])
                if final_norm:
                    y = y * lax.rsqrt(jnp.mean(y * y, axis=-1, keepdims=True) + EPS) * fg_ref[...]
                y_ref[rows, :] = y

    @pl.when(n == 0)
    def _():
        run(True)

    @pl.when(n > 0)
    def _():
        run(False)


def _attn_a_out_prompt(qat, ka, vat, ga, bias_rows, ob_lo, ob_hi, w_a, w_b, x, final_gain, *,
                       batch, seq, final_norm):
    tiles = seq // ROW_TILE
    assert tiles % 2 == 0
    half = tiles // 2
    d_model = x.shape[1]
    cur = lambda b, n: (b * tiles + n, 0)
    prev = lambda b, n: (b * tiles + jnp.maximum(n - 1, 0), 0)
    cur_t = lambda b, n: (b * tiles + n, 0, 0)
    prev_t = lambda b, n: (b * tiles + jnp.maximum(n - 1, 0), 0, 0)
    const2 = lambda b, n: (0, 0)
    blk = (ROW_TILE, A_WIDTH)
    blk_t = (None, A_WIDTH, ROW_TILE)
    return pl.pallas_call(
        functools.partial(_attn_a_prompt_kernel, final_norm=final_norm, tiles=tiles),
        grid=(batch, tiles),
        in_specs=[
            pl.BlockSpec(blk_t, cur_t),
            pl.BlockSpec(blk, prev), pl.BlockSpec(blk, cur),
            pl.BlockSpec(blk_t, prev_t), pl.BlockSpec(blk_t, cur_t),
            pl.BlockSpec(blk, cur),
            pl.BlockSpec(bias_rows.shape, const2),
            pl.BlockSpec((ROW_TILE, B_WIDTH), lambda b, n: (b * half + jnp.minimum(n, half - 1), 0)),
            pl.BlockSpec((ROW_TILE, B_WIDTH), lambda b, n: (b * half + jnp.maximum(n - half, 0), 0)),
            pl.BlockSpec(w_a.shape, const2),
            pl.BlockSpec(w_b.shape, const2),
            pl.BlockSpec((ROW_TILE, d_model), cur),
            pl.BlockSpec((1, d_model), const2),
        ],
        out_specs=pl.BlockSpec((ROW_TILE, d_model), cur),
        out_shape=jax.ShapeDtypeStruct(x.shape, F32),
        scratch_shapes=[pltpu.VMEM((A_HEADS, A_WIN, A_QB), F32),
                        pltpu.VMEM((ROW_TILE, A_WIDTH), BF16),
                        pltpu.VMEM((A_WIN, A_QB), F32),
                        pltpu.VMEM((A_WIN, A_QB), F32),
                        pltpu.VMEM((A_WIN, A_QB), F32)],
        compiler_params=pltpu.CompilerParams(
            dimension_semantics=("arbitrary", "arbitrary"), vmem_limit_bytes=VMEM_LIMIT),
        name="attn_a_out_prompt",
    )(qat, ka, ka, vat, vat, ga, bias_rows, ob_lo, ob_hi, w_a, w_b, x, final_gain)


def _diff_lambda(lq1_ref, lk1_ref, lq2_ref, lk2_ref, lambda_init):
    e1 = jnp.exp(jnp.sum(lq1_ref[...] * lk1_ref[...], axis=-1, keepdims=True))
    e2 = jnp.exp(jnp.sum(lq2_ref[...] * lk2_ref[...], axis=-1, keepdims=True))
    return e1 - e2 + lambda_init


def _stack_maps(qh):
    lane = lax.broadcasted_iota(jnp.int32, qh.shape, 1)
    zero = jnp.zeros_like(qh)
    return jnp.concatenate([jnp.where(lane < B_HEAD_DIM, qh, zero),
                            jnp.where(lane >= B_HEAD_DIM, qh, zero)], axis=0)


def _subnorm_gate(o, subg_ref, gate, lambda_init):
    o = o * lax.rsqrt(jnp.mean(o * o, axis=-1, keepdims=True) + EPS) * subg_ref[...]
    return (o * (1.0 - lambda_init)) * _silu(gate)


def _attn_b_prompt_kernel(qta_ref, qtb_ref, k_ref, vt_ref, ga_ref, gb_ref,
                          lq1_ref, lk1_ref, lq2_ref, lk2_ref, subg_ref, oa_ref, ob_ref,
                          rhs_ref, acc_ref, m_ref, sa_ref, sb_ref, *, lambda_init, tiles):
    g = pl.program_id(2)
    heads = rhs_ref.shape[1]
    for t, qt_ref in enumerate((qta_ref, qtb_ref)):
        for h in range(heads):
            qt = qt_ref[h * B_V_DIM:(h + 1) * B_V_DIM, :]
            row = lax.broadcasted_iota(jnp.int32, qt.shape, 0)
            zero = jnp.zeros_like(qt)
            rhs_ref[t, h, :, :B_TQ] = jnp.where(row < B_HEAD_DIM, qt, zero)
            rhs_ref[t, h, :, B_TQ:] = jnp.where(row >= B_HEAD_DIM, qt, zero)
    m_ref[...] = jnp.full(m_ref.shape, NEG_INF, F32)
    acc_ref[...] = jnp.zeros(acc_ref.shape, F32)

    items = [(h, c) for h in range(heads) for c in range(2 * B_TQ // B_STRIP)]

    def live_keys(c, diagonal):
        q0 = (c * B_STRIP) % B_TQ
        return min(B_TK, q0 + B_STRIP) if diagonal else B_TK

    def score_item(tile, j, n, dst, diagonal):
        h, c = items[n]
        part = B_TK // B_QK_SPLIT
        for r in range(0, live_keys(c, diagonal), part):
            k0 = pl.multiple_of(j * B_TK + r, part)
            dst[n, r:r + part, :] = _dot(k_ref[pl.ds(k0, part), h * B_V_DIM:(h + 1) * B_V_DIM],
                                         rhs_ref[tile, h, :, c * B_STRIP:(c + 1) * B_STRIP])

    def update_item(tile, j, n, src, diagonal):
        h, c = items[n]
        hs = slice(h * B_V_DIM, (h + 1) * B_V_DIM)
        cols = slice(c * B_STRIP, (c + 1) * B_STRIP)
        live = live_keys(c, diagonal)

        def chunk(r):
            s = src[n, r:r + B_ROWS, :]
            if diagonal:
                q0 = (c * B_STRIP) % B_TQ
                kc = (r + lax.broadcasted_iota(jnp.int32, s.shape, 0)) // CHUNK
                qc = (q0 + lax.broadcasted_iota(jnp.int32, s.shape, 1)) // CHUNK
                s = jnp.where(kc <= qc, s, NEG_INF)
            return s

        rows = range(0, live, B_ROWS)
        m_blk = functools.reduce(jnp.maximum, [chunk(r) for r in rows])
        m_old = m_ref[tile, h, :, cols]
        m_new = jnp.maximum(m_old, jnp.max(m_blk, axis=0, keepdims=True))
        alpha = jnp.exp2(m_old - m_new)
        m_ref[tile, h, :, cols] = m_new
        p = jnp.concatenate([jnp.exp2((chunk(r) - m_new).astype(BF16)) for r in rows], axis=0)
        vt1 = jnp.concatenate([vt_ref[j, hs, :live], jnp.ones((B_SUM_ROWS, live), BF16)], axis=0)
        acc_ref[tile, h, :, cols] = alpha * acc_ref[tile, h, :, cols] + _dot(vt1, p)

    def stage(score, dst, update, src, score_diagonal=False, update_diagonal=False):
        for n in range(len(items)):
            if score is not None:
                score_item(*score, n, dst, score_diagonal)
            if update is not None:
                update_item(*update, n, src, update_diagonal)

    buf_a, buf_b = sa_ref, sb_ref
    first = (0, g)
    last = (1, tiles - 1 - g)

    def at(s):
        in_lo = s <= g
        return jnp.where(in_lo, 0, 1), jnp.where(in_lo, s - 1, s - 1 - g)

    def pair(p, carry):
        s = 2 * p + 1
        stage(at(s + 1), buf_a, at(s), buf_b)
        stage(at(s + 2), buf_b, at(s + 1), buf_a)
        return carry

    stage(first, buf_a, None, None, score_diagonal=True)
    stage(at(1), buf_b, first, buf_a, update_diagonal=True)
    lax.fori_loop(0, (tiles - 2) // 2, pair, 0)
    stage(last, buf_a, at(tiles - 1), buf_b, score_diagonal=True)
    stage(None, None, last, buf_a, update_diagonal=True)

    lam = _diff_lambda(lq1_ref, lk1_ref, lq2_ref, lk2_ref, lambda_init)
    for t, (g_ref, o_ref) in enumerate(((ga_ref, oa_ref), (gb_ref, ob_ref))):
        for h in range(heads):
            hs = slice(h * B_V_DIM, (h + 1) * B_V_DIM)
            o = acc_ref[t, h, :B_V_DIM, :] / acc_ref[t, h, B_V_DIM:B_V_DIM + 1, :]
            o = (o[:, :B_TQ] - lam * o[:, B_TQ:]).T
            o_ref[:, hs] = _subnorm_gate(o, subg_ref, g_ref[:, hs].astype(F32),
                                         lambda_init).astype(BF16)


def _attn_b_prompt(qbt, kb, vbt, gb, lams, subg, *, batch, seq, lambda_init):
    assert B_TQ == B_TK == ROW_TILE and B_HEADS % B_HEADS_PER_STEP == 0
    tiles = seq // B_TQ
    assert tiles % 2 == 0
    half = tiles // 2
    width = B_HEADS_PER_STEP * B_V_DIM
    n_items = B_HEADS_PER_STEP * 2 * B_TQ // B_STRIP
    lo = lambda b, g: b * tiles + g
    hi = lambda b, g: b * tiles + tiles - 1 - g
    vec = pl.BlockSpec((1, B_HEAD_DIM), lambda b, h, g: (0, 0))
    vbt = vbt.reshape(batch, tiles, B_WIDTH, B_TK)
    out = jax.ShapeDtypeStruct((batch * half * B_TQ, B_WIDTH), BF16)
    return pl.pallas_call(
        functools.partial(_attn_b_prompt_kernel, lambda_init=lambda_init, tiles=tiles),
        grid=(batch, B_HEADS // B_HEADS_PER_STEP, half),
        in_specs=[
            pl.BlockSpec((None, width, B_TQ), lambda b, h, g: (lo(b, g), h, 0)),
            pl.BlockSpec((None, width, B_TQ), lambda b, h, g: (hi(b, g), h, 0)),
            pl.BlockSpec((seq, width), lambda b, h, g: (b, h), pipeline_mode=pl.Buffered(1)),
            pl.BlockSpec((None, tiles, width, B_TK), lambda b, h, g: (b, 0, h, 0),
                         pipeline_mode=pl.Buffered(1)),
            pl.BlockSpec((B_TQ, width), lambda b, h, g: (lo(b, g), h)),
            pl.BlockSpec((B_TQ, width), lambda b, h, g: (hi(b, g), h)),
            vec, vec, vec, vec,
            pl.BlockSpec((1, B_V_DIM), lambda b, h, g: (0, 0)),
        ],
        out_specs=[pl.BlockSpec((B_TQ, width), lambda b, h, g: (b * half + g, h)),
                   pl.BlockSpec((B_TQ, width), lambda b, h, g: (b * half + half - 1 - g, h))],
        out_shape=[out, out],
        scratch_shapes=[pltpu.VMEM((2, B_HEADS_PER_STEP, B_V_DIM, 2 * B_TQ), BF16),
                        pltpu.VMEM((2, B_HEADS_PER_STEP, B_V_DIM + B_SUM_ROWS, 2 * B_TQ), F32),
                        pltpu.VMEM((2, B_HEADS_PER_STEP, 1, 2 * B_TQ), F32),
                        pltpu.VMEM((n_items, B_TK, B_STRIP), F32),
                        pltpu.VMEM((n_items, B_TK, B_STRIP), F32)],
        compiler_params=pltpu.CompilerParams(
            dimension_semantics=("arbitrary", "arbitrary", "arbitrary"),
            vmem_limit_bytes=VMEM_LIMIT),
        name="attn_b_prompt",
    )(qbt, qbt, kb, vbt, gb, gb, *lams, subg)


def _roll_in(cache_t, new_rows):
    t = new_rows.shape[0]
    keep = cache_t.shape[1]
    shifted = pltpu.roll(cache_t, keep - t, 1)
    pad = jnp.concatenate([jnp.zeros((LANES - t, new_rows.shape[1]), F32), new_rows], axis=0)
    new_t = pad.T
    lane = lax.broadcasted_iota(jnp.int32, new_t.shape, 1)
    last = jnp.where(lane >= LANES - t, new_t, shifted[:, keep - LANES:])
    return jnp.concatenate([shifted[:, :keep - LANES], last], axis=1)


def _attn_a_sample_kernel(q_ref, kn_ref, vn_ref, knf_ref, vnf_ref, kc_ref, vc_ref, g_ref,
                          brow_ref, o_ref, ko_ref, vo_ref):
    t = q_ref.shape[0]
    keep = kc_ref.shape[1]
    lane = lax.broadcasted_iota(jnp.int32, (t, LANES), 1)
    low_half = lane < A_HEAD_DIM
    pair_cols = [slice(pair * LANES, (pair + 1) * LANES) for pair in range(A_HEADS // 2)]
    scores = []
    for hd in range(A_HEADS):
        cs = pair_cols[hd // 2]
        qp = q_ref[:, cs]
        bias = _toeplitz(brow_ref[hd:hd + 1, :], t)
        qm = jnp.where(low_half if hd % 2 == 0 else ~low_half, qp, jnp.zeros_like(qp))
        scores.append((_dot(qm, kc_ref[cs, :].astype(BF16)) + bias[:, :keep],
                       _nt_dot(qm, kn_ref[:, cs]) + bias[:, keep:keep + t]))
    probs = []
    for s_c, s_n in scores:
        m = jnp.maximum(jnp.max(s_c, axis=-1, keepdims=True), jnp.max(s_n, axis=-1, keepdims=True))
        p_c = jnp.exp(s_c - m)
        p_n = jnp.exp(s_n - m)
        l = jnp.sum(p_c, axis=-1, keepdims=True) + jnp.sum(p_n, axis=-1, keepdims=True)
        probs.append((p_c.astype(BF16), p_n.astype(BF16), l))
    outs = []
    for hd, (p_c, p_n, l) in enumerate(probs):
        cs = pair_cols[hd // 2]
        o = _nt_dot(p_c, vc_ref[cs, :].astype(BF16)) + _dot(p_n, vn_ref[:, cs])
        outs.append(o / l)
    for pair, cs in enumerate(pair_cols):
        o_pair = jnp.where(low_half, outs[2 * pair], outs[2 * pair + 1])
        o_ref[:, cs] = (o_pair * _silu(g_ref[:, cs].astype(F32))).astype(BF16)
    ko_ref[...] = _roll_in(kc_ref[...], knf_ref[...])
    vo_ref[...] = _roll_in(vc_ref[...], vnf_ref[...])


def _attn_a_sample(qa, ka, va, kaf, vaf, cache_kt, cache_vt, ga, bias_rows, *, batch, t):
    keep = cache_kt.shape[2]
    new = pl.BlockSpec((t, A_WIDTH), lambda b: (b, 0))
    cache = pl.BlockSpec((None, A_WIDTH, keep), lambda b: (b, 0, 0))
    return pl.pallas_call(
        _attn_a_sample_kernel,
        grid=(batch,),
        in_specs=[new, new, new, new, new, cache, cache, new,
                  pl.BlockSpec(bias_rows.shape, lambda b: (0, 0))],
        out_specs=[new, cache, cache],
        out_shape=[jax.ShapeDtypeStruct(qa.shape, BF16),
                   jax.ShapeDtypeStruct(cache_kt.shape, F32),
                   jax.ShapeDtypeStruct(cache_vt.shape, F32)],
        compiler_params=pltpu.CompilerParams(
            dimension_semantics=("arbitrary",), vmem_limit_bytes=VMEM_LIMIT),
        name="attn_a_sample",
    )(qa, ka, va, kaf, vaf, cache_kt, cache_vt, ga, bias_rows)


def _attn_b_sample_kernel(q_ref, kn_ref, vn_ref, kc_ref, vc_ref, g_ref,
                          lq1_ref, lk1_ref, lq2_ref, lk2_ref, subg_ref, o_ref, *, lambda_init):
    t = q_ref.shape[0]
    past = kc_ref.shape[1]
    lam = _diff_lambda(lq1_ref, lk1_ref, lq2_ref, lk2_ref, lambda_init)
    head_cols = [slice(h * B_V_DIM, (h + 1) * B_V_DIM) for h in range(B_HEADS)]
    scores = []
    for cs in head_cols:
        qs = _stack_maps(q_ref[:, cs])
        scores.append((_dot(qs, kc_ref[cs, :].astype(BF16)), _nt_dot(qs, kn_ref[:, cs])))
    attn = []
    for s_c, s_n in scores:
        m = jnp.maximum(jnp.max(s_c, axis=-1, keepdims=True),
                        jnp.max(s_n, axis=-1, keepdims=True))
        p_c = jnp.exp(s_c - m)
        p_n = jnp.exp(s_n - m)
        l = jnp.sum(p_c, axis=-1, keepdims=True) + jnp.sum(p_n, axis=-1, keepdims=True)
        p_c = p_c / l
        p_n = p_n / l
        attn.append(((p_c[:t] - lam * p_c[t:]).astype(BF16),
                     (p_n[:t] - lam * p_n[t:]).astype(BF16)))
    for h, (cs, (a_c, a_n)) in enumerate(zip(head_cols, attn)):
        vc = vc_ref[pl.ds(h, past, stride=B_HEADS), :].astype(BF16)
        o = _dot(a_c, vc) + _dot(a_n, vn_ref[:, cs])
        o_ref[:, cs] = _subnorm_gate(o, subg_ref, g_ref[:, cs].astype(F32),
                                     lambda_init).astype(BF16)


def _attn_b_sample(qb, kb, vb, cache_kt, cache_v, gb, lams, subg, *, batch, t, lambda_init):
    past = cache_kt.shape[2]
    new = pl.BlockSpec((t, B_WIDTH), lambda b: (b, 0))
    vec = pl.BlockSpec((1, B_HEAD_DIM), lambda b: (0, 0))
    return pl.pallas_call(
        functools.partial(_attn_b_sample_kernel, lambda_init=lambda_init),
        grid=(batch,),
        in_specs=[new, new, new,
                  pl.BlockSpec((None, B_WIDTH, past), lambda b: (b, 0, 0)),
                  pl.BlockSpec((None, past * B_HEADS, B_V_DIM), lambda b: (b, 0, 0)),
                  new, vec, vec, vec, vec,
                  pl.BlockSpec((1, B_V_DIM), lambda b: (0, 0))],
        out_specs=new,
        out_shape=jax.ShapeDtypeStruct(qb.shape, BF16),
        compiler_params=pltpu.CompilerParams(
            dimension_semantics=("arbitrary",), vmem_limit_bytes=VMEM_LIMIT),
        name="attn_b_sample",
    )(qb, kb, vb, cache_kt, cache_v, gb, *lams, subg)


def _out_kernel(oa_ref, ob_ref, wa_ref, wb_ref, x_ref, fg_ref, y_ref, *, final_norm):
    y = x_ref[...] + _dot(oa_ref[...], wa_ref[...]) + _dot(ob_ref[...], wb_ref[...])
    if final_norm:
        y = y * lax.rsqrt(jnp.mean(y * y, axis=-1, keepdims=True) + EPS) * fg_ref[...]
    y_ref[...] = y


def _out_proj(oa, ob, w_a, w_b, x, final_gain, *, tm, final_norm):
    rows, d_model = x.shape
    row_blk = lambda i: (i, 0)
    const2 = lambda i: (0, 0)
    return pl.pallas_call(
        functools.partial(_out_kernel, final_norm=final_norm),
        grid=(rows // tm,),
        in_specs=[
            pl.BlockSpec((tm, A_WIDTH), row_blk),
            pl.BlockSpec((tm, B_WIDTH), row_blk),
            pl.BlockSpec(w_a.shape, const2),
            pl.BlockSpec(w_b.shape, const2),
            pl.BlockSpec((tm, d_model), row_blk),
            pl.BlockSpec((1, d_model), const2),
        ],
        out_specs=pl.BlockSpec((tm, d_model), row_blk),
        out_shape=jax.ShapeDtypeStruct(x.shape, F32),
        compiler_params=pltpu.CompilerParams(
            dimension_semantics=("arbitrary",), vmem_limit_bytes=VMEM_LIMIT),
        name="out_proj",
    )(oa, ob, w_a, w_b, x, final_gain)


def _channel_major(a):
    n, pos = a.shape[:2]
    return jnp.moveaxis(a.reshape(n, pos, -1), 1, 2)


def _position_major(a, channel_dims):
    n, _, pos = a.shape
    return jnp.moveaxis(a, 2, 1).reshape(n, pos, *channel_dims)


def kernel(x_prompt, x_sample, cache_a_k, cache_a_v, cache_b_k, cache_b_v,
           norm_gain, w_in, w_out, rel_bias, lambda_q1, lambda_k1, lambda_q2, lambda_k2,
           subln_gain, final_gain):
    batch, seq, d_model = x_prompt.shape
    dec_batch, t_sample, _ = x_sample.shape
    depth = w_in.shape[0]
    past_len = cache_b_k.shape[2]
    a_keep = cache_a_k.shape[2]
    keep_prompt = min(BAND_ROWS, seq)
    assert seq % ROW_TILE == 0 and keep_prompt == ROW_TILE and seq % B_TQ == 0
    assert t_sample <= a_keep and t_sample <= LANES and d_model == 2 * SEG

    rows_p = batch * seq
    rows_s = dec_batch * t_sample
    tiles_per_seq = seq // ROW_TILE
    tables_p = _rope_tables(np.arange(ROW_TILE), np.arange(tiles_per_seq) * ROW_TILE)
    tables_s = _rope_tables(past_len + np.arange(rows_s) % t_sample, np.zeros(1))
    a_dims = (A_HEADS, A_HEAD_DIM)
    bk_dims = (B_HEADS, 2, B_HEAD_DIM)
    bv_dims = (B_HEADS, B_V_DIM)

    yp = x_prompt.reshape(rows_p, d_model)
    ys = x_sample.reshape(rows_s, d_model)
    fg = final_gain.reshape(1, d_model)
    outs = [[] for _ in range(8)]
    for l in range(depth):
        lambda_init = 0.8 - 0.6 * math.exp(-0.3 * l)
        last = l == depth - 1
        w_l = w_in[l].astype(BF16)
        w_a = w_out[l, :A_WIDTH].astype(BF16)
        w_b = w_out[l, A_WIDTH:].astype(BF16)
        gain = norm_gain[l].reshape(1, d_model)
        lams = [a[l].reshape(1, B_HEAD_DIM).astype(F32)
                for a in (lambda_q1, lambda_k1, lambda_q2, lambda_k2)]
        subg = subln_gain[l].reshape(1, B_V_DIM).astype(F32)

        qa, ka, va, ga, qb, kb, vb, gb, kaf, vaf, kbf, vbf = _proj(
            yp, gain, w_l, tables_p, tm=ROW_TILE, tiles_per_seq=tiles_per_seq,
            tail_every=tiles_per_seq, channel_major=True, q_scale=LOG2E * B_HEAD_DIM ** -0.5)
        bias_p = _bias_rows(rel_bias[l], 1 - A_WIN, A_QB - 1, lambda t: t + BAND_ROWS, LOG2E)
        ob_lo, ob_hi = _attn_b_prompt(qb, kb, vb, gb, lams, subg, batch=batch, seq=seq,
                                      lambda_init=lambda_init)
        yp = _attn_a_out_prompt(qa, ka, va, ga, bias_p, ob_lo, ob_hi, w_a, w_b, yp, fg,
                                batch=batch, seq=seq, final_norm=last)
        outs[0].append(_position_major(kaf, a_dims))
        outs[1].append(_position_major(vaf, a_dims))
        outs[2].append(_position_major(kbf, bk_dims))
        outs[3].append(vbf.reshape(batch, seq, *bv_dims))

        qa, ka, va, ga, qb, kb, vb, gb, kaf, vaf, kbf, vbf = _proj(
            ys, gain, w_l, tables_s, tm=rows_s, tiles_per_seq=1, tail_every=1,
            channel_major=False, q_scale=B_HEAD_DIM ** -0.5)
        oa, ak_new, av_new = _attn_a_sample(
            qa, ka, va, kaf, vaf, _channel_major(cache_a_k[l]), _channel_major(cache_a_v[l]),
            ga, _bias_rows(rel_bias[l], 1 - t_sample, a_keep + t_sample - 1, lambda t: a_keep - t),
            batch=dec_batch, t=t_sample)
        ob = _attn_b_sample(
            qb, kb, vb, _channel_major(cache_b_k[l]),
            cache_b_v[l].reshape(dec_batch, past_len * B_HEADS, B_V_DIM),
            gb, lams, subg, batch=dec_batch, t=t_sample, lambda_init=lambda_init)
        ys = _out_proj(oa, ob, w_a, w_b, ys, fg, tm=rows_s, final_norm=last)
        outs[4].append(_position_major(ak_new, a_dims))
        outs[5].append(_position_major(av_new, a_dims))
        outs[6].append(kbf.reshape(dec_batch, t_sample, *bk_dims))
        outs[7].append(vbf.reshape(dec_batch, t_sample, *bv_dims))

    return (yp.reshape(batch, seq, d_model), ys.reshape(dec_batch, t_sample, d_model),
            *[jnp.stack(o) for o in outs])
```

```python
import functools
import math

import numpy as np
import jax
import jax.numpy as jnp
from jax import lax
from jax.experimental import pallas as pl
from jax.experimental.pallas import tpu as pltpu

F32 = jnp.float32
BF16 = jnp.bfloat16
NEG_INF = float("-inf")

CHUNK = 64
N_PREV_CHUNKS = 8
A_HEADS = 8
A_HEAD_DIM = 64
A_WIDTH = A_HEADS * A_HEAD_DIM
B_HEADS = 4
B_HEAD_DIM = 64
B_V_DIM = 2 * B_HEAD_DIM
B_WIDTH = B_HEADS * B_V_DIM
SEG = 512
MAX_REL = 128
ROPE_THETA = 500000.0
ROPE_DIM = 16
EPS = 1e-6
LOG2E = math.log2(math.e)
LANES = 128
VMEM_LIMIT = 56 * 1024 * 1024
ROW_TILE = 512
BAND_ROWS = N_PREV_CHUNKS * CHUNK
A_QB = 256
A_WIN = BAND_ROWS + A_QB
A_ROWS = 64
A_QK_SPLIT = 3
A_SUM_ROWS = 16
B_TQ = 512
B_TK = 512
B_HEADS_PER_STEP = 4
B_STRIP = 256
B_ROWS = 64
B_QK_SPLIT = 2
B_SUM_ROWS = 16

NT_DIMS = (((1,), (1,)), ((), ()))


def _nt_dot(a, b):
    return lax.dot_general(a, b, NT_DIMS, preferred_element_type=F32)


def _dot(a, b):
    return jnp.dot(a, b, preferred_element_type=F32)


def _silu(g):
    return g / (1.0 + jnp.exp(-g))


def _round_up(n, m):
    return -(-n // m) * m


def _rope_lane_freq():
    d = np.arange(LANES) % B_HEAD_DIM
    inv = ROPE_THETA ** (-np.arange(0, ROPE_DIM, 2, dtype=np.float64) / ROPE_DIM)
    return np.where(d < ROPE_DIM, inv[d % (ROPE_DIM // 2)], 0.0)


def _rope_tables(row_pos, base_pos):
    f = _rope_lane_freq()[None, :]
    ar = np.asarray(row_pos, np.float64)[:, None] * f
    ab = np.asarray(base_pos, np.float64)[:, None] * f
    as32 = lambda a: jnp.asarray(a.astype(np.float32))
    return (as32(np.cos(ar)), as32(np.sin(ar)),
            as32(np.cos(ab))[:, None, :], as32(np.sin(ab))[:, None, :])


def _bias_rows(rel_bias, t_min, t_max, rel_of_t, scale=1.0):
    width = _round_up(t_max - t_min + 1, LANES)
    t = np.arange(width)
    t = np.where(t <= t_max, t, t - width)
    idx = np.clip(rel_of_t(t), -MAX_REL, MAX_REL) + MAX_REL
    return rel_bias.astype(F32)[:, idx] * scale


def _toeplitz(row, n_rows):
    return pltpu.roll(jnp.broadcast_to(row, (n_rows, row.shape[-1])), 0, 1, stride=1, stride_axis=0)


def _proj_kernel(x_ref, g_ref, w_ref, cr_ref, sr_ref, cb_ref, sb_ref, *out_refs,
                 channel_major, q_scale):
    (qa_ref, ka_ref, va_ref, ga_ref, qb_ref, kb_ref, vb_ref, gb_ref,
     kaf_ref, vaf_ref, kbf_ref, vbf_ref) = out_refs
    x = x_ref[...]
    inv = lax.rsqrt(jnp.mean(x * x, axis=-1, keepdims=True) + EPS)
    h = (x * inv * g_ref[...]).astype(BF16)

    def seg(k):
        return _dot(h, w_ref[:, k * SEG:(k + 1) * SEG])

    cb, sb = cb_ref[0], sb_ref[0]
    cr, sr = cr_ref[...], sr_ref[...]
    cos = cb * cr - sb * sr
    sin = sb * cr + cb * sr
    d = lax.broadcasted_iota(jnp.int32, cos.shape, 1) % B_HEAD_DIM
    sin_lo = jnp.where(d < ROPE_DIM // 2, -sin, 0.0)
    sin_hi = jnp.where(d >= ROPE_DIM // 2, sin, 0.0)

    def rope(z):
        cols = []
        for c in range(SEG // LANES):
            zc = z[:, c * LANES:(c + 1) * LANES]
            up = pltpu.roll(zc, LANES - ROPE_DIM // 2, 1)
            dn = pltpu.roll(zc, ROPE_DIM // 2, 1)
            cols.append(zc * cos + up * sin_lo + dn * sin_hi)
        return jnp.concatenate(cols, axis=1)

    qa = seg(0) * q_scale
    qa_ref[...] = (qa.T if channel_major else qa).astype(BF16)
    ka = seg(1)
    ka_ref[...] = ka.astype(BF16)
    va = seg(2)
    va_t = va.T if channel_major else va
    va_ref[...] = va_t.astype(BF16)
    kaf_ref[...] = ka.T if channel_major else ka
    vaf_ref[...] = va_t

    ga_ref[...] = seg(3).astype(BF16)
    qb = rope(seg(4)) * q_scale
    qb_ref[...] = (qb.T if channel_major else qb).astype(BF16)
    kb = rope(seg(5))
    kbf_ref[...] = kb.T if channel_major else kb
    kb_ref[...] = kb.astype(BF16)
    vb = seg(6)
    n_col = SEG // LANES
    for c in range(n_col):
        vbf_ref[pl.ds(c, vb.shape[0], stride=n_col), :] = vb[:, c * LANES:(c + 1) * LANES]
    vb_ref[...] = (vb.T if channel_major else vb).astype(BF16)
    gb_ref[...] = seg(7).astype(BF16)


def _proj(x, gain, w_bf16, tables, *, tm, tiles_per_seq, tail_every, channel_major, q_scale):
    rows, d_model = x.shape
    n_tiles = rows // tm
    n_tail = n_tiles // tail_every
    n_seq = n_tiles // tiles_per_seq
    cr, sr, cb, sb = tables
    row_blk = lambda i: (i, 0)
    const2 = lambda i: (0, 0)
    base_blk = lambda i: (i % tiles_per_seq, 0, 0)
    bf = jax.ShapeDtypeStruct((rows, SEG), BF16)
    f32_full = jax.ShapeDtypeStruct((rows, SEG), F32)
    full_spec = pl.BlockSpec((tm, SEG), row_blk)
    if channel_major:
        assert tm == SEG
        tail = jax.ShapeDtypeStruct((n_tail, SEG, tm), F32)
        tail_spec = pl.BlockSpec((None, SEG, tm), lambda i: (i // tail_every, 0, 0))
        kbf = jax.ShapeDtypeStruct((n_seq, SEG, tiles_per_seq * tm), F32)
        kbf_spec = pl.BlockSpec((None, SEG, tm), lambda i: (i // tiles_per_seq, 0, i % tiles_per_seq))
        bf_t = jax.ShapeDtypeStruct((n_tiles, SEG, tm), BF16)
        bf_t_spec = pl.BlockSpec((None, SEG, tm), lambda i: (i, 0, 0))
    else:
        tail = jax.ShapeDtypeStruct((n_tail * tm, SEG), F32)
        tail_spec = pl.BlockSpec((tm, SEG), lambda i: (i // tail_every, 0))
        kbf, kbf_spec = f32_full, full_spec
        bf_t, bf_t_spec = bf, full_spec
    return pl.pallas_call(
        functools.partial(_proj_kernel, channel_major=channel_major,
                          q_scale=q_scale),
        grid=(n_tiles,),
        in_specs=[
            pl.BlockSpec((tm, d_model), row_blk),
            pl.BlockSpec((1, d_model), const2),
            pl.BlockSpec(w_bf16.shape, const2),
            pl.BlockSpec((tm, LANES), const2),
            pl.BlockSpec((tm, LANES), const2),
            pl.BlockSpec((1, 1, LANES), base_blk),
            pl.BlockSpec((1, 1, LANES), base_blk),
        ],
        out_specs=([bf_t_spec, full_spec, bf_t_spec, full_spec] * 2
                   + [tail_spec, tail_spec, kbf_spec,
                      pl.BlockSpec((tm * SEG // LANES, LANES), row_blk)]),
        out_shape=([bf_t, bf, bf_t, bf] * 2
                   + [tail, tail, kbf, jax.ShapeDtypeStruct((rows * SEG // LANES, LANES), F32)]),
        compiler_params=pltpu.CompilerParams(
            dimension_semantics=("arbitrary",), vmem_limit_bytes=VMEM_LIMIT),
        name="proj",
    )(x, gain, w_bf16, cr, sr, cb, sb)


def _attn_a_prompt_kernel(qt_ref, kp_ref, kc_ref, vtp_ref, vtc_ref, g_ref, brow_ref,
                          ob_lo_ref, ob_hi_ref, wa_ref, wb_ref, x_ref, fg_ref, y_ref,
                          bias_ref, oa_ref, sa_ref, sb_ref, sc_ref, *, final_norm, tiles):
    n = pl.program_id(1)

    @pl.when((pl.program_id(0) == 0) & (n == 0))
    def _():
        j = lax.broadcasted_iota(jnp.int32, (A_WIN, A_QB), 0)
        i = lax.broadcasted_iota(jnp.int32, (A_WIN, A_QB), 1)
        gap = i // CHUNK + N_PREV_CHUNKS - j // CHUNK
        band = (gap >= 0) & (gap <= N_PREV_CHUNKS)
        for hd in range(A_HEADS):
            t = _toeplitz(brow_ref[hd:hd + 1, :], A_WIN)[:, :A_QB]
            bias_ref[hd] = jnp.where(band, t, NEG_INF)

    items = [(q0, hd) for q0 in range(0, ROW_TILE, A_QB) for hd in range(A_HEADS)]
    part = A_WIN // A_QK_SPLIT

    def window(q0, first_tile):
        pieces = []
        for r in range(0, A_WIN, part):
            pos = q0 - BAND_ROWS + r
            if pos >= 0:
                pieces.append((1, pos, r))
            elif not first_tile:
                pieces.append((0, pos + ROW_TILE, r))
        return pieces

    def score_item(t, dst, first_tile):
        q0, hd = items[t]
        lanes = slice(hd // 2 * LANES, (hd // 2 + 1) * LANES)
        qt = qt_ref[lanes, q0:q0 + A_QB]
        row = lax.broadcasted_iota(jnp.int32, qt.shape, 0)
        rhs = jnp.where((row >= A_HEAD_DIM) == bool(hd % 2), qt, jnp.zeros_like(qt))
        for which, start, r in window(q0, first_tile):
            k_ref = (kp_ref, kc_ref)[which]
            dst[r:r + part, :] = (_dot(k_ref[start:start + part, lanes], rhs)
                                  + bias_ref[hd, r:r + part, :])

    def update_item(t, src, first_tile):
        q0, hd = items[t]
        pieces = window(q0, first_tile)
        chunks = [r + c for _, _, r in pieces for c in range(0, part, A_ROWS)]
        m = jnp.max(functools.reduce(jnp.maximum, [src[r:r + A_ROWS, :] for r in chunks]),
                    axis=0, keepdims=True)
        o = None
        for which, start, r in pieces:
            p = jnp.concatenate([jnp.exp2((src[r + c:r + c + A_ROWS, :] - m).astype(BF16))
                                 for c in range(0, part, A_ROWS)], axis=0)
            vt_ref = (vtp_ref, vtc_ref)[which]
            vt1 = jnp.concatenate(
                [vt_ref[hd * A_HEAD_DIM:(hd + 1) * A_HEAD_DIM, start:start + part],
                 jnp.ones((A_SUM_ROWS, part), BF16)], axis=0)
            contrib = _dot(vt1, p)
            o = contrib if o is None else o + contrib
        return o[:A_HEAD_DIM] / o[A_HEAD_DIM:A_HEAD_DIM + 1]

    def residual_rows(q0):
        rows = slice(q0, q0 + A_QB)
        ob = jnp.where(n < tiles // 2, ob_lo_ref[rows, :], ob_hi_ref[rows, :])
        y_ref[rows, :] = x_ref[rows, :] + _dot(ob, wb_ref[...])

    def run(first_tile):
        bufs = (sa_ref, sb_ref, sc_ref)
        depth = len(bufs) - 1
        for t in range(depth):
            score_item(t, bufs[t % len(bufs)], first_tile)
        for q0 in sorted({q0 for q0, _ in items})[:-1]:
            residual_rows(q0)
        o_even = None
        for t, (q0, hd) in enumerate(items):
            if t + depth < len(items):
                score_item(t + depth, bufs[(t + depth) % len(bufs)], first_tile)
            elif t + depth == len(items):
                residual_rows(items[-1][0])
            o_head = update_item(t, bufs[t % len(bufs)], first_tile)
            if hd % 2 == 0:
                o_even = o_head
            else:
                lanes = slice(hd // 2 * LANES, (hd // 2 + 1) * LANES)
                o_pair = jnp.concatenate([o_even, o_head], axis=0).T
                gate = g_ref[q0:q0 + A_QB, lanes].astype(F32)
                oa_ref[q0:q0 + A_QB, lanes] = (o_pair * _silu(gate)).astype(BF16)
            if hd == A_HEADS - 1:
                rows = slice(q0, q0 + A_QB)
                y = y_ref[rows, :] + _dot(oa_ref[rows, :], wa_ref[...])
                if final_norm:
                    y = y * lax.rsqrt(jnp.mean(y * y, axis=-1, keepdims=True) + EPS) * fg_ref[...]
                y_ref[rows, :] = y

    @pl.when(n == 0)
    def _():
        run(True)

    @pl.when(n > 0)
    def _():
        run(False)


def _attn_a_out_prompt(qat, ka, vat, ga, bias_rows, ob_lo, ob_hi, w_a, w_b, x, final_gain, *,
                       batch, seq, final_norm):
    tiles = seq // ROW_TILE
    assert tiles % 2 == 0
    half = tiles // 2
    d_model = x.shape[1]
    cur = lambda b, n: (b * tiles + n, 0)
    prev = lambda b, n: (b * tiles + jnp.maximum(n - 1, 0), 0)
    cur_t = lambda b, n: (b * tiles + n, 0, 0)
    prev_t = lambda b, n: (b * tiles + jnp.maximum(n - 1, 0), 0, 0)
    const2 = lambda b, n: (0, 0)
    blk = (ROW_TILE, A_WIDTH)
    blk_t = (None, A_WIDTH, ROW_TILE)
    return pl.pallas_call(
        functools.partial(_attn_a_prompt_kernel, final_norm=final_norm, tiles=tiles),
        grid=(batch, tiles),
        in_specs=[
            pl.BlockSpec(blk_t, cur_t),
            pl.BlockSpec(blk, prev), pl.BlockSpec(blk, cur),
            pl.BlockSpec(blk_t, prev_t), pl.BlockSpec(blk_t, cur_t),
            pl.BlockSpec(blk, cur),
            pl.BlockSpec(bias_rows.shape, const2),
            pl.BlockSpec((ROW_TILE, B_WIDTH), lambda b, n: (b * half + jnp.minimum(n, half - 1), 0)),
            pl.BlockSpec((ROW_TILE, B_WIDTH), lambda b, n: (b * half + jnp.maximum(n - half, 0), 0)),
            pl.BlockSpec(w_a.shape, const2),
            pl.BlockSpec(w_b.shape, const2),
            pl.BlockSpec((ROW_TILE, d_model), cur),
            pl.BlockSpec((1, d_model), const2),
        ],
        out_specs=pl.BlockSpec((ROW_TILE, d_model), cur),
        out_shape=jax.ShapeDtypeStruct(x.shape, F32),
        scratch_shapes=[pltpu.VMEM((A_HEADS, A_WIN, A_QB), F32),
                        pltpu.VMEM((ROW_TILE, A_WIDTH), BF16),
                        pltpu.VMEM((A_WIN, A_QB), F32),
                        pltpu.VMEM((A_WIN, A_QB), F32),
                        pltpu.VMEM((A_WIN, A_QB), F32)],
        compiler_params=pltpu.CompilerParams(
            dimension_semantics=("arbitrary", "arbitrary"), vmem_limit_bytes=VMEM_LIMIT),
        name="attn_a_out_prompt",
    )(qat, ka, ka, vat, vat, ga, bias_rows, ob_lo, ob_hi, w_a, w_b, x, final_gain)


def _diff_lambda(lq1_ref, lk1_ref, lq2_ref, lk2_ref, lambda_init):
    e1 = jnp.exp(jnp.sum(lq1_ref[...] * lk1_ref[...], axis=-1, keepdims=True))
    e2 = jnp.exp(jnp.sum(lq2_ref[...] * lk2_ref[...], axis=-1, keepdims=True))
    return e1 - e2 + lambda_init


def _stack_maps(qh):
    lane = lax.broadcasted_iota(jnp.int32, qh.shape, 1)
    zero = jnp.zeros_like(qh)
    return jnp.concatenate([jnp.where(lane < B_HEAD_DIM, qh, zero),
                            jnp.where(lane >= B_HEAD_DIM, qh, zero)], axis=0)


def _subnorm_gate(o, subg_ref, gate, lambda_init):
    o = o * lax.rsqrt(jnp.mean(o * o, axis=-1, keepdims=True) + EPS) * subg_ref[...]
    return (o * (1.0 - lambda_init)) * _silu(gate)


def _attn_b_prompt_kernel(qta_ref, qtb_ref, k_ref, vt_ref, ga_ref, gb_ref,
                          lq1_ref, lk1_ref, lq2_ref, lk2_ref, subg_ref, oa_ref, ob_ref,
                          rhs_ref, acc_ref, m_ref, sa_ref, sb_ref, *, lambda_init, tiles):
    g = pl.program_id(2)
    heads = rhs_ref.shape[1]
    for t, qt_ref in enumerate((qta_ref, qtb_ref)):
        for h in range(heads):
            qt = qt_ref[h * B_V_DIM:(h + 1) * B_V_DIM, :]
            row = lax.broadcasted_iota(jnp.int32, qt.shape, 0)
            zero = jnp.zeros_like(qt)
            rhs_ref[t, h, :, :B_TQ] = jnp.where(row < B_HEAD_DIM, qt, zero)
            rhs_ref[t, h, :, B_TQ:] = jnp.where(row >= B_HEAD_DIM, qt, zero)
    m_ref[...] = jnp.full(m_ref.shape, NEG_INF, F32)
    acc_ref[...] = jnp.zeros(acc_ref.shape, F32)

    items = [(h, c) for h in range(heads) for c in range(2 * B_TQ // B_STRIP)]

    def live_keys(c, diagonal):
        q0 = (c * B_STRIP) % B_TQ
        return min(B_TK, q0 + B_STRIP) if diagonal else B_TK

    def score_item(tile, j, n, dst, diagonal):
        h, c = items[n]
        part = B_TK // B_QK_SPLIT
        for r in range(0, live_keys(c, diagonal), part):
            k0 = pl.multiple_of(j * B_TK + r, part)
            dst[n, r:r + part, :] = _dot(k_ref[pl.ds(k0, part), h * B_V_DIM:(h + 1) * B_V_DIM],
                                         rhs_ref[tile, h, :, c * B_STRIP:(c + 1) * B_STRIP])

    def update_item(tile, j, n, src, diagonal):
        h, c = items[n]
        hs = slice(h * B_V_DIM, (h + 1) * B_V_DIM)
        cols = slice(c * B_STRIP, (c + 1) * B_STRIP)
        live = live_keys(c, diagonal)

        def chunk(r):
            s = src[n, r:r + B_ROWS, :]
            if diagonal:
                q0 = (c * B_STRIP) % B_TQ
                kc = (r + lax.broadcasted_iota(jnp.int32, s.shape, 0)) // CHUNK
                qc = (q0 + lax.broadcasted_iota(jnp.int32, s.shape, 1)) // CHUNK
                s = jnp.where(kc <= qc, s, NEG_INF)
            return s

        rows = range(0, live, B_ROWS)
        m_blk = functools.reduce(jnp.maximum, [chunk(r) for r in rows])
        m_old = m_ref[tile, h, :, cols]
        m_new = jnp.maximum(m_old, jnp.max(m_blk, axis=0, keepdims=True))
        alpha = jnp.exp2(m_old - m_new)
        m_ref[tile, h, :, cols] = m_new
        p = jnp.concatenate([jnp.exp2((chunk(r) - m_new).astype(BF16)) for r in rows], axis=0)
        vt1 = jnp.concatenate([vt_ref[j, hs, :live], jnp.ones((B_SUM_ROWS, live), BF16)], axis=0)
        acc_ref[tile, h, :, cols] = alpha * acc_ref[tile, h, :, cols] + _dot(vt1, p)

    def stage(score, dst, update, src, score_diagonal=False, update_diagonal=False):
        for n in range(len(items)):
            if score is not None:
                score_item(*score, n, dst, score_diagonal)
            if update is not None:
                update_item(*update, n, src, update_diagonal)

    buf_a, buf_b = sa_ref, sb_ref
    first = (0, g)
    last = (1, tiles - 1 - g)

    def at(s):
        in_lo = s <= g
        return jnp.where(in_lo, 0, 1), jnp.where(in_lo, s - 1, s - 1 - g)

    def pair(p, carry):
        s = 2 * p + 1
        stage(at(s + 1), buf_a, at(s), buf_b)
        stage(at(s + 2), buf_b, at(s + 1), buf_a)
        return carry

    stage(first, buf_a, None, None, score_diagonal=True)
    stage(at(1), buf_b, first, buf_a, update_diagonal=True)
    lax.fori_loop(0, (tiles - 2) // 2, pair, 0)

    lam = _diff_lambda(lq1_ref, lk1_ref, lq2_ref, lk2_ref, lambda_init)

    def finish(t, g_ref, o_ref):
        for h in range(heads):
            hs = slice(h * B_V_DIM, (h + 1) * B_V_DIM)
            o = acc_ref[t, h, :B_V_DIM, :] / acc_ref[t, h, B_V_DIM:B_V_DIM + 1, :]
            o = (o[:, :B_TQ] - lam * o[:, B_TQ:]).T
            o_ref[:, hs] = _subnorm_gate(o, subg_ref, g_ref[:, hs].astype(F32),
                                         lambda_init).astype(BF16)

    finish(0, ga_ref, oa_ref)
    stage(last, buf_a, (1, tiles - 2 - g), buf_b, score_diagonal=True)
    stage(None, None, last, buf_a, update_diagonal=True)
    finish(1, gb_ref, ob_ref)


def _attn_b_prompt(qbt, kb, vbt, gb, lams, subg, *, batch, seq, lambda_init):
    assert B_TQ == B_TK == ROW_TILE and B_HEADS % B_HEADS_PER_STEP == 0
    tiles = seq // B_TQ
    assert tiles % 2 == 0
    half = tiles // 2
    width = B_HEADS_PER_STEP * B_V_DIM
    n_items = B_HEADS_PER_STEP * 2 * B_TQ // B_STRIP
    lo = lambda b, g: b * tiles + g
    hi = lambda b, g: b * tiles + tiles - 1 - g
    vec = pl.BlockSpec((1, B_HEAD_DIM), lambda b, h, g: (0, 0))
    vbt = vbt.reshape(batch, tiles, B_WIDTH, B_TK)
    out = jax.ShapeDtypeStruct((batch * half * B_TQ, B_WIDTH), BF16)
    return pl.pallas_call(
        functools.partial(_attn_b_prompt_kernel, lambda_init=lambda_init, tiles=tiles),
        grid=(batch, B_HEADS // B_HEADS_PER_STEP, half),
        in_specs=[
            pl.BlockSpec((None, width, B_TQ), lambda b, h, g: (lo(b, g), h, 0)),
            pl.BlockSpec((None, width, B_TQ), lambda b, h, g: (hi(b, g), h, 0)),
            pl.BlockSpec((seq, width), lambda b, h, g: (b, h), pipeline_mode=pl.Buffered(1)),
            pl.BlockSpec((None, tiles, width, B_TK), lambda b, h, g: (b, 0, h, 0),
                         pipeline_mode=pl.Buffered(1)),
            pl.BlockSpec((B_TQ, width), lambda b, h, g: (lo(b, g), h)),
            pl.BlockSpec((B_TQ, width), lambda b, h, g: (hi(b, g), h)),
            vec, vec, vec, vec,
            pl.BlockSpec((1, B_V_DIM), lambda b, h, g: (0, 0)),
        ],
        out_specs=[pl.BlockSpec((B_TQ, width), lambda b, h, g: (b * half + g, h)),
                   pl.BlockSpec((B_TQ, width), lambda b, h, g: (b * half + half - 1 - g, h))],
        out_shape=[out, out],
        scratch_shapes=[pltpu.VMEM((2, B_HEADS_PER_STEP, B_V_DIM, 2 * B_TQ), BF16),
                        pltpu.VMEM((2, B_HEADS_PER_STEP, B_V_DIM + B_SUM_ROWS, 2 * B_TQ), F32),
                        pltpu.VMEM((2, B_HEADS_PER_STEP, 1, 2 * B_TQ), F32),
                        pltpu.VMEM((n_items, B_TK, B_STRIP), F32),
                        pltpu.VMEM((n_items, B_TK, B_STRIP), F32)],
        compiler_params=pltpu.CompilerParams(
            dimension_semantics=("arbitrary", "arbitrary", "arbitrary"),
            vmem_limit_bytes=VMEM_LIMIT),
        name="attn_b_prompt",
    )(qbt, qbt, kb, vbt, gb, gb, *lams, subg)


def _roll_in(cache_t, new_rows):
    t = new_rows.shape[0]
    keep = cache_t.shape[1]
    shifted = pltpu.roll(cache_t, keep - t, 1)
    pad = jnp.concatenate([jnp.zeros((LANES - t, new_rows.shape[1]), F32), new_rows], axis=0)
    new_t = pad.T
    lane = lax.broadcasted_iota(jnp.int32, new_t.shape, 1)
    last = jnp.where(lane >= LANES - t, new_t, shifted[:, keep - LANES:])
    return jnp.concatenate([shifted[:, :keep - LANES], last], axis=1)


def _attn_a_sample_kernel(q_ref, kn_ref, vn_ref, knf_ref, vnf_ref, kc_ref, vc_ref, g_ref,
                          brow_ref, o_ref, ko_ref, vo_ref):
    t = q_ref.shape[0]
    keep = kc_ref.shape[1]
    lane = lax.broadcasted_iota(jnp.int32, (t, LANES), 1)
    low_half = lane < A_HEAD_DIM
    pair_cols = [slice(pair * LANES, (pair + 1) * LANES) for pair in range(A_HEADS // 2)]
    scores = []
    for hd in range(A_HEADS):
        cs = pair_cols[hd // 2]
        qp = q_ref[:, cs]
        bias = _toeplitz(brow_ref[hd:hd + 1, :], t)
        qm = jnp.where(low_half if hd % 2 == 0 else ~low_half, qp, jnp.zeros_like(qp))
        scores.append((_dot(qm, kc_ref[cs, :].astype(BF16)) + bias[:, :keep],
                       _nt_dot(qm, kn_ref[:, cs]) + bias[:, keep:keep + t]))
    probs = []
    for s_c, s_n in scores:
        m = jnp.maximum(jnp.max(s_c, axis=-1, keepdims=True), jnp.max(s_n, axis=-1, keepdims=True))
        p_c = jnp.exp(s_c - m)
        p_n = jnp.exp(s_n - m)
        l = jnp.sum(p_c, axis=-1, keepdims=True) + jnp.sum(p_n, axis=-1, keepdims=True)
        probs.append((p_c.astype(BF16), p_n.astype(BF16), l))
    outs = []
    for hd, (p_c, p_n, l) in enumerate(probs):
        cs = pair_cols[hd // 2]
        o = _nt_dot(p_c, vc_ref[cs, :].astype(BF16)) + _dot(p_n, vn_ref[:, cs])
        outs.append(o / l)
    for pair, cs in enumerate(pair_cols):
        o_pair = jnp.where(low_half, outs[2 * pair], outs[2 * pair + 1])
        o_ref[:, cs] = (o_pair * _silu(g_ref[:, cs].astype(F32))).astype(BF16)
    ko_ref[...] = _roll_in(kc_ref[...], knf_ref[...])
    vo_ref[...] = _roll_in(vc_ref[...], vnf_ref[...])


def _attn_a_sample(qa, ka, va, kaf, vaf, cache_kt, cache_vt, ga, bias_rows, *, batch, t):
    keep = cache_kt.shape[2]
    new = pl.BlockSpec((t, A_WIDTH), lambda b: (b, 0))
    cache = pl.BlockSpec((None, A_WIDTH, keep), lambda b: (b, 0, 0))
    return pl.pallas_call(
        _attn_a_sample_kernel,
        grid=(batch,),
        in_specs=[new, new, new, new, new, cache, cache, new,
                  pl.BlockSpec(bias_rows.shape, lambda b: (0, 0))],
        out_specs=[new, cache, cache],
        out_shape=[jax.ShapeDtypeStruct(qa.shape, BF16),
                   jax.ShapeDtypeStruct(cache_kt.shape, F32),
                   jax.ShapeDtypeStruct(cache_vt.shape, F32)],
        compiler_params=pltpu.CompilerParams(
            dimension_semantics=("arbitrary",), vmem_limit_bytes=VMEM_LIMIT),
        name="attn_a_sample",
    )(qa, ka, va, kaf, vaf, cache_kt, cache_vt, ga, bias_rows)


def _attn_b_sample_kernel(q_ref, kn_ref, vn_ref, kc_ref, vc_ref, g_ref,
                          lq1_ref, lk1_ref, lq2_ref, lk2_ref, subg_ref, o_ref, *, lambda_init):
    t = q_ref.shape[0]
    past = kc_ref.shape[1]
    lam = _diff_lambda(lq1_ref, lk1_ref, lq2_ref, lk2_ref, lambda_init)
    head_cols = [slice(h * B_V_DIM, (h + 1) * B_V_DIM) for h in range(B_HEADS)]
    scores = []
    for cs in head_cols:
        qs = _stack_maps(q_ref[:, cs])
        scores.append((_dot(qs, kc_ref[cs, :].astype(BF16)), _nt_dot(qs, kn_ref[:, cs])))
    attn = []
    for s_c, s_n in scores:
        m = jnp.maximum(jnp.max(s_c, axis=-1, keepdims=True),
                        jnp.max(s_n, axis=-1, keepdims=True))
        p_c = jnp.exp(s_c - m)
        p_n = jnp.exp(s_n - m)
        l = jnp.sum(p_c, axis=-1, keepdims=True) + jnp.sum(p_n, axis=-1, keepdims=True)
        p_c = p_c / l
        p_n = p_n / l
        attn.append(((p_c[:t] - lam * p_c[t:]).astype(BF16),
                     (p_n[:t] - lam * p_n[t:]).astype(BF16)))
    for h, (cs, (a_c, a_n)) in enumerate(zip(head_cols, attn)):
        vc = vc_ref[pl.ds(h, past, stride=B_HEADS), :].astype(BF16)
        o = _dot(a_c, vc) + _dot(a_n, vn_ref[:, cs])
        o_ref[:, cs] = _subnorm_gate(o, subg_ref, g_ref[:, cs].astype(F32),
                                     lambda_init).astype(BF16)


def _attn_b_sample(qb, kb, vb, cache_kt, cache_v, gb, lams, subg, *, batch, t, lambda_init):
    past = cache_kt.shape[2]
    new = pl.BlockSpec((t, B_WIDTH), lambda b: (b, 0))
    vec = pl.BlockSpec((1, B_HEAD_DIM), lambda b: (0, 0))
    return pl.pallas_call(
        functools.partial(_attn_b_sample_kernel, lambda_init=lambda_init),
        grid=(batch,),
        in_specs=[new, new, new,
                  pl.BlockSpec((None, B_WIDTH, past), lambda b: (b, 0, 0)),
                  pl.BlockSpec((None, past * B_HEADS, B_V_DIM), lambda b: (b, 0, 0)),
                  new, vec, vec, vec, vec,
                  pl.BlockSpec((1, B_V_DIM), lambda b: (0, 0))],
        out_specs=new,
        out_shape=jax.ShapeDtypeStruct(qb.shape, BF16),
        compiler_params=pltpu.CompilerParams(
            dimension_semantics=("arbitrary",), vmem_limit_bytes=VMEM_LIMIT),
        name="attn_b_sample",
    )(qb, kb, vb, cache_kt, cache_v, gb, *lams, subg)


def _out_kernel(oa_ref, ob_ref, wa_ref, wb_ref, x_ref, fg_ref, y_ref, *, final_norm):
    y = x_ref[...] + _dot(oa_ref[...], wa_ref[...]) + _dot(ob_ref[...], wb_ref[...])
    if final_norm:
        y = y * lax.rsqrt(jnp.mean(y * y, axis=-1, keepdims=True) + EPS) * fg_ref[...]
    y_ref[...] = y


def _out_proj(oa, ob, w_a, w_b, x, final_gain, *, tm, final_norm):
    rows, d_model = x.shape
    row_blk = lambda i: (i, 0)
    const2 = lambda i: (0, 0)
    return pl.pallas_call(
        functools.partial(_out_kernel, final_norm=final_norm),
        grid=(rows // tm,),
        in_specs=[
            pl.BlockSpec((tm, A_WIDTH), row_blk),
            pl.BlockSpec((tm, B_WIDTH), row_blk),
            pl.BlockSpec(w_a.shape, const2),
            pl.BlockSpec(w_b.shape, const2),
            pl.BlockSpec((tm, d_model), row_blk),
            pl.BlockSpec((1, d_model), const2),
        ],
        out_specs=pl.BlockSpec((tm, d_model), row_blk),
        out_shape=jax.ShapeDtypeStruct(x.shape, F32),
        compiler_params=pltpu.CompilerParams(
            dimension_semantics=("arbitrary",), vmem_limit_bytes=VMEM_LIMIT),
        name="out_proj",
    )(oa, ob, w_a, w_b, x, final_gain)


def _channel_major(a):
    n, pos = a.shape[:2]
    return jnp.moveaxis(a.reshape(n, pos, -1), 1, 2)


def _position_major(a, channel_dims):
    n, _, pos = a.shape
    return jnp.moveaxis(a, 2, 1).reshape(n, pos, *channel_dims)


def kernel(x_prompt, x_sample, cache_a_k, cache_a_v, cache_b_k, cache_b_v,
           norm_gain, w_in, w_out, rel_bias, lambda_q1, lambda_k1, lambda_q2, lambda_k2,
           subln_gain, final_gain):
    batch, seq, d_model = x_prompt.shape
    dec_batch, t_sample, _ = x_sample.shape
    depth = w_in.shape[0]
    past_len = cache_b_k.shape[2]
    a_keep = cache_a_k.shape[2]
    keep_prompt = min(BAND_ROWS, seq)
    assert seq % ROW_TILE == 0 and keep_prompt == ROW_TILE and seq % B_TQ == 0
    assert t_sample <= a_keep and t_sample <= LANES and d_model == 2 * SEG

    rows_p = batch * seq
    rows_s = dec_batch * t_sample
    tiles_per_seq = seq // ROW_TILE
    tables_p = _rope_tables(np.arange(ROW_TILE), np.arange(tiles_per_seq) * ROW_TILE)
    tables_s = _rope_tables(past_len + np.arange(rows_s) % t_sample, np.zeros(1))
    a_dims = (A_HEADS, A_HEAD_DIM)
    bk_dims = (B_HEADS, 2, B_HEAD_DIM)
    bv_dims = (B_HEADS, B_V_DIM)

    yp = x_prompt.reshape(rows_p, d_model)
    ys = x_sample.reshape(rows_s, d_model)
    fg = final_gain.reshape(1, d_model)
    outs = [[] for _ in range(8)]
    for l in range(depth):
        lambda_init = 0.8 - 0.6 * math.exp(-0.3 * l)
        last = l == depth - 1
        w_l = w_in[l].astype(BF16)
        w_a = w_out[l, :A_WIDTH].astype(BF16)
        w_b = w_out[l, A_WIDTH:].astype(BF16)
        gain = norm_gain[l].reshape(1, d_model)
        lams = [a[l].reshape(1, B_HEAD_DIM).astype(F32)
                for a in (lambda_q1, lambda_k1, lambda_q2, lambda_k2)]
        subg = subln_gain[l].reshape(1, B_V_DIM).astype(F32)

        qa, ka, va, ga, qb, kb, vb, gb, kaf, vaf, kbf, vbf = _proj(
            yp, gain, w_l, tables_p, tm=ROW_TILE, tiles_per_seq=tiles_per_seq,
            tail_every=tiles_per_seq, channel_major=True, q_scale=LOG2E * B_HEAD_DIM ** -0.5)
        bias_p = _bias_rows(rel_bias[l], 1 - A_WIN, A_QB - 1, lambda t: t + BAND_ROWS, LOG2E)
        ob_lo, ob_hi = _attn_b_prompt(qb, kb, vb, gb, lams, subg, batch=batch, seq=seq,
                                      lambda_init=lambda_init)
        yp = _attn_a_out_prompt(qa, ka, va, ga, bias_p, ob_lo, ob_hi, w_a, w_b, yp, fg,
                                batch=batch, seq=seq, final_norm=last)
        outs[0].append(_position_major(kaf, a_dims))
        outs[1].append(_position_major(vaf, a_dims))
        outs[2].append(_position_major(kbf, bk_dims))
        outs[3].append(vbf.reshape(batch, seq, *bv_dims))

        qa, ka, va, ga, qb, kb, vb, gb, kaf, vaf, kbf, vbf = _proj(
            ys, gain, w_l, tables_s, tm=rows_s, tiles_per_seq=1, tail_every=1,
            channel_major=False, q_scale=B_HEAD_DIM ** -0.5)
        oa, ak_new, av_new = _attn_a_sample(
            qa, ka, va, kaf, vaf, _channel_major(cache_a_k[l]), _channel_major(cache_a_v[l]),
            ga, _bias_rows(rel_bias[l], 1 - t_sample, a_keep + t_sample - 1, lambda t: a_keep - t),
            batch=dec_batch, t=t_sample)
        ob = _attn_b_sample(
            qb, kb, vb, _channel_major(cache_b_k[l]),
            cache_b_v[l].reshape(dec_batch, past_len * B_HEADS, B_V_DIM),
            gb, lams, subg, batch=dec_batch, t=t_sample, lambda_init=lambda_init)
        ys = _out_proj(oa, ob, w_a, w_b, ys, fg, tm=rows_s, final_norm=last)
        outs[4].append(_position_major(ak_new, a_dims))
        outs[5].append(_position_major(av_new, a_dims))
        outs[6].append(kbf.reshape(dec_batch, t_sample, *bk_dims))
        outs[7].append(vbf.reshape(dec_batch, t_sample, *bv_dims))

    return (yp.reshape(batch, seq, d_model), ys.reshape(dec_batch, t_sample, d_model),
            *[jnp.stack(o) for o in outs])
```

```python
import functools
import math

import numpy as np
import jax
import jax.numpy as jnp
from jax import lax
from jax.experimental import pallas as pl
from jax.experimental.pallas import tpu as pltpu

F32 = jnp.float32
BF16 = jnp.bfloat16
NEG_INF = float("-inf")

CHUNK = 64
N_PREV_CHUNKS = 8
A_HEADS = 8
A_HEAD_DIM = 64
A_WIDTH = A_HEADS * A_HEAD_DIM
B_HEADS = 4
B_HEAD_DIM = 64
B_V_DIM = 2 * B_HEAD_DIM
B_WIDTH = B_HEADS * B_V_DIM
SEG = 512
MAX_REL = 128
ROPE_THETA = 500000.0
ROPE_DIM = 16
EPS = 1e-6
LOG2E = math.log2(math.e)
LANES = 128
MXU_TILE = 256
BF16_SUBLANES = 16
VMEM_LIMIT = 56 * 1024 * 1024
ROW_TILE = 512
BAND_ROWS = N_PREV_CHUNKS * CHUNK
A_QB = MXU_TILE
A_WIN = BAND_ROWS + A_QB
A_ROWS = 64
A_QK_SPLIT = 3
A_SUM_ROWS = BF16_SUBLANES
B_TQ = 512
B_TK = 512
B_HEADS_PER_STEP = 4
B_STRIP = MXU_TILE
B_ROWS = 64
B_QK_SPLIT = 2
B_SUM_ROWS = BF16_SUBLANES

NT_DIMS = (((1,), (1,)), ((), ()))


def _nt_dot(a, b):
    return lax.dot_general(a, b, NT_DIMS, preferred_element_type=F32)


def _dot(a, b):
    return jnp.dot(a, b, preferred_element_type=F32)


def _silu(g):
    return g / (1.0 + jnp.exp(-g))


def _round_up(n, m):
    return -(-n // m) * m


def _rope_lane_freq():
    d = np.arange(LANES) % B_HEAD_DIM
    inv = ROPE_THETA ** (-np.arange(0, ROPE_DIM, 2, dtype=np.float64) / ROPE_DIM)
    return np.where(d < ROPE_DIM, inv[d % (ROPE_DIM // 2)], 0.0)


def _rope_tables(row_pos, base_pos):
    f = _rope_lane_freq()[None, :]
    ar = np.asarray(row_pos, np.float64)[:, None] * f
    ab = np.asarray(base_pos, np.float64)[:, None] * f
    as32 = lambda a: jnp.asarray(a.astype(np.float32))
    return (as32(np.cos(ar)), as32(np.sin(ar)),
            as32(np.cos(ab))[:, None, :], as32(np.sin(ab))[:, None, :])


def _bias_rows(rel_bias, t_min, t_max, rel_of_t, scale=1.0):
    width = _round_up(t_max - t_min + 1, LANES)
    t = np.arange(width)
    t = np.where(t <= t_max, t, t - width)
    idx = np.clip(rel_of_t(t), -MAX_REL, MAX_REL) + MAX_REL
    return rel_bias.astype(F32)[:, idx] * scale


def _toeplitz(row, n_rows):
    return pltpu.roll(jnp.broadcast_to(row, (n_rows, row.shape[-1])), 0, 1, stride=1, stride_axis=0)


def _proj_kernel(x_ref, g_ref, w_ref, cr_ref, sr_ref, cb_ref, sb_ref, *out_refs,
                 channel_major, q_scale):
    (qa_ref, ka_ref, va_ref, ga_ref, qb_ref, kb_ref, vb_ref, gb_ref,
     kaf_ref, vaf_ref, kbf_ref, vbf_ref) = out_refs
    x = x_ref[...]
    inv = lax.rsqrt(jnp.mean(x * x, axis=-1, keepdims=True) + EPS)
    h = (x * inv * g_ref[...]).astype(BF16)

    def seg(k):
        return _dot(h, w_ref[:, k * SEG:(k + 1) * SEG])

    cb, sb = cb_ref[0], sb_ref[0]
    cr, sr = cr_ref[...], sr_ref[...]
    cos = cb * cr - sb * sr
    sin = sb * cr + cb * sr
    d = lax.broadcasted_iota(jnp.int32, cos.shape, 1) % B_HEAD_DIM
    sin_lo = jnp.where(d < ROPE_DIM // 2, -sin, 0.0)
    sin_hi = jnp.where(d >= ROPE_DIM // 2, sin, 0.0)

    def rope(z):
        cols = []
        for c in range(SEG // LANES):
            zc = z[:, c * LANES:(c + 1) * LANES]
            up = pltpu.roll(zc, LANES - ROPE_DIM // 2, 1)
            dn = pltpu.roll(zc, ROPE_DIM // 2, 1)
            cols.append(zc * cos + up * sin_lo + dn * sin_hi)
        return jnp.concatenate(cols, axis=1)

    qa = seg(0) * q_scale
    qa_ref[...] = (qa.T if channel_major else qa).astype(BF16)
    ka = seg(1)
    ka_ref[...] = ka.astype(BF16)
    va = seg(2)
    va_t = va.T if channel_major else va
    va_ref[...] = va_t.astype(BF16)
    kaf_ref[...] = ka.T if channel_major else ka
    vaf_ref[...] = va_t

    ga_ref[...] = seg(3).astype(BF16)
    qb = rope(seg(4)) * q_scale
    qb_ref[...] = (qb.T if channel_major else qb).astype(BF16)
    kb = rope(seg(5))
    kbf_ref[...] = kb.T if channel_major else kb
    kb_ref[...] = kb.astype(BF16)
    vb = seg(6)
    n_col = SEG // LANES
    for c in range(n_col):
        vbf_ref[pl.ds(c, vb.shape[0], stride=n_col), :] = vb[:, c * LANES:(c + 1) * LANES]
    vb_ref[...] = (vb.T if channel_major else vb).astype(BF16)
    gb_ref[...] = seg(7).astype(BF16)


def _proj(x, gain, w_bf16, tables, *, tm, tiles_per_seq, tail_every, channel_major, q_scale):
    rows, d_model = x.shape
    n_tiles = rows // tm
    n_tail = n_tiles // tail_every
    n_seq = n_tiles // tiles_per_seq
    cr, sr, cb, sb = tables
    row_blk = lambda i: (i, 0)
    const2 = lambda i: (0, 0)
    base_blk = lambda i: (i % tiles_per_seq, 0, 0)
    bf = jax.ShapeDtypeStruct((rows, SEG), BF16)
    f32_full = jax.ShapeDtypeStruct((rows, SEG), F32)
    full_spec = pl.BlockSpec((tm, SEG), row_blk)
    if channel_major:
        assert tm == SEG
        tail = jax.ShapeDtypeStruct((n_tail, SEG, tm), F32)
        tail_spec = pl.BlockSpec((None, SEG, tm), lambda i: (i // tail_every, 0, 0))
        kbf = jax.ShapeDtypeStruct((n_seq, SEG, tiles_per_seq * tm), F32)
        kbf_spec = pl.BlockSpec((None, SEG, tm), lambda i: (i // tiles_per_seq, 0, i % tiles_per_seq))
        bf_t = jax.ShapeDtypeStruct((n_tiles, SEG, tm), BF16)
        bf_t_spec = pl.BlockSpec((None, SEG, tm), lambda i: (i, 0, 0))
    else:
        tail = jax.ShapeDtypeStruct((n_tail * tm, SEG), F32)
        tail_spec = pl.BlockSpec((tm, SEG), lambda i: (i // tail_every, 0))
        kbf, kbf_spec = f32_full, full_spec
        bf_t, bf_t_spec = bf, full_spec
    return pl.pallas_call(
        functools.partial(_proj_kernel, channel_major=channel_major,
                          q_scale=q_scale),
        grid=(n_tiles,),
        in_specs=[
            pl.BlockSpec((tm, d_model), row_blk),
            pl.BlockSpec((1, d_model), const2),
            pl.BlockSpec(w_bf16.shape, const2),
            pl.BlockSpec((tm, LANES), const2),
            pl.BlockSpec((tm, LANES), const2),
            pl.BlockSpec((1, 1, LANES), base_blk),
            pl.BlockSpec((1, 1, LANES), base_blk),
        ],
        out_specs=([bf_t_spec, full_spec, bf_t_spec, full_spec] * 2
                   + [tail_spec, tail_spec, kbf_spec,
                      pl.BlockSpec((tm * SEG // LANES, LANES), row_blk)]),
        out_shape=([bf_t, bf, bf_t, bf] * 2
                   + [tail, tail, kbf, jax.ShapeDtypeStruct((rows * SEG // LANES, LANES), F32)]),
        compiler_params=pltpu.CompilerParams(
            dimension_semantics=("arbitrary",), vmem_limit_bytes=VMEM_LIMIT),
        name="proj",
    )(x, gain, w_bf16, cr, sr, cb, sb)


def _attn_a_prompt_kernel(qt_ref, kp_ref, kc_ref, vtp_ref, vtc_ref, g_ref, brow_ref,
                          ob_lo_ref, ob_hi_ref, wa_ref, wb_ref, x_ref, fg_ref, y_ref,
                          bias_ref, oa_ref, sa_ref, sb_ref, sc_ref, sd_ref, *, final_norm, tiles):
    n = pl.program_id(1)

    @pl.when((pl.program_id(0) == 0) & (n == 0))
    def _():
        j = lax.broadcasted_iota(jnp.int32, (A_WIN, A_QB), 0)
        i = lax.broadcasted_iota(jnp.int32, (A_WIN, A_QB), 1)
        gap = i // CHUNK + N_PREV_CHUNKS - j // CHUNK
        band = (gap >= 0) & (gap <= N_PREV_CHUNKS)
        for hd in range(A_HEADS):
            t = _toeplitz(brow_ref[hd:hd + 1, :], A_WIN)[:, :A_QB]
            bias_ref[hd] = jnp.where(band, t, NEG_INF)

    items = [(q0, hd) for q0 in range(0, ROW_TILE, A_QB) for hd in range(A_HEADS)]
    part = A_WIN // A_QK_SPLIT

    def window(q0, first_tile):
        pieces = []
        for r in range(0, A_WIN, part):
            pos = q0 - BAND_ROWS + r
            if pos >= 0:
                pieces.append((1, pos, r))
            elif not first_tile:
                pieces.append((0, pos + ROW_TILE, r))
        return pieces

    def score_item(t, dst, first_tile):
        q0, hd = items[t]
        lanes = slice(hd // 2 * LANES, (hd // 2 + 1) * LANES)
        qt = qt_ref[lanes, q0:q0 + A_QB]
        row = lax.broadcasted_iota(jnp.int32, qt.shape, 0)
        rhs = jnp.where((row >= A_HEAD_DIM) == bool(hd % 2), qt, jnp.zeros_like(qt))
        for which, start, r in window(q0, first_tile):
            k_ref = (kp_ref, kc_ref)[which]
            dst[r:r + part, :] = (_dot(k_ref[start:start + part, lanes], rhs)
                                  + bias_ref[hd, r:r + part, :])

    def update_item(t, src, first_tile):
        q0, hd = items[t]
        pieces = window(q0, first_tile)
        chunks = [r + c for _, _, r in pieces for c in range(0, part, A_ROWS)]
        m = jnp.max(functools.reduce(jnp.maximum, [src[r:r + A_ROWS, :] for r in chunks]),
                    axis=0, keepdims=True)
        o = None
        for which, start, r in pieces:
            p = jnp.concatenate([jnp.exp2((src[r + c:r + c + A_ROWS, :] - m).astype(BF16))
                                 for c in range(0, part, A_ROWS)], axis=0)
            vt_ref = (vtp_ref, vtc_ref)[which]
            vt1 = jnp.concatenate(
                [vt_ref[hd * A_HEAD_DIM:(hd + 1) * A_HEAD_DIM, start:start + part],
                 jnp.ones((A_SUM_ROWS, part), BF16)], axis=0)
            contrib = _dot(vt1, p)
            o = contrib if o is None else o + contrib
        return o[:A_HEAD_DIM] / o[A_HEAD_DIM:A_HEAD_DIM + 1]

    def residual_rows(q0):
        rows = slice(q0, q0 + A_QB)
        ob = jnp.where(n < tiles // 2, ob_lo_ref[rows, :], ob_hi_ref[rows, :])
        y_ref[rows, :] = x_ref[rows, :] + _dot(ob, wb_ref[...])

    def run(first_tile):
        bufs = (sa_ref, sb_ref, sc_ref, sd_ref)
        depth = len(bufs) - 1
        for t in range(depth):
            score_item(t, bufs[t % len(bufs)], first_tile)
        for q0 in sorted({q0 for q0, _ in items})[:-1]:
            residual_rows(q0)
        o_even = None
        for t, (q0, hd) in enumerate(items):
            if t + depth < len(items):
                score_item(t + depth, bufs[(t + depth) % len(bufs)], first_tile)
            elif t + depth == len(items):
                residual_rows(items[-1][0])
            o_head = update_item(t, bufs[t % len(bufs)], first_tile)
            if hd % 2 == 0:
                o_even = o_head
            else:
                lanes = slice(hd // 2 * LANES, (hd // 2 + 1) * LANES)
                o_pair = jnp.concatenate([o_even, o_head], axis=0).T
                gate = g_ref[q0:q0 + A_QB, lanes].astype(F32)
                oa_ref[q0:q0 + A_QB, lanes] = (o_pair * _silu(gate)).astype(BF16)
            if hd == A_HEADS - 1:
                rows = slice(q0, q0 + A_QB)
                y = y_ref[rows, :] + _dot(oa_ref[rows, :], wa_ref[...])
                if final_norm:
                    y = y * lax.rsqrt(jnp.mean(y * y, axis=-1, keepdims=True) + EPS) * fg_ref[...]
                y_ref[rows, :] = y

    @pl.when(n == 0)
    def _():
        run(True)

    @pl.when(n > 0)
    def _():
        run(False)


def _attn_a_out_prompt(qat, ka, vat, ga, bias_rows, ob_lo, ob_hi, w_a, w_b, x, final_gain, *,
                       batch, seq, final_norm):
    tiles = seq // ROW_TILE
    assert tiles % 2 == 0
    half = tiles // 2
    d_model = x.shape[1]
    cur = lambda b, n: (b * tiles + n, 0)
    prev = lambda b, n: (b * tiles + jnp.maximum(n - 1, 0), 0)
    cur_t = lambda b, n: (b * tiles + n, 0, 0)
    prev_t = lambda b, n: (b * tiles + jnp.maximum(n - 1, 0), 0, 0)
    const2 = lambda b, n: (0, 0)
    blk = (ROW_TILE, A_WIDTH)
    blk_t = (None, A_WIDTH, ROW_TILE)
    return pl.pallas_call(
        functools.partial(_attn_a_prompt_kernel, final_norm=final_norm, tiles=tiles),
        grid=(batch, tiles),
        in_specs=[
            pl.BlockSpec(blk_t, cur_t),
            pl.BlockSpec(blk, prev), pl.BlockSpec(blk, cur),
            pl.BlockSpec(blk_t, prev_t), pl.BlockSpec(blk_t, cur_t),
            pl.BlockSpec(blk, cur),
            pl.BlockSpec(bias_rows.shape, const2),
            pl.BlockSpec((ROW_TILE, B_WIDTH), lambda b, n: (b * half + jnp.minimum(n, half - 1), 0)),
            pl.BlockSpec((ROW_TILE, B_WIDTH), lambda b, n: (b * half + jnp.maximum(n - half, 0), 0)),
            pl.BlockSpec(w_a.shape, const2),
            pl.BlockSpec(w_b.shape, const2),
            pl.BlockSpec((ROW_TILE, d_model), cur),
            pl.BlockSpec((1, d_model), const2),
        ],
        out_specs=pl.BlockSpec((ROW_TILE, d_model), cur),
        out_shape=jax.ShapeDtypeStruct(x.shape, F32),
        scratch_shapes=[pltpu.VMEM((A_HEADS, A_WIN, A_QB), F32),
                        pltpu.VMEM((ROW_TILE, A_WIDTH), BF16),
                        pltpu.VMEM((A_WIN, A_QB), F32),
                        pltpu.VMEM((A_WIN, A_QB), F32),
                        pltpu.VMEM((A_WIN, A_QB), F32),
                        pltpu.VMEM((A_WIN, A_QB), F32)],
        compiler_params=pltpu.CompilerParams(
            dimension_semantics=("arbitrary", "arbitrary"), vmem_limit_bytes=VMEM_LIMIT),
        name="attn_a_out_prompt",
    )(qat, ka, ka, vat, vat, ga, bias_rows, ob_lo, ob_hi, w_a, w_b, x, final_gain)


def _diff_lambda(lq1_ref, lk1_ref, lq2_ref, lk2_ref, lambda_init):
    e1 = jnp.exp(jnp.sum(lq1_ref[...] * lk1_ref[...], axis=-1, keepdims=True))
    e2 = jnp.exp(jnp.sum(lq2_ref[...] * lk2_ref[...], axis=-1, keepdims=True))
    return e1 - e2 + lambda_init


def _stack_maps(qh):
    lane = lax.broadcasted_iota(jnp.int32, qh.shape, 1)
    zero = jnp.zeros_like(qh)
    return jnp.concatenate([jnp.where(lane < B_HEAD_DIM, qh, zero),
                            jnp.where(lane >= B_HEAD_DIM, qh, zero)], axis=0)


def _subnorm_gate(o, subg_ref, gate, lambda_init):
    o = o * lax.rsqrt(jnp.mean(o * o, axis=-1, keepdims=True) + EPS) * subg_ref[...]
    return (o * (1.0 - lambda_init)) * _silu(gate)


def _attn_b_prompt_kernel(qta_ref, qtb_ref, k_ref, vt_ref, ga_ref, gb_ref,
                          lq1_ref, lk1_ref, lq2_ref, lk2_ref, subg_ref, oa_ref, ob_ref,
                          rhs_ref, acc_ref, m_ref, sa_ref, sb_ref, *, lambda_init, tiles):
    g = pl.program_id(2)
    heads = rhs_ref.shape[1]
    for t, qt_ref in enumerate((qta_ref, qtb_ref)):
        for h in range(heads):
            qt = qt_ref[h * B_V_DIM:(h + 1) * B_V_DIM, :]
            row = lax.broadcasted_iota(jnp.int32, qt.shape, 0)
            zero = jnp.zeros_like(qt)
            rhs_ref[t, h, :, :B_TQ] = jnp.where(row < B_HEAD_DIM, qt, zero)
            rhs_ref[t, h, :, B_TQ:] = jnp.where(row >= B_HEAD_DIM, qt, zero)
    m_ref[...] = jnp.full(m_ref.shape, NEG_INF, F32)
    acc_ref[...] = jnp.zeros(acc_ref.shape, F32)

    items = [(h, c) for h in range(heads) for c in range(2 * B_TQ // B_STRIP)]

    def live_keys(c, diagonal):
        q0 = (c * B_STRIP) % B_TQ
        return min(B_TK, q0 + B_STRIP) if diagonal else B_TK

    def score_item(tile, j, n, dst, diagonal):
        h, c = items[n]
        part = B_TK // B_QK_SPLIT
        for r in range(0, live_keys(c, diagonal), part):
            k0 = pl.multiple_of(j * B_TK + r, part)
            dst[n, r:r + part, :] = _dot(k_ref[pl.ds(k0, part), h * B_V_DIM:(h + 1) * B_V_DIM],
                                         rhs_ref[tile, h, :, c * B_STRIP:(c + 1) * B_STRIP])

    def update_item(tile, j, n, src, diagonal):
        h, c = items[n]
        hs = slice(h * B_V_DIM, (h + 1) * B_V_DIM)
        cols = slice(c * B_STRIP, (c + 1) * B_STRIP)
        live = live_keys(c, diagonal)

        def chunk(r):
            s = src[n, r:r + B_ROWS, :]
            if diagonal:
                q0 = (c * B_STRIP) % B_TQ
                kc = (r + lax.broadcasted_iota(jnp.int32, s.shape, 0)) // CHUNK
                qc = (q0 + lax.broadcasted_iota(jnp.int32, s.shape, 1)) // CHUNK
                s = jnp.where(kc <= qc, s, NEG_INF)
            return s

        rows = range(0, live, B_ROWS)
        m_blk = functools.reduce(jnp.maximum, [chunk(r) for r in rows])
        m_old = m_ref[tile, h, :, cols]
        m_new = jnp.maximum(m_old, jnp.max(m_blk, axis=0, keepdims=True))
        alpha = jnp.exp2(m_old - m_new)
        m_ref[tile, h, :, cols] = m_new
        p = jnp.concatenate([jnp.exp2((chunk(r) - m_new).astype(BF16)) for r in rows], axis=0)
        vt1 = jnp.concatenate([vt_ref[j, hs, :live], jnp.ones((B_SUM_ROWS, live), BF16)], axis=0)
        acc_ref[tile, h, :, cols] = alpha * acc_ref[tile, h, :, cols] + _dot(vt1, p)

    def stage(score, dst, update, src, score_diagonal=False, update_diagonal=False):
        for n in range(len(items)):
            if score is not None:
                score_item(*score, n, dst, score_diagonal)
            if update is not None:
                update_item(*update, n, src, update_diagonal)

    buf_a, buf_b = sa_ref, sb_ref
    first = (0, g)
    last = (1, tiles - 1 - g)

    def at(s):
        in_lo = s <= g
        return jnp.where(in_lo, 0, 1), jnp.where(in_lo, s - 1, s - 1 - g)

    def pair(p, carry):
        s = 2 * p + 1
        stage(at(s + 1), buf_a, at(s), buf_b)
        stage(at(s + 2), buf_b, at(s + 1), buf_a)
        return carry

    stage(first, buf_a, None, None, score_diagonal=True)
    stage(at(1), buf_b, first, buf_a, update_diagonal=True)
    lax.fori_loop(0, (tiles - 2) // 2, pair, 0)

    lam = _diff_lambda(lq1_ref, lk1_ref, lq2_ref, lk2_ref, lambda_init)

    def finish(t, g_ref, o_ref):
        for h in range(heads):
            hs = slice(h * B_V_DIM, (h + 1) * B_V_DIM)
            o = acc_ref[t, h, :B_V_DIM, :] / acc_ref[t, h, B_V_DIM:B_V_DIM + 1, :]
            o = (o[:, :B_TQ] - lam * o[:, B_TQ:]).T
            o_ref[:, hs] = _subnorm_gate(o, subg_ref, g_ref[:, hs].astype(F32),
                                         lambda_init).astype(BF16)

    finish(0, ga_ref, oa_ref)
    stage(last, buf_a, (1, tiles - 2 - g), buf_b, score_diagonal=True)
    stage(None, None, last, buf_a, update_diagonal=True)
    finish(1, gb_ref, ob_ref)


def _attn_b_prompt(qbt, kb, vbt, gb, lams, subg, *, batch, seq, lambda_init):
    assert B_TQ == B_TK == ROW_TILE and B_HEADS % B_HEADS_PER_STEP == 0
    tiles = seq // B_TQ
    assert tiles % 2 == 0
    half = tiles // 2
    width = B_HEADS_PER_STEP * B_V_DIM
    n_items = B_HEADS_PER_STEP * 2 * B_TQ // B_STRIP
    lo = lambda b, g: b * tiles + g
    hi = lambda b, g: b * tiles + tiles - 1 - g
    vec = pl.BlockSpec((1, B_HEAD_DIM), lambda b, h, g: (0, 0))
    vbt = vbt.reshape(batch, tiles, B_WIDTH, B_TK)
    out = jax.ShapeDtypeStruct((batch * half * B_TQ, B_WIDTH), BF16)
    return pl.pallas_call(
        functools.partial(_attn_b_prompt_kernel, lambda_init=lambda_init, tiles=tiles),
        grid=(batch, B_HEADS // B_HEADS_PER_STEP, half),
        in_specs=[
            pl.BlockSpec((None, width, B_TQ), lambda b, h, g: (lo(b, g), h, 0)),
            pl.BlockSpec((None, width, B_TQ), lambda b, h, g: (hi(b, g), h, 0)),
            pl.BlockSpec((seq, width), lambda b, h, g: (b, h), pipeline_mode=pl.Buffered(1)),
            pl.BlockSpec((None, tiles, width, B_TK), lambda b, h, g: (b, 0, h, 0),
                         pipeline_mode=pl.Buffered(1)),
            pl.BlockSpec((B_TQ, width), lambda b, h, g: (lo(b, g), h)),
            pl.BlockSpec((B_TQ, width), lambda b, h, g: (hi(b, g), h)),
            vec, vec, vec, vec,
            pl.BlockSpec((1, B_V_DIM), lambda b, h, g: (0, 0)),
        ],
        out_specs=[pl.BlockSpec((B_TQ, width), lambda b, h, g: (b * half + g, h)),
                   pl.BlockSpec((B_TQ, width), lambda b, h, g: (b * half + half - 1 - g, h))],
        out_shape=[out, out],
        scratch_shapes=[pltpu.VMEM((2, B_HEADS_PER_STEP, B_V_DIM, 2 * B_TQ), BF16),
                        pltpu.VMEM((2, B_HEADS_PER_STEP, B_V_DIM + B_SUM_ROWS, 2 * B_TQ), F32),
                        pltpu.VMEM((2, B_HEADS_PER_STEP, 1, 2 * B_TQ), F32),
                        pltpu.VMEM((n_items, B_TK, B_STRIP), F32),
                        pltpu.VMEM((n_items, B_TK, B_STRIP), F32)],
        compiler_params=pltpu.CompilerParams(
            dimension_semantics=("arbitrary", "arbitrary", "arbitrary"),
            vmem_limit_bytes=VMEM_LIMIT),
        name="attn_b_prompt",
    )(qbt, qbt, kb, vbt, gb, gb, *lams, subg)


def _roll_in(cache_t, new_rows):
    t = new_rows.shape[0]
    keep = cache_t.shape[1]
    shifted = pltpu.roll(cache_t, keep - t, 1)
    pad = jnp.concatenate([jnp.zeros((LANES - t, new_rows.shape[1]), F32), new_rows], axis=0)
    new_t = pad.T
    lane = lax.broadcasted_iota(jnp.int32, new_t.shape, 1)
    last = jnp.where(lane >= LANES - t, new_t, shifted[:, keep - LANES:])
    return jnp.concatenate([shifted[:, :keep - LANES], last], axis=1)


def _attn_a_sample_kernel(q_ref, kn_ref, vn_ref, knf_ref, vnf_ref, kc_ref, vc_ref, g_ref,
                          brow_ref, o_ref, ko_ref, vo_ref):
    t = q_ref.shape[0]
    keep = kc_ref.shape[1]
    lane = lax.broadcasted_iota(jnp.int32, (t, LANES), 1)
    low_half = lane < A_HEAD_DIM
    pair_cols = [slice(pair * LANES, (pair + 1) * LANES) for pair in range(A_HEADS // 2)]
    scores = []
    for hd in range(A_HEADS):
        cs = pair_cols[hd // 2]
        qp = q_ref[:, cs]
        bias = _toeplitz(brow_ref[hd:hd + 1, :], t)
        qm = jnp.where(low_half if hd % 2 == 0 else ~low_half, qp, jnp.zeros_like(qp))
        scores.append((_dot(qm, kc_ref[cs, :].astype(BF16)) + bias[:, :keep],
                       _nt_dot(qm, kn_ref[:, cs]) + bias[:, keep:keep + t]))
    probs = []
    for s_c, s_n in scores:
        m = jnp.maximum(jnp.max(s_c, axis=-1, keepdims=True), jnp.max(s_n, axis=-1, keepdims=True))
        p_c = jnp.exp(s_c - m)
        p_n = jnp.exp(s_n - m)
        l = jnp.sum(p_c, axis=-1, keepdims=True) + jnp.sum(p_n, axis=-1, keepdims=True)
        probs.append((p_c.astype(BF16), p_n.astype(BF16), l))
    outs = []
    for hd, (p_c, p_n, l) in enumerate(probs):
        cs = pair_cols[hd // 2]
        o = _nt_dot(p_c, vc_ref[cs, :].astype(BF16)) + _dot(p_n, vn_ref[:, cs])
        outs.append(o / l)
    for pair, cs in enumerate(pair_cols):
        o_pair = jnp.where(low_half, outs[2 * pair], outs[2 * pair + 1])
        o_ref[:, cs] = (o_pair * _silu(g_ref[:, cs].astype(F32))).astype(BF16)
    ko_ref[...] = _roll_in(kc_ref[...], knf_ref[...])
    vo_ref[...] = _roll_in(vc_ref[...], vnf_ref[...])


def _attn_a_sample(qa, ka, va, kaf, vaf, cache_kt, cache_vt, ga, bias_rows, *, batch, t):
    keep = cache_kt.shape[2]
    new = pl.BlockSpec((t, A_WIDTH), lambda b: (b, 0))
    cache = pl.BlockSpec((None, A_WIDTH, keep), lambda b: (b, 0, 0))
    return pl.pallas_call(
        _attn_a_sample_kernel,
        grid=(batch,),
        in_specs=[new, new, new, new, new, cache, cache, new,
                  pl.BlockSpec(bias_rows.shape, lambda b: (0, 0))],
        out_specs=[new, cache, cache],
        out_shape=[jax.ShapeDtypeStruct(qa.shape, BF16),
                   jax.ShapeDtypeStruct(cache_kt.shape, F32),
                   jax.ShapeDtypeStruct(cache_vt.shape, F32)],
        compiler_params=pltpu.CompilerParams(
            dimension_semantics=("arbitrary",), vmem_limit_bytes=VMEM_LIMIT),
        name="attn_a_sample",
    )(qa, ka, va, kaf, vaf, cache_kt, cache_vt, ga, bias_rows)


def _attn_b_sample_kernel(q_ref, kn_ref, vn_ref, kc_ref, vc_ref, g_ref,
                          lq1_ref, lk1_ref, lq2_ref, lk2_ref, subg_ref, o_ref, *, lambda_init):
    t = q_ref.shape[0]
    past = kc_ref.shape[1]
    lam = _diff_lambda(lq1_ref, lk1_ref, lq2_ref, lk2_ref, lambda_init)
    head_cols = [slice(h * B_V_DIM, (h + 1) * B_V_DIM) for h in range(B_HEADS)]
    scores = []
    for cs in head_cols:
        qs = _stack_maps(q_ref[:, cs])
        scores.append((_dot(qs, kc_ref[cs, :].astype(BF16)), _nt_dot(qs, kn_ref[:, cs])))
    attn = []
    for s_c, s_n in scores:
        m = jnp.maximum(jnp.max(s_c, axis=-1, keepdims=True),
                        jnp.max(s_n, axis=-1, keepdims=True))
        p_c = jnp.exp(s_c - m)
        p_n = jnp.exp(s_n - m)
        l = jnp.sum(p_c, axis=-1, keepdims=True) + jnp.sum(p_n, axis=-1, keepdims=True)
        p_c = p_c / l
        p_n = p_n / l
        attn.append(((p_c[:t] - lam * p_c[t:]).astype(BF16),
                     (p_n[:t] - lam * p_n[t:]).astype(BF16)))
    for h, (cs, (a_c, a_n)) in enumerate(zip(head_cols, attn)):
        vc = vc_ref[pl.ds(h, past, stride=B_HEADS), :].astype(BF16)
        o = _dot(a_c, vc) + _dot(a_n, vn_ref[:, cs])
        o_ref[:, cs] = _subnorm_gate(o, subg_ref, g_ref[:, cs].astype(F32),
                                     lambda_init).astype(BF16)


def _attn_b_sample(qb, kb, vb, cache_kt, cache_v, gb, lams, subg, *, batch, t, lambda_init):
    past = cache_kt.shape[2]
    new = pl.BlockSpec((t, B_WIDTH), lambda b: (b, 0))
    vec = pl.BlockSpec((1, B_HEAD_DIM), lambda b: (0, 0))
    return pl.pallas_call(
        functools.partial(_attn_b_sample_kernel, lambda_init=lambda_init),
        grid=(batch,),
        in_specs=[new, new, new,
                  pl.BlockSpec((None, B_WIDTH, past), lambda b: (b, 0, 0)),
                  pl.BlockSpec((None, past * B_HEADS, B_V_DIM), lambda b: (b, 0, 0)),
                  new, vec, vec, vec, vec,
                  pl.BlockSpec((1, B_V_DIM), lambda b: (0, 0))],
        out_specs=new,
        out_shape=jax.ShapeDtypeStruct(qb.shape, BF16),
        compiler_params=pltpu.CompilerParams(
            dimension_semantics=("arbitrary",), vmem_limit_bytes=VMEM_LIMIT),
        name="attn_b_sample",
    )(qb, kb, vb, cache_kt, cache_v, gb, *lams, subg)


def _out_kernel(oa_ref, ob_ref, wa_ref, wb_ref, x_ref, fg_ref, y_ref, *, final_norm):
    y = x_ref[...] + _dot(oa_ref[...], wa_ref[...]) + _dot(ob_ref[...], wb_ref[...])
    if final_norm:
        y = y * lax.rsqrt(jnp.mean(y * y, axis=-1, keepdims=True) + EPS) * fg_ref[...]
    y_ref[...] = y


def _out_proj(oa, ob, w_a, w_b, x, final_gain, *, tm, final_norm):
    rows, d_model = x.shape
    row_blk = lambda i: (i, 0)
    const2 = lambda i: (0, 0)
    return pl.pallas_call(
        functools.partial(_out_kernel, final_norm=final_norm),
        grid=(rows // tm,),
        in_specs=[
            pl.BlockSpec((tm, A_WIDTH), row_blk),
            pl.BlockSpec((tm, B_WIDTH), row_blk),
            pl.BlockSpec(w_a.shape, const2),
            pl.BlockSpec(w_b.shape, const2),
            pl.BlockSpec((tm, d_model), row_blk),
            pl.BlockSpec((1, d_model), const2),
        ],
        out_specs=pl.BlockSpec((tm, d_model), row_blk),
        out_shape=jax.ShapeDtypeStruct(x.shape, F32),
        compiler_params=pltpu.CompilerParams(
            dimension_semantics=("arbitrary",), vmem_limit_bytes=VMEM_LIMIT),
        name="out_proj",
    )(oa, ob, w_a, w_b, x, final_gain)


def _channel_major(a):
    n, pos = a.shape[:2]
    return jnp.moveaxis(a.reshape(n, pos, -1), 1, 2)


def _position_major(a, channel_dims):
    n, _, pos = a.shape
    return jnp.moveaxis(a, 2, 1).reshape(n, pos, *channel_dims)


def kernel(x_prompt, x_sample, cache_a_k, cache_a_v, cache_b_k, cache_b_v,
           norm_gain, w_in, w_out, rel_bias, lambda_q1, lambda_k1, lambda_q2, lambda_k2,
           subln_gain, final_gain):
    batch, seq, d_model = x_prompt.shape
    dec_batch, t_sample, _ = x_sample.shape
    depth = w_in.shape[0]
    past_len = cache_b_k.shape[2]
    a_keep = cache_a_k.shape[2]
    keep_prompt = min(BAND_ROWS, seq)
    assert seq % ROW_TILE == 0 and keep_prompt == ROW_TILE and seq % B_TQ == 0
    assert t_sample <= a_keep and t_sample <= LANES and d_model == 2 * SEG

    rows_p = batch * seq
    rows_s = dec_batch * t_sample
    tiles_per_seq = seq // ROW_TILE
    tables_p = _rope_tables(np.arange(ROW_TILE), np.arange(tiles_per_seq) * ROW_TILE)
    tables_s = _rope_tables(past_len + np.arange(rows_s) % t_sample, np.zeros(1))
    a_dims = (A_HEADS, A_HEAD_DIM)
    bk_dims = (B_HEADS, 2, B_HEAD_DIM)
    bv_dims = (B_HEADS, B_V_DIM)

    yp = x_prompt.reshape(rows_p, d_model)
    ys = x_sample.reshape(rows_s, d_model)
    fg = final_gain.reshape(1, d_model)
    outs = [[] for _ in range(8)]
    for l in range(depth):
        lambda_init = 0.8 - 0.6 * math.exp(-0.3 * l)
        last = l == depth - 1
        w_l = w_in[l].astype(BF16)
        w_a = w_out[l, :A_WIDTH].astype(BF16)
        w_b = w_out[l, A_WIDTH:].astype(BF16)
        gain = norm_gain[l].reshape(1, d_model)
        lams = [a[l].reshape(1, B_HEAD_DIM).astype(F32)
                for a in (lambda_q1, lambda_k1, lambda_q2, lambda_k2)]
        subg = subln_gain[l].reshape(1, B_V_DIM).astype(F32)

        qa, ka, va, ga, qb, kb, vb, gb, kaf, vaf, kbf, vbf = _proj(
            yp, gain, w_l, tables_p, tm=ROW_TILE, tiles_per_seq=tiles_per_seq,
            tail_every=tiles_per_seq, channel_major=True, q_scale=LOG2E * B_HEAD_DIM ** -0.5)
        bias_p = _bias_rows(rel_bias[l], 1 - A_WIN, A_QB - 1, lambda t: t + BAND_ROWS, LOG2E)
        ob_lo, ob_hi = _attn_b_prompt(qb, kb, vb, gb, lams, subg, batch=batch, seq=seq,
                                      lambda_init=lambda_init)
        yp = _attn_a_out_prompt(qa, ka, va, ga, bias_p, ob_lo, ob_hi, w_a, w_b, yp, fg,
                                batch=batch, seq=seq, final_norm=last)
        outs[0].append(_position_major(kaf, a_dims))
        outs[1].append(_position_major(vaf, a_dims))
        outs[2].append(_position_major(kbf, bk_dims))
        outs[3].append(vbf.reshape(batch, seq, *bv_dims))

        qa, ka, va, ga, qb, kb, vb, gb, kaf, vaf, kbf, vbf = _proj(
            ys, gain, w_l, tables_s, tm=rows_s, tiles_per_seq=1, tail_every=1,
            channel_major=False, q_scale=B_HEAD_DIM ** -0.5)
        oa, ak_new, av_new = _attn_a_sample(
            qa, ka, va, kaf, vaf, _channel_major(cache_a_k[l]), _channel_major(cache_a_v[l]),
            ga, _bias_rows(rel_bias[l], 1 - t_sample, a_keep + t_sample - 1, lambda t: a_keep - t),
            batch=dec_batch, t=t_sample)
        ob = _attn_b_sample(
            qb, kb, vb, _channel_major(cache_b_k[l]),
            cache_b_v[l].reshape(dec_batch, past_len * B_HEADS, B_V_DIM),
            gb, lams, subg, batch=dec_batch, t=t_sample, lambda_init=lambda_init)
        ys = _out_proj(oa, ob, w_a, w_b, ys, fg, tm=rows_s, final_norm=last)
        outs[4].append(_position_major(ak_new, a_dims))
        outs[5].append(_position_major(av_new, a_dims))
        outs[6].append(kbf.reshape(dec_batch, t_sample, *bk_dims))
        outs[7].append(vbf.reshape(dec_batch, t_sample, *bv_dims))

    return (yp.reshape(batch, seq, d_model), ys.reshape(dec_batch, t_sample, d_model),
            *[jnp.stack(o) for o in outs])
```

```python
import functools
import math

import numpy as np
import jax
import jax.numpy as jnp
from jax import lax
from jax.experimental import pallas as pl
from jax.experimental.pallas import tpu as pltpu

F32 = jnp.float32
BF16 = jnp.bfloat16
NEG_INF = float("-inf")

CHUNK = 64
N_PREV_CHUNKS = 8
A_HEADS = 8
A_HEAD_DIM = 64
A_WIDTH = A_HEADS * A_HEAD_DIM
B_HEADS = 4
B_HEAD_DIM = 64
B_V_DIM = 2 * B_HEAD_DIM
B_WIDTH = B_HEADS * B_V_DIM
SEG = 512
MAX_REL = 128
ROPE_THETA = 500000.0
ROPE_DIM = 16
EPS = 1e-6
LOG2E = math.log2(math.e)
LANES = 128
MXU_TILE = 256
BF16_SUBLANES = 16
VMEM_LIMIT = 56 * 1024 * 1024
B_VMEM_LIMIT = 58 * 1024 * 1024
ROW_TILE = 512
BAND_ROWS = N_PREV_CHUNKS * CHUNK
A_QB = MXU_TILE
A_WIN = BAND_ROWS + A_QB
A_ROWS = 64
A_QK_SPLIT = 3
A_SUM_ROWS = BF16_SUBLANES
B_TQ = 512
B_TK = 512
B_HEADS_PER_STEP = 4
B_STRIP = MXU_TILE
B_ROWS = 64
B_QK_SPLIT = 2
B_SUM_ROWS = BF16_SUBLANES

NT_DIMS = (((1,), (1,)), ((), ()))


def _nt_dot(a, b):
    return lax.dot_general(a, b, NT_DIMS, preferred_element_type=F32)


def _dot(a, b):
    return jnp.dot(a, b, preferred_element_type=F32)


def _silu(g):
    return g / (1.0 + jnp.exp(-g))


def _round_up(n, m):
    return -(-n // m) * m


def _rope_lane_freq():
    d = np.arange(LANES) % B_HEAD_DIM
    inv = ROPE_THETA ** (-np.arange(0, ROPE_DIM, 2, dtype=np.float64) / ROPE_DIM)
    return np.where(d < ROPE_DIM, inv[d % (ROPE_DIM // 2)], 0.0)


def _rope_tables(row_pos, base_pos):
    f = _rope_lane_freq()[None, :]
    ar = np.asarray(row_pos, np.float64)[:, None] * f
    ab = np.asarray(base_pos, np.float64)[:, None] * f
    as32 = lambda a: jnp.asarray(a.astype(np.float32))
    return (as32(np.cos(ar)), as32(np.sin(ar)),
            as32(np.cos(ab))[:, None, :], as32(np.sin(ab))[:, None, :])


def _bias_rows(rel_bias, t_min, t_max, rel_of_t, scale=1.0):
    width = _round_up(t_max - t_min + 1, LANES)
    t = np.arange(width)
    t = np.where(t <= t_max, t, t - width)
    idx = np.clip(rel_of_t(t), -MAX_REL, MAX_REL) + MAX_REL
    return rel_bias.astype(F32)[:, idx] * scale


def _toeplitz(row, n_rows):
    return pltpu.roll(jnp.broadcast_to(row, (n_rows, row.shape[-1])), 0, 1, stride=1, stride_axis=0)


def _proj_kernel(x_ref, g_ref, w_ref, cr_ref, sr_ref, cb_ref, sb_ref, *out_refs,
                 channel_major, q_scale):
    (qa_ref, ka_ref, va_ref, ga_ref, qb_ref, kb_ref, vb_ref, gb_ref,
     kaf_ref, vaf_ref, kbf_ref, vbf_ref) = out_refs
    x = x_ref[...]
    inv = lax.rsqrt(jnp.mean(x * x, axis=-1, keepdims=True) + EPS)
    h = (x * inv * g_ref[...]).astype(BF16)

    def seg(k):
        return _dot(h, w_ref[:, k * SEG:(k + 1) * SEG])

    cb, sb = cb_ref[0], sb_ref[0]
    cr, sr = cr_ref[...], sr_ref[...]
    cos = cb * cr - sb * sr
    sin = sb * cr + cb * sr
    d = lax.broadcasted_iota(jnp.int32, cos.shape, 1) % B_HEAD_DIM
    sin_lo = jnp.where(d < ROPE_DIM // 2, -sin, 0.0)
    sin_hi = jnp.where(d >= ROPE_DIM // 2, sin, 0.0)

    def rope(z):
        cols = []
        for c in range(SEG // LANES):
            zc = z[:, c * LANES:(c + 1) * LANES]
            up = pltpu.roll(zc, LANES - ROPE_DIM // 2, 1)
            dn = pltpu.roll(zc, ROPE_DIM // 2, 1)
            cols.append(zc * cos + up * sin_lo + dn * sin_hi)
        return jnp.concatenate(cols, axis=1)

    qa = seg(0) * q_scale
    qa_ref[...] = (qa.T if channel_major else qa).astype(BF16)
    ka = seg(1)
    ka_ref[...] = ka.astype(BF16)
    va = seg(2)
    va_t = va.T if channel_major else va
    va_ref[...] = va_t.astype(BF16)
    kaf_ref[...] = ka.T if channel_major else ka
    vaf_ref[...] = va_t

    ga_ref[...] = seg(3).astype(BF16)
    qb = rope(seg(4)) * q_scale
    qb_ref[...] = (qb.T if channel_major else qb).astype(BF16)
    kb = rope(seg(5))
    kbf_ref[...] = kb.T if channel_major else kb
    kb_ref[...] = kb.astype(BF16)
    vb = seg(6)
    n_col = SEG // LANES
    for c in range(n_col):
        vbf_ref[pl.ds(c, vb.shape[0], stride=n_col), :] = vb[:, c * LANES:(c + 1) * LANES]
    vb_ref[...] = (vb.T if channel_major else vb).astype(BF16)
    gb_ref[...] = seg(7).astype(BF16)


def _proj(x, gain, w_bf16, tables, *, tm, tiles_per_seq, tail_every, channel_major, q_scale):
    rows, d_model = x.shape
    n_tiles = rows // tm
    n_tail = n_tiles // tail_every
    n_seq = n_tiles // tiles_per_seq
    cr, sr, cb, sb = tables
    row_blk = lambda i: (i, 0)
    const2 = lambda i: (0, 0)
    base_blk = lambda i: (i % tiles_per_seq, 0, 0)
    bf = jax.ShapeDtypeStruct((rows, SEG), BF16)
    f32_full = jax.ShapeDtypeStruct((rows, SEG), F32)
    full_spec = pl.BlockSpec((tm, SEG), row_blk)
    if channel_major:
        assert tm == SEG
        tail = jax.ShapeDtypeStruct((n_tail, SEG, tm), F32)
        tail_spec = pl.BlockSpec((None, SEG, tm), lambda i: (i // tail_every, 0, 0))
        kbf = jax.ShapeDtypeStruct((n_seq, SEG, tiles_per_seq * tm), F32)
        kbf_spec = pl.BlockSpec((None, SEG, tm), lambda i: (i // tiles_per_seq, 0, i % tiles_per_seq))
        bf_t = jax.ShapeDtypeStruct((n_tiles, SEG, tm), BF16)
        bf_t_spec = pl.BlockSpec((None, SEG, tm), lambda i: (i, 0, 0))
    else:
        tail = jax.ShapeDtypeStruct((n_tail * tm, SEG), F32)
        tail_spec = pl.BlockSpec((tm, SEG), lambda i: (i // tail_every, 0))
        kbf, kbf_spec = f32_full, full_spec
        bf_t, bf_t_spec = bf, full_spec
    return pl.pallas_call(
        functools.partial(_proj_kernel, channel_major=channel_major,
                          q_scale=q_scale),
        grid=(n_tiles,),
        in_specs=[
            pl.BlockSpec((tm, d_model), row_blk),
            pl.BlockSpec((1, d_model), const2),
            pl.BlockSpec(w_bf16.shape, const2),
            pl.BlockSpec((tm, LANES), const2),
            pl.BlockSpec((tm, LANES), const2),
            pl.BlockSpec((1, 1, LANES), base_blk),
            pl.BlockSpec((1, 1, LANES), base_blk),
        ],
        out_specs=([bf_t_spec, full_spec, bf_t_spec, full_spec] * 2
                   + [tail_spec, tail_spec, kbf_spec,
                      pl.BlockSpec((tm * SEG // LANES, LANES), row_blk)]),
        out_shape=([bf_t, bf, bf_t, bf] * 2
                   + [tail, tail, kbf, jax.ShapeDtypeStruct((rows * SEG // LANES, LANES), F32)]),
        compiler_params=pltpu.CompilerParams(
            dimension_semantics=("arbitrary",), vmem_limit_bytes=VMEM_LIMIT),
        name="proj",
    )(x, gain, w_bf16, cr, sr, cb, sb)


def _attn_a_prompt_kernel(qt_ref, kp_ref, kc_ref, vtp_ref, vtc_ref, g_ref, brow_ref,
                          ob_lo_ref, ob_hi_ref, wa_ref, wb_ref, x_ref, fg_ref, y_ref,
                          bias_ref, oa_ref, sa_ref, sb_ref, sc_ref, sd_ref, *, final_norm, tiles):
    n = pl.program_id(1)

    @pl.when((pl.program_id(0) == 0) & (n == 0))
    def _():
        j = lax.broadcasted_iota(jnp.int32, (A_WIN, A_QB), 0)
        i = lax.broadcasted_iota(jnp.int32, (A_WIN, A_QB), 1)
        gap = i // CHUNK + N_PREV_CHUNKS - j // CHUNK
        band = (gap >= 0) & (gap <= N_PREV_CHUNKS)
        for hd in range(A_HEADS):
            t = _toeplitz(brow_ref[hd:hd + 1, :], A_WIN)[:, :A_QB]
            bias_ref[hd] = jnp.where(band, t, NEG_INF)

    items = [(q0, hd) for q0 in range(0, ROW_TILE, A_QB) for hd in range(A_HEADS)]
    part = A_WIN // A_QK_SPLIT

    def window(q0, first_tile):
        pieces = []
        for r in range(0, A_WIN, part):
            pos = q0 - BAND_ROWS + r
            if pos >= 0:
                pieces.append((1, pos, r))
            elif not first_tile:
                pieces.append((0, pos + ROW_TILE, r))
        return pieces

    def score_item(t, dst, first_tile):
        q0, hd = items[t]
        lanes = slice(hd // 2 * LANES, (hd // 2 + 1) * LANES)
        qt = qt_ref[lanes, q0:q0 + A_QB]
        row = lax.broadcasted_iota(jnp.int32, qt.shape, 0)
        rhs = jnp.where((row >= A_HEAD_DIM) == bool(hd % 2), qt, jnp.zeros_like(qt))
        for which, start, r in window(q0, first_tile):
            k_ref = (kp_ref, kc_ref)[which]
            dst[r:r + part, :] = (_dot(k_ref[start:start + part, lanes], rhs)
                                  + bias_ref[hd, r:r + part, :])

    def update_item(t, src, first_tile):
        q0, hd = items[t]
        pieces = window(q0, first_tile)
        chunks = [r + c for _, _, r in pieces for c in range(0, part, A_ROWS)]
        m = jnp.max(functools.reduce(jnp.maximum, [src[r:r + A_ROWS, :] for r in chunks]),
                    axis=0, keepdims=True)
        o = None
        for which, start, r in pieces:
            p = jnp.concatenate([jnp.exp2((src[r + c:r + c + A_ROWS, :] - m).astype(BF16))
                                 for c in range(0, part, A_ROWS)], axis=0)
            vt_ref = (vtp_ref, vtc_ref)[which]
            vt1 = jnp.concatenate(
                [vt_ref[hd * A_HEAD_DIM:(hd + 1) * A_HEAD_DIM, start:start + part],
                 jnp.ones((A_SUM_ROWS, part), BF16)], axis=0)
            contrib = _dot(vt1, p)
            o = contrib if o is None else o + contrib
        return o[:A_HEAD_DIM] / o[A_HEAD_DIM:A_HEAD_DIM + 1]

    def residual_rows(q0):
        rows = slice(q0, q0 + A_QB)
        ob = jnp.where(n < tiles // 2, ob_lo_ref[rows, :], ob_hi_ref[rows, :])
        y_ref[rows, :] = x_ref[rows, :] + _dot(ob, wb_ref[...])

    def run(first_tile):
        bufs = (sa_ref, sb_ref, sc_ref, sd_ref)
        depth = len(bufs) - 1
        for t in range(depth):
            score_item(t, bufs[t % len(bufs)], first_tile)
        for q0 in sorted({q0 for q0, _ in items})[:-1]:
            residual_rows(q0)
        o_even = None
        for t, (q0, hd) in enumerate(items):
            if t + depth < len(items):
                score_item(t + depth, bufs[(t + depth) % len(bufs)], first_tile)
            elif t + depth == len(items):
                residual_rows(items[-1][0])
            o_head = update_item(t, bufs[t % len(bufs)], first_tile)
            if hd % 2 == 0:
                o_even = o_head
            else:
                lanes = slice(hd // 2 * LANES, (hd // 2 + 1) * LANES)
                o_pair = jnp.concatenate([o_even, o_head], axis=0).T
                gate = g_ref[q0:q0 + A_QB, lanes].astype(F32)
                oa_ref[q0:q0 + A_QB, lanes] = (o_pair * _silu(gate)).astype(BF16)
            if hd == A_HEADS - 1:
                rows = slice(q0, q0 + A_QB)
                y = y_ref[rows, :] + _dot(oa_ref[rows, :], wa_ref[...])
                if final_norm:
                    y = y * lax.rsqrt(jnp.mean(y * y, axis=-1, keepdims=True) + EPS) * fg_ref[...]
                y_ref[rows, :] = y

    @pl.when(n == 0)
    def _():
        run(True)

    @pl.when(n > 0)
    def _():
        run(False)


def _attn_a_out_prompt(qat, ka, vat, ga, bias_rows, ob_lo, ob_hi, w_a, w_b, x, final_gain, *,
                       batch, seq, final_norm):
    tiles = seq // ROW_TILE
    assert tiles % 2 == 0
    half = tiles // 2
    d_model = x.shape[1]
    cur = lambda b, n: (b * tiles + n, 0)
    prev = lambda b, n: (b * tiles + jnp.maximum(n - 1, 0), 0)
    cur_t = lambda b, n: (b * tiles + n, 0, 0)
    prev_t = lambda b, n: (b * tiles + jnp.maximum(n - 1, 0), 0, 0)
    const2 = lambda b, n: (0, 0)
    blk = (ROW_TILE, A_WIDTH)
    blk_t = (None, A_WIDTH, ROW_TILE)
    return pl.pallas_call(
        functools.partial(_attn_a_prompt_kernel, final_norm=final_norm, tiles=tiles),
        grid=(batch, tiles),
        in_specs=[
            pl.BlockSpec(blk_t, cur_t),
            pl.BlockSpec(blk, prev), pl.BlockSpec(blk, cur),
            pl.BlockSpec(blk_t, prev_t), pl.BlockSpec(blk_t, cur_t),
            pl.BlockSpec(blk, cur),
            pl.BlockSpec(bias_rows.shape, const2),
            pl.BlockSpec((ROW_TILE, B_WIDTH), lambda b, n: (b * half + jnp.minimum(n, half - 1), 0)),
            pl.BlockSpec((ROW_TILE, B_WIDTH), lambda b, n: (b * half + jnp.maximum(n - half, 0), 0)),
            pl.BlockSpec(w_a.shape, const2),
            pl.BlockSpec(w_b.shape, const2),
            pl.BlockSpec((ROW_TILE, d_model), cur),
            pl.BlockSpec((1, d_model), const2),
        ],
        out_specs=pl.BlockSpec((ROW_TILE, d_model), cur),
        out_shape=jax.ShapeDtypeStruct(x.shape, F32),
        scratch_shapes=[pltpu.VMEM((A_HEADS, A_WIN, A_QB), F32),
                        pltpu.VMEM((ROW_TILE, A_WIDTH), BF16),
                        pltpu.VMEM((A_WIN, A_QB), F32),
                        pltpu.VMEM((A_WIN, A_QB), F32),
                        pltpu.VMEM((A_WIN, A_QB), F32),
                        pltpu.VMEM((A_WIN, A_QB), F32)],
        compiler_params=pltpu.CompilerParams(
            dimension_semantics=("arbitrary", "arbitrary"), vmem_limit_bytes=VMEM_LIMIT),
        name="attn_a_out_prompt",
    )(qat, ka, ka, vat, vat, ga, bias_rows, ob_lo, ob_hi, w_a, w_b, x, final_gain)


def _diff_lambda(lq1_ref, lk1_ref, lq2_ref, lk2_ref, lambda_init):
    e1 = jnp.exp(jnp.sum(lq1_ref[...] * lk1_ref[...], axis=-1, keepdims=True))
    e2 = jnp.exp(jnp.sum(lq2_ref[...] * lk2_ref[...], axis=-1, keepdims=True))
    return e1 - e2 + lambda_init


def _stack_maps(qh):
    lane = lax.broadcasted_iota(jnp.int32, qh.shape, 1)
    zero = jnp.zeros_like(qh)
    return jnp.concatenate([jnp.where(lane < B_HEAD_DIM, qh, zero),
                            jnp.where(lane >= B_HEAD_DIM, qh, zero)], axis=0)


def _subnorm_gate(o, subg_ref, gate, lambda_init):
    o = o * lax.rsqrt(jnp.mean(o * o, axis=-1, keepdims=True) + EPS) * subg_ref[...]
    return (o * (1.0 - lambda_init)) * _silu(gate)


def _attn_b_prompt_kernel(qta_ref, qtb_ref, k_ref, vt_ref, ga_ref, gb_ref,
                          lq1_ref, lk1_ref, lq2_ref, lk2_ref, subg_ref, oa_ref, ob_ref,
                          rhs_ref, acc_ref, m_ref, sa_ref, sb_ref, *, lambda_init, tiles):
    g = pl.program_id(2)
    heads = rhs_ref.shape[1]
    for t, qt_ref in enumerate((qta_ref, qtb_ref)):
        for h in range(heads):
            qt = qt_ref[h * B_V_DIM:(h + 1) * B_V_DIM, :]
            row = lax.broadcasted_iota(jnp.int32, qt.shape, 0)
            zero = jnp.zeros_like(qt)
            rhs_ref[t, h, :, :B_TQ] = jnp.where(row < B_HEAD_DIM, qt, zero)
            rhs_ref[t, h, :, B_TQ:] = jnp.where(row >= B_HEAD_DIM, qt, zero)
    m_ref[...] = jnp.full(m_ref.shape, NEG_INF, F32)
    acc_ref[...] = jnp.zeros(acc_ref.shape, F32)

    items = [(h, c) for h in range(heads) for c in range(2 * B_TQ // B_STRIP)]

    def live_keys(c, diagonal):
        q0 = (c * B_STRIP) % B_TQ
        return min(B_TK, q0 + B_STRIP) if diagonal else B_TK

    def score_item(tile, j, n, dst, diagonal):
        h, c = items[n]
        part = B_TK // B_QK_SPLIT
        for r in range(0, live_keys(c, diagonal), part):
            k0 = pl.multiple_of(j * B_TK + r, part)
            dst[n, r:r + part, :] = _dot(k_ref[pl.ds(k0, part), h * B_V_DIM:(h + 1) * B_V_DIM],
                                         rhs_ref[tile, h, :, c * B_STRIP:(c + 1) * B_STRIP])

    def update_item(tile, j, n, src, diagonal):
        h, c = items[n]
        hs = slice(h * B_V_DIM, (h + 1) * B_V_DIM)
        cols = slice(c * B_STRIP, (c + 1) * B_STRIP)
        live = live_keys(c, diagonal)

        def chunk(r):
            s = src[n, r:r + B_ROWS, :]
            if diagonal:
                q0 = (c * B_STRIP) % B_TQ
                kc = (r + lax.broadcasted_iota(jnp.int32, s.shape, 0)) // CHUNK
                qc = (q0 + lax.broadcasted_iota(jnp.int32, s.shape, 1)) // CHUNK
                s = jnp.where(kc <= qc, s, NEG_INF)
            return s

        rows = range(0, live, B_ROWS)
        m_blk = functools.reduce(jnp.maximum, [chunk(r) for r in rows])
        m_old = m_ref[tile, h, :, cols]
        m_new = jnp.maximum(m_old, jnp.max(m_blk, axis=0, keepdims=True))
        alpha = jnp.exp2(m_old - m_new)
        m_ref[tile, h, :, cols] = m_new
        p = jnp.concatenate([jnp.exp2((chunk(r) - m_new).astype(BF16)) for r in rows], axis=0)
        vt1 = jnp.concatenate([vt_ref[j, hs, :live], jnp.ones((B_SUM_ROWS, live), BF16)], axis=0)
        acc_ref[tile, h, :, cols] = alpha * acc_ref[tile, h, :, cols] + _dot(vt1, p)

    def stage(score, dst, update, src, score_diagonal=False, update_diagonal=False):
        for n in range(len(items)):
            if score is not None:
                score_item(*score, n, dst, score_diagonal)
            if update is not None:
                update_item(*update, n, src, update_diagonal)

    buf_a, buf_b = sa_ref, sb_ref
    first = (0, g)
    last = (1, tiles - 1 - g)

    def at(s):
        in_lo = s <= g
        return jnp.where(in_lo, 0, 1), jnp.where(in_lo, s - 1, s - 1 - g)

    def pair(p, carry):
        s = 2 * p + 1
        stage(at(s + 1), buf_a, at(s), buf_b)
        stage(at(s + 2), buf_b, at(s + 1), buf_a)
        return carry

    stage(first, buf_a, None, None, score_diagonal=True)
    stage(at(1), buf_b, first, buf_a, update_diagonal=True)
    lax.fori_loop(0, (tiles - 2) // 2, pair, 0)

    lam = _diff_lambda(lq1_ref, lk1_ref, lq2_ref, lk2_ref, lambda_init)

    def finish(t, g_ref, o_ref):
        for h in range(heads):
            hs = slice(h * B_V_DIM, (h + 1) * B_V_DIM)
            o = acc_ref[t, h, :B_V_DIM, :] / acc_ref[t, h, B_V_DIM:B_V_DIM + 1, :]
            o = (o[:, :B_TQ] - lam * o[:, B_TQ:]).T
            o_ref[:, hs] = _subnorm_gate(o, subg_ref, g_ref[:, hs].astype(F32),
                                         lambda_init).astype(BF16)

    finish(0, ga_ref, oa_ref)
    stage(last, buf_a, (1, tiles - 2 - g), buf_b, score_diagonal=True)
    stage(None, None, last, buf_a, update_diagonal=True)
    finish(1, gb_ref, ob_ref)


def _attn_b_prompt(qbt, kb, vbt, gb, lams, subg, *, batch, seq, lambda_init):
    assert B_TQ == B_TK == ROW_TILE and B_HEADS % B_HEADS_PER_STEP == 0
    tiles = seq // B_TQ
    assert tiles % 2 == 0
    half = tiles // 2
    width = B_HEADS_PER_STEP * B_V_DIM
    n_items = B_HEADS_PER_STEP * 2 * B_TQ // B_STRIP
    lo = lambda b, g: b * tiles + g
    hi = lambda b, g: b * tiles + tiles - 1 - g
    vec = pl.BlockSpec((1, B_HEAD_DIM), lambda b, h, g: (0, 0))
    vbt = vbt.reshape(batch, tiles, B_WIDTH, B_TK)
    out = jax.ShapeDtypeStruct((batch * half * B_TQ, B_WIDTH), BF16)
    return pl.pallas_call(
        functools.partial(_attn_b_prompt_kernel, lambda_init=lambda_init, tiles=tiles),
        grid=(batch, B_HEADS // B_HEADS_PER_STEP, half),
        in_specs=[
            pl.BlockSpec((None, width, B_TQ), lambda b, h, g: (lo(b, g), h, 0)),
            pl.BlockSpec((None, width, B_TQ), lambda b, h, g: (hi(b, g), h, 0)),
            pl.BlockSpec((seq, width), lambda b, h, g: (b, h)),
            pl.BlockSpec((None, tiles, width, B_TK), lambda b, h, g: (b, 0, h, 0),
                         pipeline_mode=pl.Buffered(1)),
            pl.BlockSpec((B_TQ, width), lambda b, h, g: (lo(b, g), h)),
            pl.BlockSpec((B_TQ, width), lambda b, h, g: (hi(b, g), h)),
            vec, vec, vec, vec,
            pl.BlockSpec((1, B_V_DIM), lambda b, h, g: (0, 0)),
        ],
        out_specs=[pl.BlockSpec((B_TQ, width), lambda b, h, g: (b * half + g, h)),
                   pl.BlockSpec((B_TQ, width), lambda b, h, g: (b * half + half - 1 - g, h))],
        out_shape=[out, out],
        scratch_shapes=[pltpu.VMEM((2, B_HEADS_PER_STEP, B_V_DIM, 2 * B_TQ), BF16),
                        pltpu.VMEM((2, B_HEADS_PER_STEP, B_V_DIM + B_SUM_ROWS, 2 * B_TQ), F32),
                        pltpu.VMEM((2, B_HEADS_PER_STEP, 1, 2 * B_TQ), F32),
                        pltpu.VMEM((n_items, B_TK, B_STRIP), F32),
                        pltpu.VMEM((n_items, B_TK, B_STRIP), F32)],
        compiler_params=pltpu.CompilerParams(
            dimension_semantics=("arbitrary", "arbitrary", "arbitrary"),
            vmem_limit_bytes=B_VMEM_LIMIT),
        name="attn_b_prompt",
    )(qbt, qbt, kb, vbt, gb, gb, *lams, subg)


def _roll_in(cache_t, new_rows):
    t = new_rows.shape[0]
    keep = cache_t.shape[1]
    shifted = pltpu.roll(cache_t, keep - t, 1)
    pad = jnp.concatenate([jnp.zeros((LANES - t, new_rows.shape[1]), F32), new_rows], axis=0)
    new_t = pad.T
    lane = lax.broadcasted_iota(jnp.int32, new_t.shape, 1)
    last = jnp.where(lane >= LANES - t, new_t, shifted[:, keep - LANES:])
    return jnp.concatenate([shifted[:, :keep - LANES], last], axis=1)


def _attn_a_sample_kernel(q_ref, kn_ref, vn_ref, knf_ref, vnf_ref, kc_ref, vc_ref, g_ref,
                          brow_ref, o_ref, ko_ref, vo_ref):
    t = q_ref.shape[0]
    keep = kc_ref.shape[1]
    lane = lax.broadcasted_iota(jnp.int32, (t, LANES), 1)
    low_half = lane < A_HEAD_DIM
    pair_cols = [slice(pair * LANES, (pair + 1) * LANES) for pair in range(A_HEADS // 2)]
    scores = []
    for hd in range(A_HEADS):
        cs = pair_cols[hd // 2]
        qp = q_ref[:, cs]
        bias = _toeplitz(brow_ref[hd:hd + 1, :], t)
        qm = jnp.where(low_half if hd % 2 == 0 else ~low_half, qp, jnp.zeros_like(qp))
        scores.append((_dot(qm, kc_ref[cs, :].astype(BF16)) + bias[:, :keep],
                       _nt_dot(qm, kn_ref[:, cs]) + bias[:, keep:keep + t]))
    probs = []
    for s_c, s_n in scores:
        m = jnp.maximum(jnp.max(s_c, axis=-1, keepdims=True), jnp.max(s_n, axis=-1, keepdims=True))
        p_c = jnp.exp(s_c - m)
        p_n = jnp.exp(s_n - m)
        l = jnp.sum(p_c, axis=-1, keepdims=True) + jnp.sum(p_n, axis=-1, keepdims=True)
        probs.append((p_c.astype(BF16), p_n.astype(BF16), l))
    outs = []
    for hd, (p_c, p_n, l) in enumerate(probs):
        cs = pair_cols[hd // 2]
        o = _nt_dot(p_c, vc_ref[cs, :].astype(BF16)) + _dot(p_n, vn_ref[:, cs])
        outs.append(o / l)
    for pair, cs in enumerate(pair_cols):
        o_pair = jnp.where(low_half, outs[2 * pair], outs[2 * pair + 1])
        o_ref[:, cs] = (o_pair * _silu(g_ref[:, cs].astype(F32))).astype(BF16)
    ko_ref[...] = _roll_in(kc_ref[...], knf_ref[...])
    vo_ref[...] = _roll_in(vc_ref[...], vnf_ref[...])


def _attn_a_sample(qa, ka, va, kaf, vaf, cache_kt, cache_vt, ga, bias_rows, *, batch, t):
    keep = cache_kt.shape[2]
    new = pl.BlockSpec((t, A_WIDTH), lambda b: (b, 0))
    cache = pl.BlockSpec((None, A_WIDTH, keep), lambda b: (b, 0, 0))
    return pl.pallas_call(
        _attn_a_sample_kernel,
        grid=(batch,),
        in_specs=[new, new, new, new, new, cache, cache, new,
                  pl.BlockSpec(bias_rows.shape, lambda b: (0, 0))],
        out_specs=[new, cache, cache],
        out_shape=[jax.ShapeDtypeStruct(qa.shape, BF16),
                   jax.ShapeDtypeStruct(cache_kt.shape, F32),
                   jax.ShapeDtypeStruct(cache_vt.shape, F32)],
        compiler_params=pltpu.CompilerParams(
            dimension_semantics=("arbitrary",), vmem_limit_bytes=VMEM_LIMIT),
        name="attn_a_sample",
    )(qa, ka, va, kaf, vaf, cache_kt, cache_vt, ga, bias_rows)


def _attn_b_sample_kernel(q_ref, kn_ref, vn_ref, kc_ref, vc_ref, g_ref,
                          lq1_ref, lk1_ref, lq2_ref, lk2_ref, subg_ref, o_ref, *, lambda_init):
    t = q_ref.shape[0]
    past = kc_ref.shape[1]
    lam = _diff_lambda(lq1_ref, lk1_ref, lq2_ref, lk2_ref, lambda_init)
    head_cols = [slice(h * B_V_DIM, (h + 1) * B_V_DIM) for h in range(B_HEADS)]
    scores = []
    for cs in head_cols:
        qs = _stack_maps(q_ref[:, cs])
        scores.append((_dot(qs, kc_ref[cs, :].astype(BF16)), _nt_dot(qs, kn_ref[:, cs])))
    attn = []
    for s_c, s_n in scores:
        m = jnp.maximum(jnp.max(s_c, axis=-1, keepdims=True),
                        jnp.max(s_n, axis=-1, keepdims=True))
        p_c = jnp.exp(s_c - m)
        p_n = jnp.exp(s_n - m)
        l = jnp.sum(p_c, axis=-1, keepdims=True) + jnp.sum(p_n, axis=-1, keepdims=True)
        p_c = p_c / l
        p_n = p_n / l
        attn.append(((p_c[:t] - lam * p_c[t:]).astype(BF16),
                     (p_n[:t] - lam * p_n[t:]).astype(BF16)))
    for h, (cs, (a_c, a_n)) in enumerate(zip(head_cols, attn)):
        vc = vc_ref[pl.ds(h, past, stride=B_HEADS), :].astype(BF16)
        o = _dot(a_c, vc) + _dot(a_n, vn_ref[:, cs])
        o_ref[:, cs] = _subnorm_gate(o, subg_ref, g_ref[:, cs].astype(F32),
                                     lambda_init).astype(BF16)


def _attn_b_sample(qb, kb, vb, cache_kt, cache_v, gb, lams, subg, *, batch, t, lambda_init):
    past = cache_kt.shape[2]
    new = pl.BlockSpec((t, B_WIDTH), lambda b: (b, 0))
    vec = pl.BlockSpec((1, B_HEAD_DIM), lambda b: (0, 0))
    return pl.pallas_call(
        functools.partial(_attn_b_sample_kernel, lambda_init=lambda_init),
        grid=(batch,),
        in_specs=[new, new, new,
                  pl.BlockSpec((None, B_WIDTH, past), lambda b: (b, 0, 0)),
                  pl.BlockSpec((None, past * B_HEADS, B_V_DIM), lambda b: (b, 0, 0)),
                  new, vec, vec, vec, vec,
                  pl.BlockSpec((1, B_V_DIM), lambda b: (0, 0))],
        out_specs=new,
        out_shape=jax.ShapeDtypeStruct(qb.shape, BF16),
        compiler_params=pltpu.CompilerParams(
            dimension_semantics=("arbitrary",), vmem_limit_bytes=VMEM_LIMIT),
        name="attn_b_sample",
    )(qb, kb, vb, cache_kt, cache_v, gb, *lams, subg)


def _out_kernel(oa_ref, ob_ref, wa_ref, wb_ref, x_ref, fg_ref, y_ref, *, final_norm):
    y = x_ref[...] + _dot(oa_ref[...], wa_ref[...]) + _dot(ob_ref[...], wb_ref[...])
    if final_norm:
        y = y * lax.rsqrt(jnp.mean(y * y, axis=-1, keepdims=True) + EPS) * fg_ref[...]
    y_ref[...] = y


def _out_proj(oa, ob, w_a, w_b, x, final_gain, *, tm, final_norm):
    rows, d_model = x.shape
    row_blk = lambda i: (i, 0)
    const2 = lambda i: (0, 0)
    return pl.pallas_call(
        functools.partial(_out_kernel, final_norm=final_norm),
        grid=(rows // tm,),
        in_specs=[
            pl.BlockSpec((tm, A_WIDTH), row_blk),
            pl.BlockSpec((tm, B_WIDTH), row_blk),
            pl.BlockSpec(w_a.shape, const2),
            pl.BlockSpec(w_b.shape, const2),
            pl.BlockSpec((tm, d_model), row_blk),
            pl.BlockSpec((1, d_model), const2),
        ],
        out_specs=pl.BlockSpec((tm, d_model), row_blk),
        out_shape=jax.ShapeDtypeStruct(x.shape, F32),
        compiler_params=pltpu.CompilerParams(
            dimension_semantics=("arbitrary",), vmem_limit_bytes=VMEM_LIMIT),
        name="out_proj",
    )(oa, ob, w_a, w_b, x, final_gain)


def _channel_major(a):
    n, pos = a.shape[:2]
    return jnp.moveaxis(a.reshape(n, pos, -1), 1, 2)


def _position_major(a, channel_dims):
    n, _, pos = a.shape
    return jnp.moveaxis(a, 2, 1).reshape(n, pos, *channel_dims)


def kernel(x_prompt, x_sample, cache_a_k, cache_a_v, cache_b_k, cache_b_v,
           norm_gain, w_in, w_out, rel_bias, lambda_q1, lambda_k1, lambda_q2, lambda_k2,
           subln_gain, final_gain):
    batch, seq, d_model = x_prompt.shape
    dec_batch, t_sample, _ = x_sample.shape
    depth = w_in.shape[0]
    past_len = cache_b_k.shape[2]
    a_keep = cache_a_k.shape[2]
    keep_prompt = min(BAND_ROWS, seq)
    assert seq % ROW_TILE == 0 and keep_prompt == ROW_TILE and seq % B_TQ == 0
    assert t_sample <= a_keep and t_sample <= LANES and d_model == 2 * SEG

    rows_p = batch * seq
    rows_s = dec_batch * t_sample
    tiles_per_seq = seq // ROW_TILE
    tables_p = _rope_tables(np.arange(ROW_TILE), np.arange(tiles_per_seq) * ROW_TILE)
    tables_s = _rope_tables(past_len + np.arange(rows_s) % t_sample, np.zeros(1))
    a_dims = (A_HEADS, A_HEAD_DIM)
    bk_dims = (B_HEADS, 2, B_HEAD_DIM)
    bv_dims = (B_HEADS, B_V_DIM)

    yp = x_prompt.reshape(rows_p, d_model)
    ys = x_sample.reshape(rows_s, d_model)
    fg = final_gain.reshape(1, d_model)
    outs = [[] for _ in range(8)]
    for l in range(depth):
        lambda_init = 0.8 - 0.6 * math.exp(-0.3 * l)
        last = l == depth - 1
        w_l = w_in[l].astype(BF16)
        w_a = w_out[l, :A_WIDTH].astype(BF16)
        w_b = w_out[l, A_WIDTH:].astype(BF16)
        gain = norm_gain[l].reshape(1, d_model)
        lams = [a[l].reshape(1, B_HEAD_DIM).astype(F32)
                for a in (lambda_q1, lambda_k1, lambda_q2, lambda_k2)]
        subg = subln_gain[l].reshape(1, B_V_DIM).astype(F32)

        qa, ka, va, ga, qb, kb, vb, gb, kaf, vaf, kbf, vbf = _proj(
            yp, gain, w_l, tables_p, tm=ROW_TILE, tiles_per_seq=tiles_per_seq,
            tail_every=tiles_per_seq, channel_major=True, q_scale=LOG2E * B_HEAD_DIM ** -0.5)
        bias_p = _bias_rows(rel_bias[l], 1 - A_WIN, A_QB - 1, lambda t: t + BAND_ROWS, LOG2E)
        ob_lo, ob_hi = _attn_b_prompt(qb, kb, vb, gb, lams, subg, batch=batch, seq=seq,
                                      lambda_init=lambda_init)
        yp = _attn_a_out_prompt(qa, ka, va, ga, bias_p, ob_lo, ob_hi, w_a, w_b, yp, fg,
                                batch=batch, seq=seq, final_norm=last)
        outs[0].append(_position_major(kaf, a_dims))
        outs[1].append(_position_major(vaf, a_dims))
        outs[2].append(_position_major(kbf, bk_dims))
        outs[3].append(vbf.reshape(batch, seq, *bv_dims))

        qa, ka, va, ga, qb, kb, vb, gb, kaf, vaf, kbf, vbf = _proj(
            ys, gain, w_l, tables_s, tm=rows_s, tiles_per_seq=1, tail_every=1,
            channel_major=False, q_scale=B_HEAD_DIM ** -0.5)
        oa, ak_new, av_new = _attn_a_sample(
            qa, ka, va, kaf, vaf, _channel_major(cache_a_k[l]), _channel_major(cache_a_v[l]),
            ga, _bias_rows(rel_bias[l], 1 - t_sample, a_keep + t_sample - 1, lambda t: a_keep - t),
            batch=dec_batch, t=t_sample)
        ob = _attn_b_sample(
            qb, kb, vb, _channel_major(cache_b_k[l]),
            cache_b_v[l].reshape(dec_batch, past_len * B_HEADS, B_V_DIM),
            gb, lams, subg, batch=dec_batch, t=t_sample, lambda_init=lambda_init)
        ys = _out_proj(oa, ob, w_a, w_b, ys, fg, tm=rows_s, final_norm=last)
        outs[4].append(_position_major(ak_new, a_dims))
        outs[5].append(_position_major(av_new, a_dims))
        outs[6].append(kbf.reshape(dec_batch, t_sample, *bk_dims))
        outs[7].append(vbf.reshape(dec_batch, t_sample, *bv_dims))

    return (yp.reshape(batch, seq, d_model), ys.reshape(dec_batch, t_sample, d_model),
            *[jnp.stack(o) for o in outs])
```

```python
import functools
import math

import numpy as np
import jax
import jax.numpy as jnp
from jax import lax
from jax.experimental import pallas as pl
from jax.experimental.pallas import tpu as pltpu

F32 = jnp.float32
BF16 = jnp.bfloat16
NEG_INF = float("-inf")

CHUNK = 64
N_PREV_CHUNKS = 8
A_HEADS = 8
A_HEAD_DIM = 64
A_WIDTH = A_HEADS * A_HEAD_DIM
B_HEADS = 4
B_HEAD_DIM = 64
B_V_DIM = 2 * B_HEAD_DIM
B_WIDTH = B_HEADS * B_V_DIM
SEG = 512
MAX_REL = 128
ROPE_THETA = 500000.0
ROPE_DIM = 16
EPS = 1e-6
LOG2E = math.log2(math.e)
LANES = 128
MXU_TILE = 256
BF16_SUBLANES = 16
VMEM_LIMIT = 56 * 1024 * 1024
B_VMEM_LIMIT = 58 * 1024 * 1024
ROW_TILE = 512
BAND_ROWS = N_PREV_CHUNKS * CHUNK
A_QB = MXU_TILE
A_WIN = BAND_ROWS + A_QB
A_ROWS = 64
A_QK_SPLIT = 3
A_SUM_ROWS = BF16_SUBLANES
B_TQ = 512
B_TK = 512
B_HEADS_PER_STEP = 4
B_STRIP = MXU_TILE
B_ROWS = 64
B_QK_SPLIT = 2
B_SUM_ROWS = BF16_SUBLANES

NT_DIMS = (((1,), (1,)), ((), ()))


def _nt_dot(a, b):
    return lax.dot_general(a, b, NT_DIMS, preferred_element_type=F32)


def _dot(a, b):
    return jnp.dot(a, b, preferred_element_type=F32)


def _silu(g):
    return g / (1.0 + jnp.exp(-g))


def _round_up(n, m):
    return -(-n // m) * m


def _rope_lane_freq():
    d = np.arange(LANES) % B_HEAD_DIM
    inv = ROPE_THETA ** (-np.arange(0, ROPE_DIM, 2, dtype=np.float64) / ROPE_DIM)
    return np.where(d < ROPE_DIM, inv[d % (ROPE_DIM // 2)], 0.0)


def _rope_tables(row_pos, base_pos):
    f = _rope_lane_freq()[None, :]
    ar = np.asarray(row_pos, np.float64)[:, None] * f
    ab = np.asarray(base_pos, np.float64)[:, None] * f
    as32 = lambda a: jnp.asarray(a.astype(np.float32))
    return (as32(np.cos(ar)), as32(np.sin(ar)),
            as32(np.cos(ab))[:, None, :], as32(np.sin(ab))[:, None, :])


def _bias_rows(rel_bias, t_min, t_max, rel_of_t, scale=1.0):
    width = _round_up(t_max - t_min + 1, LANES)
    t = np.arange(width)
    t = np.where(t <= t_max, t, t - width)
    idx = np.clip(rel_of_t(t), -MAX_REL, MAX_REL) + MAX_REL
    return rel_bias.astype(F32)[:, idx] * scale


def _toeplitz(row, n_rows):
    return pltpu.roll(jnp.broadcast_to(row, (n_rows, row.shape[-1])), 0, 1, stride=1, stride_axis=0)


def _proj_kernel(x_ref, g_ref, w_ref, cr_ref, sr_ref, cb_ref, sb_ref, *out_refs,
                 channel_major, q_scale):
    (qa_ref, ka_ref, va_ref, ga_ref, qb_ref, kb_ref, vb_ref, gb_ref,
     kaf_ref, vaf_ref, kbf_ref, vbf_ref) = out_refs
    x = x_ref[...]
    inv = lax.rsqrt(jnp.mean(x * x, axis=-1, keepdims=True) + EPS)
    h = (x * inv * g_ref[...]).astype(BF16)

    def seg(k):
        return _dot(h, w_ref[:, k * SEG:(k + 1) * SEG])

    cb, sb = cb_ref[0], sb_ref[0]
    cr, sr = cr_ref[...], sr_ref[...]
    cos = cb * cr - sb * sr
    sin = sb * cr + cb * sr
    d = lax.broadcasted_iota(jnp.int32, cos.shape, 1) % B_HEAD_DIM
    sin_lo = jnp.where(d < ROPE_DIM // 2, -sin, 0.0)
    sin_hi = jnp.where(d >= ROPE_DIM // 2, sin, 0.0)

    def rope(z):
        cols = []
        for c in range(SEG // LANES):
            zc = z[:, c * LANES:(c + 1) * LANES]
            up = pltpu.roll(zc, LANES - ROPE_DIM // 2, 1)
            dn = pltpu.roll(zc, ROPE_DIM // 2, 1)
            cols.append(zc * cos + up * sin_lo + dn * sin_hi)
        return jnp.concatenate(cols, axis=1)

    qa = seg(0) * q_scale
    qa_ref[...] = (qa.T if channel_major else qa).astype(BF16)
    ka = seg(1)
    ka_ref[...] = ka.astype(BF16)
    va = seg(2)
    va_t = va.T if channel_major else va
    va_ref[...] = va_t.astype(BF16)
    kaf_ref[...] = ka.T if channel_major else ka
    vaf_ref[...] = va_t

    ga_ref[...] = seg(3).astype(BF16)
    qb = rope(seg(4)) * q_scale
    qb_ref[...] = (qb.T if channel_major else qb).astype(BF16)
    kb = rope(seg(5))
    kbf_ref[...] = kb.T if channel_major else kb
    kb_ref[...] = kb.astype(BF16)
    vb = seg(6)
    n_col = SEG // LANES
    for c in range(n_col):
        vbf_ref[pl.ds(c, vb.shape[0], stride=n_col), :] = vb[:, c * LANES:(c + 1) * LANES]
    vb_ref[...] = (vb.T if channel_major else vb).astype(BF16)
    gb_ref[...] = seg(7).astype(BF16)


def _proj(x, gain, w_bf16, tables, *, tm, tiles_per_seq, tail_every, channel_major, q_scale):
    rows, d_model = x.shape
    n_tiles = rows // tm
    n_tail = n_tiles // tail_every
    n_seq = n_tiles // tiles_per_seq
    cr, sr, cb, sb = tables
    row_blk = lambda i: (i, 0)
    const2 = lambda i: (0, 0)
    base_blk = lambda i: (i % tiles_per_seq, 0, 0)
    bf = jax.ShapeDtypeStruct((rows, SEG), BF16)
    f32_full = jax.ShapeDtypeStruct((rows, SEG), F32)
    full_spec = pl.BlockSpec((tm, SEG), row_blk)
    if channel_major:
        assert tm == SEG
        tail = jax.ShapeDtypeStruct((n_tail, SEG, tm), F32)
        tail_spec = pl.BlockSpec((None, SEG, tm), lambda i: (i // tail_every, 0, 0))
        kbf = jax.ShapeDtypeStruct((n_seq, SEG, tiles_per_seq * tm), F32)
        kbf_spec = pl.BlockSpec((None, SEG, tm), lambda i: (i // tiles_per_seq, 0, i % tiles_per_seq))
        bf_t = jax.ShapeDtypeStruct((n_tiles, SEG, tm), BF16)
        bf_t_spec = pl.BlockSpec((None, SEG, tm), lambda i: (i, 0, 0))
    else:
        tail = jax.ShapeDtypeStruct((n_tail * tm, SEG), F32)
        tail_spec = pl.BlockSpec((tm, SEG), lambda i: (i // tail_every, 0))
        kbf, kbf_spec = f32_full, full_spec
        bf_t, bf_t_spec = bf, full_spec
    return pl.pallas_call(
        functools.partial(_proj_kernel, channel_major=channel_major,
                          q_scale=q_scale),
        grid=(n_tiles,),
        in_specs=[
            pl.BlockSpec((tm, d_model), row_blk),
            pl.BlockSpec((1, d_model), const2),
            pl.BlockSpec(w_bf16.shape, const2),
            pl.BlockSpec((tm, LANES), const2),
            pl.BlockSpec((tm, LANES), const2),
            pl.BlockSpec((1, 1, LANES), base_blk),
            pl.BlockSpec((1, 1, LANES), base_blk),
        ],
        out_specs=([bf_t_spec, full_spec, bf_t_spec, full_spec] * 2
                   + [tail_spec, tail_spec, kbf_spec,
                      pl.BlockSpec((tm * SEG // LANES, LANES), row_blk)]),
        out_shape=([bf_t, bf, bf_t, bf] * 2
                   + [tail, tail, kbf, jax.ShapeDtypeStruct((rows * SEG // LANES, LANES), F32)]),
        compiler_params=pltpu.CompilerParams(
            dimension_semantics=("arbitrary",), vmem_limit_bytes=VMEM_LIMIT),
        name="proj",
    )(x, gain, w_bf16, cr, sr, cb, sb)


def _attn_a_prompt_kernel(qt_ref, kp_ref, kc_ref, vtp_ref, vtc_ref, g_ref, brow_ref,
                          ob_lo_ref, ob_hi_ref, wa_ref, wb_ref, x_ref, fg_ref, y_ref,
                          bias_ref, oa_ref, sa_ref, sb_ref, sc_ref, sd_ref, *, final_norm, tiles):
    n = pl.program_id(1)

    @pl.when((pl.program_id(0) == 0) & (n == 0))
    def _():
        j = lax.broadcasted_iota(jnp.int32, (A_WIN, A_QB), 0)
        i = lax.broadcasted_iota(jnp.int32, (A_WIN, A_QB), 1)
        gap = i // CHUNK + N_PREV_CHUNKS - j // CHUNK
        band = (gap >= 0) & (gap <= N_PREV_CHUNKS)
        for hd in range(A_HEADS):
            t = _toeplitz(brow_ref[hd:hd + 1, :], A_WIN)[:, :A_QB]
            bias_ref[hd] = jnp.where(band, t, NEG_INF)

    items = [(q0, hd) for q0 in range(0, ROW_TILE, A_QB) for hd in range(A_HEADS)]
    part = A_WIN // A_QK_SPLIT

    def window(q0, first_tile):
        pieces = []
        for r in range(0, A_WIN, part):
            pos = q0 - BAND_ROWS + r
            if pos >= 0:
                pieces.append((1, pos, r))
            elif not first_tile:
                pieces.append((0, pos + ROW_TILE, r))
        return pieces

    def score_item(t, dst, first_tile):
        q0, hd = items[t]
        lanes = slice(hd // 2 * LANES, (hd // 2 + 1) * LANES)
        qt = qt_ref[lanes, q0:q0 + A_QB]
        row = lax.broadcasted_iota(jnp.int32, qt.shape, 0)
        rhs = jnp.where((row >= A_HEAD_DIM) == bool(hd % 2), qt, jnp.zeros_like(qt))
        for which, start, r in window(q0, first_tile):
            k_ref = (kp_ref, kc_ref)[which]
            dst[r:r + part, :] = (_dot(k_ref[start:start + part, lanes], rhs)
                                  + bias_ref[hd, r:r + part, :])

    def update_item(t, src, first_tile):
        q0, hd = items[t]
        pieces = window(q0, first_tile)
        chunks = [r + c for _, _, r in pieces for c in range(0, part, A_ROWS)]
        m = jnp.max(functools.reduce(jnp.maximum, [src[r:r + A_ROWS, :] for r in chunks]),
                    axis=0, keepdims=True)
        o = None
        for which, start, r in pieces:
            p = jnp.concatenate([jnp.exp2((src[r + c:r + c + A_ROWS, :] - m).astype(BF16))
                                 for c in range(0, part, A_ROWS)], axis=0)
            vt_ref = (vtp_ref, vtc_ref)[which]
            vt1 = jnp.concatenate(
                [vt_ref[hd * A_HEAD_DIM:(hd + 1) * A_HEAD_DIM, start:start + part],
                 jnp.ones((A_SUM_ROWS, part), BF16)], axis=0)
            contrib = _dot(vt1, p)
            o = contrib if o is None else o + contrib
        return o[:A_HEAD_DIM] / o[A_HEAD_DIM:A_HEAD_DIM + 1]

    def residual_rows(q0):
        rows = slice(q0, q0 + A_QB)
        ob = jnp.where(n < tiles // 2, ob_lo_ref[rows, :], ob_hi_ref[rows, :])
        y_ref[rows, :] = x_ref[rows, :] + _dot(ob, wb_ref[...])

    def run(first_tile):
        bufs = (sa_ref, sb_ref, sc_ref, sd_ref)
        depth = len(bufs) - 1
        for t in range(depth):
            score_item(t, bufs[t % len(bufs)], first_tile)
        for q0 in sorted({q0 for q0, _ in items})[:-1]:
            residual_rows(q0)
        o_even = None
        for t, (q0, hd) in enumerate(items):
            if t + depth < len(items):
                score_item(t + depth, bufs[(t + depth) % len(bufs)], first_tile)
            elif t + depth == len(items):
                residual_rows(items[-1][0])
            o_head = update_item(t, bufs[t % len(bufs)], first_tile)
            if hd % 2 == 0:
                o_even = o_head
            else:
                lanes = slice(hd // 2 * LANES, (hd // 2 + 1) * LANES)
                o_pair = jnp.concatenate([o_even, o_head], axis=0).T
                gate = g_ref[q0:q0 + A_QB, lanes].astype(F32)
                oa_ref[q0:q0 + A_QB, lanes] = (o_pair * _silu(gate)).astype(BF16)
            if hd == A_HEADS - 1:
                rows = slice(q0, q0 + A_QB)
                y = y_ref[rows, :] + _dot(oa_ref[rows, :], wa_ref[...])
                if final_norm:
                    y = y * lax.rsqrt(jnp.mean(y * y, axis=-1, keepdims=True) + EPS) * fg_ref[...]
                y_ref[rows, :] = y

    @pl.when(n == 0)
    def _():
        run(True)

    @pl.when(n > 0)
    def _():
        run(False)


def _attn_a_out_prompt(qat, ka, vat, ga, bias_rows, ob_lo, ob_hi, w_a, w_b, x, final_gain, *,
                       batch, seq, final_norm):
    tiles = seq // ROW_TILE
    assert tiles % 2 == 0
    half = tiles // 2
    d_model = x.shape[1]
    cur = lambda b, n: (b * tiles + n, 0)
    prev = lambda b, n: (b * tiles + jnp.maximum(n - 1, 0), 0)
    cur_t = lambda b, n: (b * tiles + n, 0, 0)
    prev_t = lambda b, n: (b * tiles + jnp.maximum(n - 1, 0), 0, 0)
    const2 = lambda b, n: (0, 0)
    blk = (ROW_TILE, A_WIDTH)
    blk_t = (None, A_WIDTH, ROW_TILE)
    return pl.pallas_call(
        functools.partial(_attn_a_prompt_kernel, final_norm=final_norm, tiles=tiles),
        grid=(batch, tiles),
        in_specs=[
            pl.BlockSpec(blk_t, cur_t),
            pl.BlockSpec(blk, prev), pl.BlockSpec(blk, cur),
            pl.BlockSpec(blk_t, prev_t), pl.BlockSpec(blk_t, cur_t),
            pl.BlockSpec(blk, cur),
            pl.BlockSpec(bias_rows.shape, const2),
            pl.BlockSpec((ROW_TILE, B_WIDTH), lambda b, n: (b * half + jnp.minimum(n, half - 1), 0)),
            pl.BlockSpec((ROW_TILE, B_WIDTH), lambda b, n: (b * half + jnp.maximum(n - half, 0), 0)),
            pl.BlockSpec(w_a.shape, const2),
            pl.BlockSpec(w_b.shape, const2),
            pl.BlockSpec((ROW_TILE, d_model), cur),
            pl.BlockSpec((1, d_model), const2),
        ],
        out_specs=pl.BlockSpec((ROW_TILE, d_model), cur),
        out_shape=jax.ShapeDtypeStruct(x.shape, F32),
        scratch_shapes=[pltpu.VMEM((A_HEADS, A_WIN, A_QB), F32),
                        pltpu.VMEM((ROW_TILE, A_WIDTH), BF16),
                        pltpu.VMEM((A_WIN, A_QB), F32),
                        pltpu.VMEM((A_WIN, A_QB), F32),
                        pltpu.VMEM((A_WIN, A_QB), F32),
                        pltpu.VMEM((A_WIN, A_QB), F32)],
        compiler_params=pltpu.CompilerParams(
            dimension_semantics=("arbitrary", "arbitrary"), vmem_limit_bytes=VMEM_LIMIT),
        name="attn_a_out_prompt",
    )(qat, ka, ka, vat, vat, ga, bias_rows, ob_lo, ob_hi, w_a, w_b, x, final_gain)


def _diff_lambda(lq1_ref, lk1_ref, lq2_ref, lk2_ref, lambda_init):
    e1 = jnp.exp(jnp.sum(lq1_ref[...] * lk1_ref[...], axis=-1, keepdims=True))
    e2 = jnp.exp(jnp.sum(lq2_ref[...] * lk2_ref[...], axis=-1, keepdims=True))
    return e1 - e2 + lambda_init


def _stack_maps(qh):
    lane = lax.broadcasted_iota(jnp.int32, qh.shape, 1)
    zero = jnp.zeros_like(qh)
    return jnp.concatenate([jnp.where(lane < B_HEAD_DIM, qh, zero),
                            jnp.where(lane >= B_HEAD_DIM, qh, zero)], axis=0)


def _subnorm_gate(o, subg_ref, gate, lambda_init):
    o = o * lax.rsqrt(jnp.mean(o * o, axis=-1, keepdims=True) + EPS) * subg_ref[...]
    return (o * (1.0 - lambda_init)) * _silu(gate)


def _attn_b_prompt_kernel(qta_ref, qtb_ref, k_ref, vt_ref, ga_ref, gb_ref,
                          lq1_ref, lk1_ref, lq2_ref, lk2_ref, subg_ref, oa_ref, ob_ref,
                          rhs_ref, acc_ref, m_ref, sa_ref, sb_ref, *, lambda_init, tiles):
    g = pl.program_id(2)
    heads = rhs_ref.shape[1]
    for t, qt_ref in enumerate((qta_ref, qtb_ref)):
        for h in range(heads):
            qt = qt_ref[h * B_V_DIM:(h + 1) * B_V_DIM, :]
            row = lax.broadcasted_iota(jnp.int32, qt.shape, 0)
            zero = jnp.zeros_like(qt)
            rhs_ref[t, h, :, :B_TQ] = jnp.where(row < B_HEAD_DIM, qt, zero)
            rhs_ref[t, h, :, B_TQ:] = jnp.where(row >= B_HEAD_DIM, qt, zero)
    m_ref[...] = jnp.full(m_ref.shape, NEG_INF, F32)
    acc_ref[...] = jnp.zeros(acc_ref.shape, F32)

    items = [(h, c) for h in range(heads) for c in range(2 * B_TQ // B_STRIP)]

    def live_keys(c, diagonal):
        q0 = (c * B_STRIP) % B_TQ
        return min(B_TK, q0 + B_STRIP) if diagonal else B_TK

    def score_item(tile, j, n, dst, diagonal):
        h, c = items[n]
        part = B_TK // B_QK_SPLIT
        for r in range(0, live_keys(c, diagonal), part):
            k0 = pl.multiple_of(j * B_TK + r, part)
            dst[n, r:r + part, :] = _dot(k_ref[pl.ds(k0, part), h * B_V_DIM:(h + 1) * B_V_DIM],
                                         rhs_ref[tile, h, :, c * B_STRIP:(c + 1) * B_STRIP])

    def update_item(tile, j, n, src, diagonal):
        h, c = items[n]
        hs = slice(h * B_V_DIM, (h + 1) * B_V_DIM)
        cols = slice(c * B_STRIP, (c + 1) * B_STRIP)
        live = live_keys(c, diagonal)

        def chunk(r):
            s = src[n, r:r + B_ROWS, :]
            if diagonal:
                q0 = (c * B_STRIP) % B_TQ
                kc = (r + lax.broadcasted_iota(jnp.int32, s.shape, 0)) // CHUNK
                qc = (q0 + lax.broadcasted_iota(jnp.int32, s.shape, 1)) // CHUNK
                s = jnp.where(kc <= qc, s, NEG_INF)
            return s

        rows = range(0, live, B_ROWS)
        m_blk = functools.reduce(jnp.maximum, [chunk(r) for r in rows])
        m_old = m_ref[tile, h, :, cols]
        m_new = jnp.maximum(m_old, jnp.max(m_blk, axis=0, keepdims=True))
        alpha = jnp.exp2(m_old - m_new)
        m_ref[tile, h, :, cols] = m_new
        p = jnp.concatenate([jnp.exp2((chunk(r) - m_new).astype(BF16)) for r in rows], axis=0)
        vt1 = jnp.concatenate([vt_ref[j, hs, :live], jnp.ones((B_SUM_ROWS, live), BF16)], axis=0)
        acc_ref[tile, h, :, cols] = alpha * acc_ref[tile, h, :, cols] + _dot(vt1, p)

    def stage(score, dst, update, src, score_diagonal=False, update_diagonal=False):
        for n in range(len(items)):
            if score is not None:
                score_item(*score, n, dst, score_diagonal)
            if update is not None:
                update_item(*update, n, src, update_diagonal)

    buf_a, buf_b = sa_ref, sb_ref
    first = (0, g)
    last = (1, tiles - 1 - g)

    def at(s):
        in_lo = s <= g
        return jnp.where(in_lo, 0, 1), jnp.where(in_lo, s - 1, s - 1 - g)

    def pair(p, carry):
        s = 2 * p + 1
        stage(at(s + 1), buf_a, at(s), buf_b)
        stage(at(s + 2), buf_b, at(s + 1), buf_a)
        return carry

    stage(first, buf_a, None, None, score_diagonal=True)
    stage(at(1), buf_b, first, buf_a, update_diagonal=True)
    lax.fori_loop(0, (tiles - 2) // 2, pair, 0)

    lam = _diff_lambda(lq1_ref, lk1_ref, lq2_ref, lk2_ref, lambda_init)

    def finish(t, g_ref, o_ref):
        for h in range(heads):
            hs = slice(h * B_V_DIM, (h + 1) * B_V_DIM)
            o = acc_ref[t, h, :B_V_DIM, :] / acc_ref[t, h, B_V_DIM:B_V_DIM + 1, :]
            o = (o[:, :B_TQ] - lam * o[:, B_TQ:]).T
            o_ref[:, hs] = _subnorm_gate(o, subg_ref, g_ref[:, hs].astype(F32),
                                         lambda_init).astype(BF16)

    finish(0, ga_ref, oa_ref)
    stage(last, buf_a, (1, tiles - 2 - g), buf_b, score_diagonal=True)
    stage(None, None, last, buf_a, update_diagonal=True)
    finish(1, gb_ref, ob_ref)


def _attn_b_prompt(qbt, kb, vbt, gb, lams, subg, *, batch, seq, lambda_init):
    assert B_TQ == B_TK == ROW_TILE and B_HEADS % B_HEADS_PER_STEP == 0
    tiles = seq // B_TQ
    assert tiles % 2 == 0
    half = tiles // 2
    width = B_HEADS_PER_STEP * B_V_DIM
    n_items = B_HEADS_PER_STEP * 2 * B_TQ // B_STRIP
    lo = lambda b, g: b * tiles + g
    hi = lambda b, g: b * tiles + tiles - 1 - g
    vec = pl.BlockSpec((1, B_HEAD_DIM), lambda b, h, g: (0, 0))
    vbt = vbt.reshape(batch, tiles, B_WIDTH, B_TK)
    out = jax.ShapeDtypeStruct((batch * half * B_TQ, B_WIDTH), BF16)
    return pl.pallas_call(
        functools.partial(_attn_b_prompt_kernel, lambda_init=lambda_init, tiles=tiles),
        grid=(batch, B_HEADS // B_HEADS_PER_STEP, half),
        in_specs=[
            pl.BlockSpec((None, width, B_TQ), lambda b, h, g: (lo(b, g), h, 0)),
            pl.BlockSpec((None, width, B_TQ), lambda b, h, g: (hi(b, g), h, 0)),
            pl.BlockSpec((seq, width), lambda b, h, g: (b, h)),
            pl.BlockSpec((None, tiles, width, B_TK), lambda b, h, g: (b, 0, h, 0),
                         pipeline_mode=pl.Buffered(1)),
            pl.BlockSpec((B_TQ, width), lambda b, h, g: (lo(b, g), h)),
            pl.BlockSpec((B_TQ, width), lambda b, h, g: (hi(b, g), h)),
            vec, vec, vec, vec,
            pl.BlockSpec((1, B_V_DIM), lambda b, h, g: (0, 0)),
        ],
        out_specs=[pl.BlockSpec((B_TQ, width), lambda b, h, g: (b * half + g, h)),
                   pl.BlockSpec((B_TQ, width), lambda b, h, g: (b * half + half - 1 - g, h))],
        out_shape=[out, out],
        scratch_shapes=[pltpu.VMEM((2, B_HEADS_PER_STEP, B_V_DIM, 2 * B_TQ), BF16),
                        pltpu.VMEM((2, B_HEADS_PER_STEP, B_V_DIM + B_SUM_ROWS, 2 * B_TQ), F32),
                        pltpu.VMEM((2, B_HEADS_PER_STEP, 1, 2 * B_TQ), F32),
                        pltpu.VMEM((n_items, B_TK, B_STRIP), F32),
                        pltpu.VMEM((n_items, B_TK, B_STRIP), F32)],
        compiler_params=pltpu.CompilerParams(
            dimension_semantics=("arbitrary", "arbitrary", "arbitrary"),
            vmem_limit_bytes=B_VMEM_LIMIT),
        name="attn_b_prompt",
    )(qbt, qbt, kb, vbt, gb, gb, *lams, subg)


def _roll_in(cache_t, new_rows):
    t = new_rows.shape[0]
    keep = cache_t.shape[1]
    shifted = pltpu.roll(cache_t, keep - t, 1)
    pad = jnp.concatenate([jnp.zeros((LANES - t, new_rows.shape[1]), F32), new_rows], axis=0)
    new_t = pad.T
    lane = lax.broadcasted_iota(jnp.int32, new_t.shape, 1)
    last = jnp.where(lane >= LANES - t, new_t, shifted[:, keep - LANES:])
    return jnp.concatenate([shifted[:, :keep - LANES], last], axis=1)


def _attn_a_sample_kernel(q_ref, kn_ref, vn_ref, knf_ref, vnf_ref, kc_ref, vc_ref, g_ref,
                          brow_ref, o_ref, ko_ref, vo_ref):
    b = pl.program_id(0)
    q_ref, kn_ref, vn_ref, knf_ref, vnf_ref, g_ref = (
        r.at[b] for r in (q_ref, kn_ref, vn_ref, knf_ref, vnf_ref, g_ref))
    t = q_ref.shape[0]
    keep = kc_ref.shape[1]
    lane = lax.broadcasted_iota(jnp.int32, (t, LANES), 1)
    low_half = lane < A_HEAD_DIM
    pair_cols = [slice(pair * LANES, (pair + 1) * LANES) for pair in range(A_HEADS // 2)]
    scores = []
    for hd in range(A_HEADS):
        cs = pair_cols[hd // 2]
        qp = q_ref[:, cs]
        bias = _toeplitz(brow_ref[hd:hd + 1, :], t)
        qm = jnp.where(low_half if hd % 2 == 0 else ~low_half, qp, jnp.zeros_like(qp))
        scores.append((_dot(qm, kc_ref[cs, :].astype(BF16)) + bias[:, :keep],
                       _nt_dot(qm, kn_ref[:, cs]) + bias[:, keep:keep + t]))
    probs = []
    for s_c, s_n in scores:
        m = jnp.maximum(jnp.max(s_c, axis=-1, keepdims=True), jnp.max(s_n, axis=-1, keepdims=True))
        p_c = jnp.exp(s_c - m)
        p_n = jnp.exp(s_n - m)
        l = jnp.sum(p_c, axis=-1, keepdims=True) + jnp.sum(p_n, axis=-1, keepdims=True)
        probs.append((p_c.astype(BF16), p_n.astype(BF16), l))
    outs = []
    for hd, (p_c, p_n, l) in enumerate(probs):
        cs = pair_cols[hd // 2]
        o = _nt_dot(p_c, vc_ref[cs, :].astype(BF16)) + _dot(p_n, vn_ref[:, cs])
        outs.append(o / l)
    for pair, cs in enumerate(pair_cols):
        o_pair = jnp.where(low_half, outs[2 * pair], outs[2 * pair + 1])
        o_ref[:, cs] = (o_pair * _silu(g_ref[:, cs].astype(F32))).astype(BF16)
    ko_ref[...] = _roll_in(kc_ref[...], knf_ref[...])
    vo_ref[...] = _roll_in(vc_ref[...], vnf_ref[...])


def _attn_a_sample(qa, ka, va, kaf, vaf, cache_kt, cache_vt, ga, bias_rows, *, batch, t):
    keep = cache_kt.shape[2]
    per_stream = lambda a: a.reshape(batch, t, A_WIDTH)
    resident = pl.BlockSpec((batch, t, A_WIDTH), lambda b: (0, 0, 0))
    new = pl.BlockSpec((t, A_WIDTH), lambda b: (b, 0))
    cache = pl.BlockSpec((None, A_WIDTH, keep), lambda b: (b, 0, 0))
    return pl.pallas_call(
        _attn_a_sample_kernel,
        grid=(batch,),
        in_specs=[resident, resident, resident, resident, resident, cache, cache, resident,
                  pl.BlockSpec(bias_rows.shape, lambda b: (0, 0))],
        out_specs=[new, cache, cache],
        out_shape=[jax.ShapeDtypeStruct(qa.shape, BF16),
                   jax.ShapeDtypeStruct(cache_kt.shape, F32),
                   jax.ShapeDtypeStruct(cache_vt.shape, F32)],
        compiler_params=pltpu.CompilerParams(
            dimension_semantics=("arbitrary",), vmem_limit_bytes=VMEM_LIMIT),
        name="attn_a_sample",
    )(per_stream(qa), per_stream(ka), per_stream(va), per_stream(kaf), per_stream(vaf),
      cache_kt, cache_vt, per_stream(ga), bias_rows)


def _attn_b_sample_kernel(q_ref, kn_ref, vn_ref, kc_ref, vc_ref, g_ref,
                          lq1_ref, lk1_ref, lq2_ref, lk2_ref, subg_ref, o_ref, *, lambda_init):
    b = pl.program_id(0)
    q_ref, kn_ref, vn_ref, g_ref = (r.at[b] for r in (q_ref, kn_ref, vn_ref, g_ref))
    t = q_ref.shape[0]
    past = kc_ref.shape[1]
    lam = _diff_lambda(lq1_ref, lk1_ref, lq2_ref, lk2_ref, lambda_init)
    head_cols = [slice(h * B_V_DIM, (h + 1) * B_V_DIM) for h in range(B_HEADS)]
    scores = []
    for cs in head_cols:
        qs = _stack_maps(q_ref[:, cs])
        scores.append((_dot(qs, kc_ref[cs, :].astype(BF16)), _nt_dot(qs, kn_ref[:, cs])))
    attn = []
    for s_c, s_n in scores:
        m = jnp.maximum(jnp.max(s_c, axis=-1, keepdims=True),
                        jnp.max(s_n, axis=-1, keepdims=True))
        p_c = jnp.exp(s_c - m)
        p_n = jnp.exp(s_n - m)
        l = jnp.sum(p_c, axis=-1, keepdims=True) + jnp.sum(p_n, axis=-1, keepdims=True)
        p_c = p_c / l
        p_n = p_n / l
        attn.append(((p_c[:t] - lam * p_c[t:]).astype(BF16),
                     (p_n[:t] - lam * p_n[t:]).astype(BF16)))
    for h, (cs, (a_c, a_n)) in enumerate(zip(head_cols, attn)):
        vc = vc_ref[pl.ds(h, past, stride=B_HEADS), :].astype(BF16)
        o = _dot(a_c, vc) + _dot(a_n, vn_ref[:, cs])
        o_ref[:, cs] = _subnorm_gate(o, subg_ref, g_ref[:, cs].astype(F32),
                                     lambda_init).astype(BF16)


def _attn_b_sample(qb, kb, vb, cache_kt, cache_v, gb, lams, subg, *, batch, t, lambda_init):
    past = cache_kt.shape[2]
    per_stream = lambda a: a.reshape(batch, t, B_WIDTH)
    resident = pl.BlockSpec((batch, t, B_WIDTH), lambda b: (0, 0, 0))
    new = pl.BlockSpec((t, B_WIDTH), lambda b: (b, 0))
    vec = pl.BlockSpec((1, B_HEAD_DIM), lambda b: (0, 0))
    return pl.pallas_call(
        functools.partial(_attn_b_sample_kernel, lambda_init=lambda_init),
        grid=(batch,),
        in_specs=[resident, resident, resident,
                  pl.BlockSpec((None, B_WIDTH, past), lambda b: (b, 0, 0)),
                  pl.BlockSpec((None, past * B_HEADS, B_V_DIM), lambda b: (b, 0, 0)),
                  resident, vec, vec, vec, vec,
                  pl.BlockSpec((1, B_V_DIM), lambda b: (0, 0))],
        out_specs=new,
        out_shape=jax.ShapeDtypeStruct(qb.shape, BF16),
        compiler_params=pltpu.CompilerParams(
            dimension_semantics=("arbitrary",), vmem_limit_bytes=VMEM_LIMIT),
        name="attn_b_sample",
    )(per_stream(qb), per_stream(kb), per_stream(vb), cache_kt, cache_v, per_stream(gb),
      *lams, subg)


def _out_kernel(oa_ref, ob_ref, wa_ref, wb_ref, x_ref, fg_ref, y_ref, *, final_norm):
    y = x_ref[...] + _dot(oa_ref[...], wa_ref[...]) + _dot(ob_ref[...], wb_ref[...])
    if final_norm:
        y = y * lax.rsqrt(jnp.mean(y * y, axis=-1, keepdims=True) + EPS) * fg_ref[...]
    y_ref[...] = y


def _out_proj(oa, ob, w_a, w_b, x, final_gain, *, tm, final_norm):
    rows, d_model = x.shape
    row_blk = lambda i: (i, 0)
    const2 = lambda i: (0, 0)
    return pl.pallas_call(
        functools.partial(_out_kernel, final_norm=final_norm),
        grid=(rows // tm,),
        in_specs=[
            pl.BlockSpec((tm, A_WIDTH), row_blk),
            pl.BlockSpec((tm, B_WIDTH), row_blk),
            pl.BlockSpec(w_a.shape, const2),
            pl.BlockSpec(w_b.shape, const2),
            pl.BlockSpec((tm, d_model), row_blk),
            pl.BlockSpec((1, d_model), const2),
        ],
        out_specs=pl.BlockSpec((tm, d_model), row_blk),
        out_shape=jax.ShapeDtypeStruct(x.shape, F32),
        compiler_params=pltpu.CompilerParams(
            dimension_semantics=("arbitrary",), vmem_limit_bytes=VMEM_LIMIT),
        name="out_proj",
    )(oa, ob, w_a, w_b, x, final_gain)


def _channel_major(a):
    n, pos = a.shape[:2]
    return jnp.moveaxis(a.reshape(n, pos, -1), 1, 2)


def _position_major(a, channel_dims):
    n, _, pos = a.shape
    return jnp.moveaxis(a, 2, 1).reshape(n, pos, *channel_dims)


def kernel(x_prompt, x_sample, cache_a_k, cache_a_v, cache_b_k, cache_b_v,
           norm_gain, w_in, w_out, rel_bias, lambda_q1, lambda_k1, lambda_q2, lambda_k2,
           subln_gain, final_gain):
    batch, seq, d_model = x_prompt.shape
    dec_batch, t_sample, _ = x_sample.shape
    depth = w_in.shape[0]
    past_len = cache_b_k.shape[2]
    a_keep = cache_a_k.shape[2]
    keep_prompt = min(BAND_ROWS, seq)
    assert seq % ROW_TILE == 0 and keep_prompt == ROW_TILE and seq % B_TQ == 0
    assert t_sample <= a_keep and t_sample <= LANES and d_model == 2 * SEG

    rows_p = batch * seq
    rows_s = dec_batch * t_sample
    tiles_per_seq = seq // ROW_TILE
    tables_p = _rope_tables(np.arange(ROW_TILE), np.arange(tiles_per_seq) * ROW_TILE)
    tables_s = _rope_tables(past_len + np.arange(rows_s) % t_sample, np.zeros(1))
    a_dims = (A_HEADS, A_HEAD_DIM)
    bk_dims = (B_HEADS, 2, B_HEAD_DIM)
    bv_dims = (B_HEADS, B_V_DIM)

    yp = x_prompt.reshape(rows_p, d_model)
    ys = x_sample.reshape(rows_s, d_model)
    fg = final_gain.reshape(1, d_model)
    outs = [[] for _ in range(8)]
    for l in range(depth):
        lambda_init = 0.8 - 0.6 * math.exp(-0.3 * l)
        last = l == depth - 1
        w_l = w_in[l].astype(BF16)
        w_a = w_out[l, :A_WIDTH].astype(BF16)
        w_b = w_out[l, A_WIDTH:].astype(BF16)
        gain = norm_gain[l].reshape(1, d_model)
        lams = [a[l].reshape(1, B_HEAD_DIM).astype(F32)
                for a in (lambda_q1, lambda_k1, lambda_q2, lambda_k2)]
        subg = subln_gain[l].reshape(1, B_V_DIM).astype(F32)

        qa, ka, va, ga, qb, kb, vb, gb, kaf, vaf, kbf, vbf = _proj(
            yp, gain, w_l, tables_p, tm=ROW_TILE, tiles_per_seq=tiles_per_seq,
            tail_every=tiles_per_seq, channel_major=True, q_scale=LOG2E * B_HEAD_DIM ** -0.5)
        bias_p = _bias_rows(rel_bias[l], 1 - A_WIN, A_QB - 1, lambda t: t + BAND_ROWS, LOG2E)
        ob_lo, ob_hi = _attn_b_prompt(qb, kb, vb, gb, lams, subg, batch=batch, seq=seq,
                                      lambda_init=lambda_init)
        yp = _attn_a_out_prompt(qa, ka, va, ga, bias_p, ob_lo, ob_hi, w_a, w_b, yp, fg,
                                batch=batch, seq=seq, final_norm=last)
        outs[0].append(_position_major(kaf, a_dims))
        outs[1].append(_position_major(vaf, a_dims))
        outs[2].append(_position_major(kbf, bk_dims))
        outs[3].append(vbf.reshape(batch, seq, *bv_dims))

        qa, ka, va, ga, qb, kb, vb, gb, kaf, vaf, kbf, vbf = _proj(
            ys, gain, w_l, tables_s, tm=rows_s, tiles_per_seq=1, tail_every=1,
            channel_major=False, q_scale=B_HEAD_DIM ** -0.5)
        oa, ak_new, av_new = _attn_a_sample(
            qa, ka, va, kaf, vaf, _channel_major(cache_a_k[l]), _channel_major(cache_a_v[l]),
            ga, _bias_rows(rel_bias[l], 1 - t_sample, a_keep + t_sample - 1, lambda t: a_keep - t),
            batch=dec_batch, t=t_sample)
        ob = _attn_b_sample(
            qb, kb, vb, _channel_major(cache_b_k[l]),
            cache_b_v[l].reshape(dec_batch, past_len * B_HEADS, B_V_DIM),
            gb, lams, subg, batch=dec_batch, t=t_sample, lambda_init=lambda_init)
        ys = _out_proj(oa, ob, w_a, w_b, ys, fg, tm=rows_s, final_norm=last)
        outs[4].append(_position_major(ak_new, a_dims))
        outs[5].append(_position_major(av_new, a_dims))
        outs[6].append(kbf.reshape(dec_batch, t_sample, *bk_dims))
        outs[7].append(vbf.reshape(dec_batch, t_sample, *bv_dims))

    return (yp.reshape(batch, seq, d_model), ys.reshape(dec_batch, t_sample, d_model),
            *[jnp.stack(o) for o in outs])
```
